```python
import math
import jax, jax.numpy as jnp
from jax import lax
import numpy as np

D_MODEL = 1024
BATCH = 16
SEQ = 2048
DEPTH = 2
DEC_BATCH = 32
DEC_SEQ = 4
PAST_LEN = 16384
PAGE_SIZE = 128

N_EVEN = (DEPTH + 1) // 2
N_ODD = DEPTH // 2
CONV_WIDTH = 3
N_HEADS = 16
HEAD_DIM = D_MODEL // N_HEADS
N_KV_HEADS = 4
HEADS_PER_GROUP = N_HEADS // N_KV_HEADS
N_BRANCHES = 3
CMP_BLOCK = 32
SEL_BLOCK = 64
CMP_PER_SEL = SEL_BLOCK // CMP_BLOCK
N_SELECTED = 16
WINDOW = 512
N_BUCKETS = 32
MAX_DISTANCE = 128
D_FF = 2816
N_EXPERTS = 8
TOP_K = 2
D_FF_EXPERT = 1408
PLE_DIM = 256
QB_SEL = 16
QB_WIN = 128
EPS = 1e-6
NSA_Q_COLS = N_HEADS * HEAD_DIM
NSA_KV_COLS = N_BRANCHES * N_KV_HEADS * 2 * HEAD_DIM
NSA_IN = NSA_Q_COLS + NSA_KV_COLS + N_BRANCHES * N_HEADS
FORCED_SCORE = 1e4
NEG = -1e30

kernel_name = "hybrid_conv_nsa_moe_ple_step"


def rmsnorm(x, g):
    xf = x.astype(jnp.float32)
    y = xf * lax.rsqrt(jnp.mean(xf * xf, axis=-1, keepdims=True) + EPS)
    return (y * g.astype(jnp.float32)).astype(x.dtype)


def masked_softmax(logits, mask, axis=-1):
    p = jax.nn.softmax(jnp.where(mask, logits, NEG), axis=axis)
    return jnp.where(mask, p, 0.0)


def rel_bucket(dist):
    n = jnp.maximum(dist, 0)
    max_exact = N_BUCKETS // 2
    nf = jnp.maximum(n, 1).astype(jnp.float32)
    large = max_exact + (jnp.log(nf / max_exact) / math.log(MAX_DISTANCE / max_exact)
                         * (N_BUCKETS - max_exact)).astype(jnp.int32)
    large = jnp.minimum(large, N_BUCKETS - 1)
    return jnp.where(n < max_exact, n, large)


def rel_bias_heads(rel_bias, rel):
    bias = rel_bias[rel_bucket(rel)].astype(jnp.float32)
    q_n, k_n = rel.shape
    return bias.reshape(q_n, k_n, N_KV_HEADS, HEADS_PER_GROUP).transpose(2, 3, 0, 1)


def conv_mixer(a, state, w_in, w_conv, w_out):
    gb, gc, v = jnp.split(a @ w_in, 3, axis=-1)
    u = gc * v
    u_ext = jnp.concatenate([state.astype(u.dtype), u], axis=1)
    t = a.shape[1]
    conv = w_conv[0] * u_ext[:, 0:t]
    for k in range(1, CONV_WIDTH):
        conv = conv + w_conv[k] * u_ext[:, k:k + t]
    return (gb * conv) @ w_out, u_ext[:, t:]


def nsa_project(a, w_in, q_gain):
    b, t, _ = a.shape
    proj = a @ w_in
    q = proj[..., :NSA_Q_COLS].reshape(b, t, N_KV_HEADS, HEADS_PER_GROUP, HEAD_DIM)
    q = rmsnorm(q, q_gain) * HEAD_DIM ** -0.5
    kv = proj[..., NSA_Q_COLS:NSA_Q_COLS + NSA_KV_COLS].reshape(b, t, N_BRANCHES, N_KV_HEADS, 2, HEAD_DIM)
    gates = jax.nn.sigmoid(proj[..., NSA_Q_COLS + NSA_KV_COLS:].reshape(
        b, t, N_BRANCHES, N_KV_HEADS, HEADS_PER_GROUP))
    return q, kv[:, :, 0], kv[:, :, 1], kv[:, :, 2], gates


def compress(rows, w_cmp):
    b, L = rows.shape[:2]
    nc = L // CMP_BLOCK
    blocks = rows[:, :nc * CMP_BLOCK].reshape(b, nc, CMP_BLOCK, N_KV_HEADS, 2, HEAD_DIM)
    return jnp.einsum('bnlgcd,lgcde->bngce', blocks, w_cmp)


def cmp_branch(q, kv_c, q_pos, rel_bias, k_gain):
    nc = kv_c.shape[1]
    kc = rmsnorm(kv_c[..., 0, :], k_gain)
    vc = kv_c[..., 1, :]
    end_pos = (jnp.arange(nc) + 1) * CMP_BLOCK - 1
    rel = q_pos[:, None] - end_pos[None, :]
    logits = jnp.einsum('btgrd,bngd->bgrtn', q, kc).astype(jnp.float32) + rel_bias_heads(rel_bias, rel)
    p = masked_softmax(logits, rel >= 0)
    out = jnp.einsum('bgrtn,bngd->btgrd', p.astype(vc.dtype), vc)
    return out, p.sum(axis=2)


def sel_attend(q, k, v, k_pos, q_pos, rel_bias):
    logits = jnp.einsum('btgrd,bgtnd->bgtnr', q, k).astype(jnp.float32)
    bucket = rel_bucket(q_pos[None, None, :, None] - k_pos)
    table = rel_bias.reshape(N_BUCKETS, N_KV_HEADS, HEADS_PER_GROUP)
    bias = table[bucket, jnp.arange(N_KV_HEADS)[None, :, None, None]].astype(jnp.float32)
    mask = (k_pos <= q_pos[None, None, :, None])[..., None]
    p = masked_softmax(logits + bias, mask, axis=3)
    return jnp.einsum('bgtnr,bgtnd->btgrd', p.astype(v.dtype), v)


def sel_branch_prompt(q, kv_s, imp_sel, rel_bias, k_gain):
    b, s = q.shape[:2]
    ns = imp_sel.shape[-1]
    n_top = min(N_SELECTED, ns)
    cur = jnp.arange(s) // SEL_BLOCK
    blk = jnp.arange(ns)[None, :]
    forced = (blk == 0) | (blk == cur[:, None]) | (blk == cur[:, None] - 1)
    valid = blk <= cur[:, None]
    score = jnp.where(forced, FORCED_SCORE, jnp.where(valid, imp_sel, -1.0))
    _, idx = lax.top_k(score, n_top)
    kvn = jnp.stack([rmsnorm(kv_s[..., 0, :], k_gain), kv_s[..., 1, :]], axis=-2)
    kv_blocks = kvn.reshape(b, ns, SEL_BLOCK, N_KV_HEADS, 2, HEAD_DIM).transpose(0, 3, 1, 2, 4, 5)
    nqb = s // QB_SEL
    q_b = q.reshape(b, nqb, QB_SEL, N_KV_HEADS, HEADS_PER_GROUP, HEAD_DIM).swapaxes(0, 1)
    idx_b = idx.reshape(b, N_KV_HEADS, nqb, QB_SEL, n_top).transpose(2, 0, 1, 3, 4)
    bi = jnp.arange(b)[:, None, None, None]
    gi = jnp.arange(N_KV_HEADS)[None, :, None, None]

    def step(args):
        qn, idn, n = args
        rows = kv_blocks[bi, gi, idn]
        rows = rows.reshape(b, N_KV_HEADS, QB_SEL, n_top * SEL_BLOCK, 2, HEAD_DIM)
        k_pos = (idn[..., None] * SEL_BLOCK + jnp.arange(SEL_BLOCK)).reshape(
            b, N_KV_HEADS, QB_SEL, n_top * SEL_BLOCK)
        q_pos = n * QB_SEL + jnp.arange(QB_SEL)
        return sel_attend(qn, rows[..., 0, :], rows[..., 1, :], k_pos, q_pos, rel_bias)

    out = lax.map(step, (q_b, idx_b, jnp.arange(nqb)))
    return out.swapaxes(0, 1).reshape(b, s, N_KV_HEADS, HEADS_PER_GROUP, HEAD_DIM)


def sel_branch_sample(q, kv_new, pool, page_table, imp_sel, past_len, rel_bias, k_gain):
    b, t = q.shape[:2]
    page = pool.shape[1]
    ns = imp_sel.shape[-1]
    n_top = min(N_SELECTED - 1, ns)
    blk = jnp.arange(ns)
    forced = (blk == 0) | (blk == ns - 1)
    score = jnp.where(forced, FORCED_SCORE, imp_sel)
    _, idx = lax.top_k(score, n_top)
    start = idx * SEL_BLOCK
    phys = page_table[jnp.arange(b)[:, None, None, None], start // page]
    off = (start % page)[..., None] + jnp.arange(SEL_BLOCK)
    gi = jnp.arange(N_KV_HEADS)[None, :, None, None, None]
    rows = pool[phys[..., None], off, gi]
    rows = rows.reshape(b, N_KV_HEADS, t, n_top * SEL_BLOCK, 2, HEAD_DIM).astype(kv_new.dtype)
    cur = jnp.broadcast_to(kv_new.transpose(0, 2, 1, 3, 4)[:, :, None], (b, N_KV_HEADS, t, t, 2, HEAD_DIM))
    rows = jnp.concatenate([rows, cur], axis=3)
    q_pos = past_len + jnp.arange(t)
    k_pos = jnp.concatenate([
        (start[..., None] + jnp.arange(SEL_BLOCK)).reshape(b, N_KV_HEADS, t, n_top * SEL_BLOCK),
        jnp.broadcast_to(q_pos, (b, N_KV_HEADS, t, t))], axis=3)
    return sel_attend(q, rmsnorm(rows[..., 0, :], k_gain), rows[..., 1, :], k_pos, q_pos, rel_bias)


def win_branch_prompt(q, kv_w, rel_bias, k_gain):
    b, s = q.shape[:2]
    pad = ((0, 0), (WINDOW, 0), (0, 0), (0, 0))
    kpad = jnp.pad(rmsnorm(kv_w[..., 0, :], k_gain), pad)
    vpad = jnp.pad(kv_w[..., 1, :], pad)
    nqb = s // QB_WIN
    span = WINDOW + QB_WIN
    i = jnp.arange(QB_WIN)
    j = jnp.arange(span)
    rel = i[:, None] + WINDOW - j[None, :]
    band = (rel >= 0) & (rel < WINDOW)
    bias = rel_bias_heads(rel_bias, rel)
    q_b = q.reshape(b, nqb, QB_WIN, N_KV_HEADS, HEADS_PER_GROUP, HEAD_DIM).swapaxes(0, 1)

    def step(args):
        qn, n = args
        kn = lax.dynamic_slice_in_dim(kpad, n * QB_WIN, span, axis=1)
        vn = lax.dynamic_slice_in_dim(vpad, n * QB_WIN, span, axis=1)
        logits = jnp.einsum('bqgrd,bjgd->bgrqj', qn, kn).astype(jnp.float32) + bias
        mask = band & (n * QB_WIN - WINDOW + j >= 0)[None, :]
        p = masked_softmax(logits, mask)
        return jnp.einsum('bgrqj,bjgd->bqgrd', p.astype(vn.dtype), vn)

    out = lax.map(step, (q_b, jnp.arange(nqb)))
    return out.swapaxes(0, 1).reshape(b, s, N_KV_HEADS, HEADS_PER_GROUP, HEAD_DIM)


def win_branch_sample(q, kv_new, buf, past_len, rel_bias, k_gain):
    t = q.shape[1]
    wb = buf.shape[1]
    rows = jnp.concatenate([buf.astype(kv_new.dtype), kv_new], axis=1)
    k = rmsnorm(rows[..., 0, :], k_gain)
    v = rows[..., 1, :]
    q_pos = past_len + jnp.arange(t)
    k_pos = past_len - wb + jnp.arange(wb + t)
    rel = q_pos[:, None] - k_pos[None, :]
    logits = jnp.einsum('btgrd,bjgd->bgrtj', q, k).astype(jnp.float32) + rel_bias_heads(rel_bias, rel)
    p = masked_softmax(logits, (rel >= 0) & (rel < WINDOW))
    out = jnp.einsum('bgrtj,bjgd->btgrd', p.astype(v.dtype), v)
    return out, rows[:, t:]


def nsa_merge(gates, o_c, o_s, o_w, w_out):
    o = gates[:, :, 0, ..., None] * o_c + gates[:, :, 1, ..., None] * o_s + gates[:, :, 2, ..., None] * o_w
    b, t = o.shape[:2]
    return o.reshape(b, t, N_HEADS * HEAD_DIM) @ w_out


def nsa_prompt(a, w_in, w_cmp, q_gain, k_gain, w_out, rel_bias):
    b, s, _ = a.shape
    q, kv_c, kv_s, kv_w, gates = nsa_project(a, w_in, q_gain)
    o_c, imp = cmp_branch(q, compress(kv_c, w_cmp), jnp.arange(s), rel_bias, k_gain[0])
    ns = s // SEL_BLOCK
    imp_sel = imp[..., :ns * CMP_PER_SEL].reshape(b, N_KV_HEADS, s, ns, CMP_PER_SEL).sum(-1)
    o_s = sel_branch_prompt(q, kv_s, imp_sel, rel_bias, k_gain[1])
    o_w = win_branch_prompt(q, kv_w, rel_bias, k_gain[2])
    y = nsa_merge(gates, o_c, o_s, o_w, w_out)
    return y, kv_c, kv_s, kv_w[:, s - min(WINDOW, s):]


def nsa_sample(a, pool_cmp, pool_sel, buf_win, page_table, w_in, w_cmp, q_gain, k_gain, w_out, rel_bias):
    b, t, _ = a.shape
    past_len = page_table.shape[1] * pool_cmp.shape[1]
    q, kv_c, kv_s, kv_w, gates = nsa_project(a, w_in, q_gain)
    q_pos = past_len + jnp.arange(t)
    rows_c = pool_cmp[page_table].reshape(b, past_len, N_KV_HEADS, 2, HEAD_DIM).astype(kv_c.dtype)
    n_extra = (t // CMP_BLOCK) * CMP_BLOCK
    if n_extra:
        rows_c = jnp.concatenate([rows_c, kv_c[:, :n_extra]], axis=1)
    o_c, imp = cmp_branch(q, compress(rows_c, w_cmp), q_pos, rel_bias, k_gain[0])
    ns = past_len // SEL_BLOCK
    imp_sel = imp[..., :ns * CMP_PER_SEL].reshape(b, N_KV_HEADS, t, ns, CMP_PER_SEL).sum(-1)
    o_s = sel_branch_sample(q, kv_s, pool_sel, page_table, imp_sel, past_len, rel_bias, k_gain[1])
    o_w, new_buf = win_branch_sample(q, kv_w, buf_win, past_len, rel_bias, k_gain[2])
    y = nsa_merge(gates, o_c, o_s, o_w, w_out)
    return y, kv_c, kv_s, new_buf


def dense_swiglu(a, w_gate, w_up, w_down):
    return (jax.nn.silu(a @ w_gate) * (a @ w_up)) @ w_down


def moe_swiglu(a, router, w_gate, w_up, w_down):
    logits = (a @ router).astype(jnp.float32)
    top_v, top_i = lax.top_k(logits, TOP_K)
    top_w = jax.nn.softmax(top_v, axis=-1)
    gate = jnp.sum(jax.nn.one_hot(top_i, N_EXPERTS, dtype=jnp.float32) * top_w[..., None], axis=-2)
    out = jnp.zeros_like(a)
    for e in range(N_EXPERTS):
        h = jax.nn.silu(a @ w_gate[e]) * (a @ w_up[e])
        out = out + gate[..., e:e + 1].astype(a.dtype) * (h @ w_down[e])
    return out


def ple_add(h, p_i, gain, w_proj, w_gate):
    g = jax.nn.sigmoid(rmsnorm(h, gain) @ w_gate)
    return h + g * (p_i @ w_proj)


def setup_inputs(seed: int = 0) -> dict:
    key = jax.random.key(seed)
    ks = iter(jax.random.split(key, 40))

    def nrm(shape, scale=1.0):
        return jax.random.normal(next(ks), shape, jnp.float32) * scale

    def gain(shape):
        return 1.0 + 0.1 * nrm(shape)

    n_pages = PAST_LEN // PAGE_SIZE
    n_used = DEC_BATCH * n_pages
    n_pool = n_used + (n_used + 3) // 4
    wb = min(WINDOW, PAST_LEN)
    x_prompt = nrm((BATCH, SEQ, D_MODEL))
    x_sample = nrm((DEC_BATCH, DEC_SEQ, D_MODEL))
    state_conv = nrm((N_EVEN, DEC_BATCH, CONV_WIDTH - 1, D_MODEL))
    cache_cmp = nrm((N_ODD, n_pool, PAGE_SIZE, N_KV_HEADS, 2, HEAD_DIM))
    cache_sel = nrm((N_ODD, n_pool, PAGE_SIZE, N_KV_HEADS, 2, HEAD_DIM))
    state_win = nrm((N_ODD, DEC_BATCH, wb, N_KV_HEADS, 2, HEAD_DIM))
    page_table = jax.random.permutation(next(ks), n_pool)[:n_used].reshape(DEC_BATCH, n_pages).astype(jnp.int32)
    p_prompt = nrm((DEPTH, BATCH, SEQ, PLE_DIM))
    p_sample = nrm((DEPTH, DEC_BATCH, DEC_SEQ, PLE_DIM))
    return {
        "x_prompt": x_prompt, "x_sample": x_sample,
        "state_conv": state_conv, "cache_cmp": cache_cmp, "cache_sel": cache_sel,
        "state_win": state_win, "page_table": page_table,
        "p_prompt": p_prompt, "p_sample": p_sample,
        "norm_mix": gain((DEPTH, D_MODEL)),
        "norm_ffn": gain((DEPTH, D_MODEL)),
        "norm_ple": gain((DEPTH, D_MODEL)),
        "conv_w_in": nrm((N_EVEN, D_MODEL, 3 * D_MODEL), D_MODEL ** -0.5),
        "conv_w": nrm((N_EVEN, CONV_WIDTH, D_MODEL), CONV_WIDTH ** -0.5),
        "conv_w_out": nrm((N_EVEN, D_MODEL, D_MODEL), D_MODEL ** -0.5),
        "nsa_w_in": nrm((N_ODD, D_MODEL, NSA_IN), D_MODEL ** -0.5),
        "nsa_w_cmp": nrm((N_ODD, CMP_BLOCK, N_KV_HEADS, 2, HEAD_DIM, HEAD_DIM), (CMP_BLOCK * HEAD_DIM) ** -0.5),
        "nsa_q_norm": gain((N_ODD, HEAD_DIM)),
        "nsa_k_norm": gain((N_ODD, N_BRANCHES, HEAD_DIM)),
        "nsa_w_out": nrm((N_ODD, N_HEADS * HEAD_DIM, D_MODEL), D_MODEL ** -0.5),
        "rel_bias": nrm((N_BUCKETS, N_HEADS), 0.5),
        "ffn_w_gate": nrm((N_EVEN, D_MODEL, D_FF), D_MODEL ** -0.5),
        "ffn_w_up": nrm((N_EVEN, D_MODEL, D_FF), D_MODEL ** -0.5),
        "ffn_w_down": nrm((N_EVEN, D_FF, D_MODEL), D_FF ** -0.5),
        "moe_router": nrm((N_ODD, D_MODEL, N_EXPERTS), D_MODEL ** -0.5),
        "moe_w_gate": nrm((N_ODD, N_EXPERTS, D_MODEL, D_FF_EXPERT), D_MODEL ** -0.5),
        "moe_w_up": nrm((N_ODD, N_EXPERTS, D_MODEL, D_FF_EXPERT), D_MODEL ** -0.5),
        "moe_w_down": nrm((N_ODD, N_EXPERTS, D_FF_EXPERT, D_MODEL), D_FF_EXPERT ** -0.5),
        "ple_w_proj": nrm((DEPTH, PLE_DIM, D_MODEL), PLE_DIM ** -0.5),
        "ple_w_gate": nrm((DEPTH, D_MODEL, D_MODEL), D_MODEL ** -0.5),
    }


def reference(x_prompt, x_sample, state_conv, cache_cmp, cache_sel, state_win, page_table,
              p_prompt, p_sample, norm_mix, norm_ffn, norm_ple, conv_w_in, conv_w, conv_w_out,
              nsa_w_in, nsa_w_cmp, nsa_q_norm, nsa_k_norm, nsa_w_out, rel_bias,
              ffn_w_gate, ffn_w_up, ffn_w_down, moe_router, moe_w_gate, moe_w_up, moe_w_down,
              ple_w_proj, ple_w_gate):
    h_p, h_s = x_prompt, x_sample
    b_p = x_prompt.shape[0]
    conv_p, conv_s, cmp_p, cmp_s, sel_p, sel_s, win_p, win_s = ([] for _ in range(8))
    for i in range(DEPTH):
        li = i // 2
        a_p = rmsnorm(h_p, norm_mix[i])
        a_s = rmsnorm(h_s, norm_mix[i])
        if i % 2 == 0:
            zero_state = jnp.zeros((b_p, CONV_WIDTH - 1, D_MODEL), x_prompt.dtype)
            y_p, st_p = conv_mixer(a_p, zero_state, conv_w_in[li], conv_w[li], conv_w_out[li])
            y_s, st_s = conv_mixer(a_s, state_conv[li], conv_w_in[li], conv_w[li], conv_w_out[li])
            conv_p.append(st_p)
            conv_s.append(st_s)
        else:
            y_p, c_p, s_p, w_p = nsa_prompt(a_p, nsa_w_in[li], nsa_w_cmp[li], nsa_q_norm[li],
                                            nsa_k_norm[li], nsa_w_out[li], rel_bias)
            y_s, c_s, s_s, w_s = nsa_sample(a_s, cache_cmp[li], cache_sel[li], state_win[li], page_table,
                                            nsa_w_in[li], nsa_w_cmp[li], nsa_q_norm[li],
                                            nsa_k_norm[li], nsa_w_out[li], rel_bias)
            cmp_p.append(c_p)
            cmp_s.append(c_s)
            sel_p.append(s_p)
            sel_s.append(s_s)
            win_p.append(w_p)
            win_s.append(w_s)
        h_p = h_p + y_p
        h_s = h_s + y_s
        f_p = rmsnorm(h_p, norm_ffn[i])
        f_s = rmsnorm(h_s, norm_ffn[i])
        if i % 2 == 0:
            h_p = h_p + dense_swiglu(f_p, ffn_w_gate[li], ffn_w_up[li], ffn_w_down[li])
            h_s = h_s + dense_swiglu(f_s, ffn_w_gate[li], ffn_w_up[li], ffn_w_down[li])
        else:
            h_p = h_p + moe_swiglu(f_p, moe_router[li], moe_w_gate[li], moe_w_up[li], moe_w_down[li])
            h_s = h_s + moe_swiglu(f_s, moe_router[li], moe_w_gate[li], moe_w_up[li], moe_w_down[li])
        h_p = ple_add(h_p, p_prompt[i], norm_ple[i], ple_w_proj[i], ple_w_gate[i])
        h_s = ple_add(h_s, p_sample[i], norm_ple[i], ple_w_proj[i], ple_w_gate[i])
    return (h_p, h_s, jnp.stack(conv_p), jnp.stack(conv_s), jnp.stack(cmp_p), jnp.stack(cmp_s),
            jnp.stack(sel_p), jnp.stack(sel_s), jnp.stack(win_p), jnp.stack(win_s))
```

```python
import functools
import math

import jax
import jax.numpy as jnp
from jax import lax
from jax.experimental import pallas as pl
from jax.experimental.pallas import tpu as pltpu

F32 = jnp.float32
BF16 = jnp.bfloat16

D_MODEL = 1024
N_HEADS = 16
HEAD_DIM = 64
N_KV_HEADS = 4
HEADS_PER_GROUP = 4
N_BRANCHES = 3
CMP_BLOCK = 32
SEL_BLOCK = 64
CMP_PER_SEL = SEL_BLOCK // CMP_BLOCK
N_SELECTED = 16
WINDOW = 512
N_BUCKETS = 32
MAX_DISTANCE = 128
N_EXPERTS = 8
CONV_WIDTH = 3
EPS = 1e-6
FORCED_SCORE = 1e4
NEG = -1e30
KV_COLS = N_KV_HEADS * 2 * HEAD_DIM
GROUP_COLS = 2 * HEAD_DIM
Q_GROUP_COLS = HEADS_PER_GROUP * HEAD_DIM

LANES = 128
SUBLANES = 8
MXU_DIM = 256
VMEM_LIMIT = 56 * 1024 * 1024

TOKEN_TILE = 512
ATTN_TILE = 256
PAGES_PER_STEP = 32


def _cparams(sem, vmem=VMEM_LIMIT):
    return pltpu.CompilerParams(dimension_semantics=sem, vmem_limit_bytes=vmem)


def _const_spec(shape):
    nd = len(shape)
    return pl.BlockSpec(shape, lambda *_: (0,) * nd, pipeline_mode=pl.Buffered(1))


def _smem_spec():
    return pl.BlockSpec(memory_space=pltpu.SMEM)


def _dot(a, b):
    return jnp.dot(a, b, preferred_element_type=F32)


def _dot_nt(a, b):
    return lax.dot_general(a, b, (((1,), (1,)), ((), ())), preferred_element_type=F32)


def _split3(x):
    hi = x.astype(BF16)
    r1 = x - hi.astype(F32)
    mid = r1.astype(BF16)
    lo = (r1 - mid.astype(F32)).astype(BF16)
    return hi, mid, lo


def _exact_dot(x, m01):
    hi, mid, lo = _split3(x)
    return (_dot(hi, m01) + _dot(mid, m01)) + _dot(lo, m01)


def _rmsnorm(x, g):
    ms = jnp.mean(x * x, axis=-1, keepdims=True)
    return x * lax.rsqrt(ms + EPS) * g


def _group_mean_sq(x, gmat):
    n = x.shape[1]
    w = gmat.shape[0]
    outs = []
    for c in range(n // w):
        blk = x[:, c * w:(c + 1) * w]
        sq = blk * blk
        hi = sq.astype(BF16)
        lo = (sq - hi.astype(F32)).astype(BF16)
        outs.append(_dot(hi, gmat) + _dot(lo, gmat))
    out = outs[0] if len(outs) == 1 else jnp.concatenate(outs, axis=1)
    return out * (1.0 / HEAD_DIM)


def _norm_keys(kv, gain_row, gmat):
    ms = _group_mean_sq(kv, gmat)
    lane = lax.broadcasted_iota(jnp.int32, kv.shape, 1)
    is_k = (lane % GROUP_COLS) < HEAD_DIM
    return jnp.where(is_k, kv * lax.rsqrt(ms + EPS) * gain_row, kv)


def _bucket_bias(dist, thr_ref, value_of_bucket):
    val = value_of_bucket(0)
    val = jnp.broadcast_to(val, dist.shape).astype(F32)
    for k in range(1, N_BUCKETS):
        val = jnp.where(dist >= thr_ref[k], value_of_bucket(k), val)
    return val


def _silu(x):
    return x * jax.nn.sigmoid(x)


def _ple(h, p, gain, wpg_ref, wpp_ref):
    r = _rmsnorm(h, gain).astype(BF16)
    g = jax.nn.sigmoid(_dot(r, wpg_ref[...]))
    return h + g * _dot(p.astype(BF16), wpp_ref[...])


def _conv_mix_tail(x, gb, u, um1, um2, cw_ref, wout_ref):
    conv = cw_ref[0:1, :] * um2 + cw_ref[1:2, :] * um1 + cw_ref[2:3, :] * u
    y = _dot((gb * conv).astype(BF16), wout_ref[...])
    return x + y


def _mix0_prompt_kernel(x_ref, g_ref, win_ref, cw_ref, wout_ref, h_ref, st_ref, carry_ref):
    j = pl.program_id(1)
    tm = x_ref.shape[0]

    @pl.when(j == 0)
    def _():
        carry_ref[...] = jnp.zeros_like(carry_ref)

    x = x_ref[...]
    a = _rmsnorm(x, g_ref[...]).astype(BF16)
    proj = _dot(a, win_ref[...])
    gb = proj[:, :D_MODEL]
    u = proj[:, D_MODEL:2 * D_MODEL] * proj[:, 2 * D_MODEL:]
    c0 = carry_ref[SUBLANES - 2:SUBLANES - 1, :]
    c1 = carry_ref[SUBLANES - 1:SUBLANES, :]
    row = lax.broadcasted_iota(jnp.int32, u.shape, 0)
    um1 = jnp.where(row == 0, c1, pltpu.roll(u, 1, 0))
    um2 = jnp.where(row == 0, c0, jnp.where(row == 1, c1, pltpu.roll(u, 2, 0)))
    h_ref[...] = _conv_mix_tail(x, gb, u, um1, um2, cw_ref, wout_ref)
    tail = u[tm - SUBLANES:, :]
    carry_ref[...] = tail
    st_ref[...] = tail


def _mix0_sample_kernel(x_ref, g_ref, win_ref, cw_ref, wout_ref, s1_ref, s2_ref, h_ref, u_ref,
                        *, seq):
    x = x_ref[...]
    a = _rmsnorm(x, g_ref[...]).astype(BF16)
    proj = _dot(a, win_ref[...])
    gb = proj[:, :D_MODEL]
    u = proj[:, D_MODEL:2 * D_MODEL] * proj[:, 2 * D_MODEL:]
    t = lax.broadcasted_iota(jnp.int32, u.shape, 0) % seq
    um1 = jnp.where(t >= 1, pltpu.roll(u, 1, 0), s1_ref[...])
    um2 = jnp.where(t >= 2, pltpu.roll(u, 2, 0), s2_ref[...])
    h_ref[...] = _conv_mix_tail(x, gb, u, um1, um2, cw_ref, wout_ref)
    u_ref[...] = u


def _mix0_prompt(x, gain, w_in, cw, w_out):
    b, s, d = x.shape
    tm = min(TOKEN_TILE, s)
    grid = (b, s // tm)
    return pl.pallas_call(
        _mix0_prompt_kernel,
        grid=grid,
        in_specs=[
            pl.BlockSpec((None, tm, d), lambda i, j: (i, j, 0)),
            _const_spec((1, d)),
            _const_spec(w_in.shape),
            _const_spec(cw.shape),
            _const_spec(w_out.shape),
        ],
        out_specs=[
            pl.BlockSpec((None, tm, d), lambda i, j: (i, j, 0)),
            pl.BlockSpec((None, SUBLANES, d), lambda i, j: (i, 0, 0)),
        ],
        out_shape=[
            jax.ShapeDtypeStruct((b, s, d), F32),
            jax.ShapeDtypeStruct((b, SUBLANES, d), F32),
        ],
        scratch_shapes=[pltpu.VMEM((SUBLANES, d), F32)],
        compiler_params=_cparams(("arbitrary", "arbitrary")),
        name="mix0_prompt",
    )(x, gain, w_in, cw, w_out)


def _mix0_sample(x2d, gain, w_in, cw, w_out, s1, s2, seq):
    n, d = x2d.shape
    return pl.pallas_call(
        functools.partial(_mix0_sample_kernel, seq=seq),
        out_shape=[jax.ShapeDtypeStruct((n, d), F32), jax.ShapeDtypeStruct((n, d), F32)],
        compiler_params=_cparams(None),
        name="mix0_sample",
    )(x2d, gain, w_in, cw, w_out, s1, s2)


def _ffn0_kernel(h_ref, p_ref, gf_ref, wg_ref, wu_ref, wd_ref, gp_ref, wpg_ref, wpp_ref, o_ref,
                 *, chunk):
    h = h_ref[...]
    f = _rmsnorm(h, gf_ref[...]).astype(BF16)
    d_ff = wg_ref.shape[1]
    acc = None
    for c in range(d_ff // chunk):
        sl = slice(c * chunk, (c + 1) * chunk)
        hid = _silu(_dot(f, wg_ref[:, sl])) * _dot(f, wu_ref[:, sl])
        part = _dot(hid.astype(BF16), wd_ref[sl, :])
        acc = part if acc is None else acc + part
    o_ref[...] = _ple(h + acc, p_ref[...], gp_ref[...], wpg_ref, wpp_ref)


def _ffn0(h2d, p2d, gf, wg, wu, wd, gp, wpg, wpp):
    n, d = h2d.shape
    tm = min(TOKEN_TILE, n)
    d_ff = wg.shape[1]
    chunk = d_ff // 2
    assert chunk % LANES == 0
    return pl.pallas_call(
        functools.partial(_ffn0_kernel, chunk=chunk),
        grid=(n // tm,),
        in_specs=[
            pl.BlockSpec((tm, d), lambda i: (i, 0)),
            pl.BlockSpec((tm, p2d.shape[1]), lambda i: (i, 0)),
            _const_spec((1, d)),
            _const_spec(wg.shape), _const_spec(wu.shape), _const_spec(wd.shape),
            _const_spec((1, d)),
            _const_spec(wpg.shape), _const_spec(wpp.shape),
        ],
        out_specs=pl.BlockSpec((tm, d), lambda i: (i, 0)),
        out_shape=jax.ShapeDtypeStruct((n, d), F32),
        compiler_params=_cparams(("arbitrary",)),
        name="ffn0_ple",
    )(h2d, p2d, gf, wg, wu, wd, gp, wpg, wpp)


def _nsa_proj_kernel(h_ref, g_ref, wq_ref, wkv_ref, wgt_ref, gm_ref, qg_ref, kgs_ref, kgw_ref,
                     q_ref, kvc_ref, kvs_ref, kvw_ref, ksn_ref, kwn_ref, gt_ref):
    a = _rmsnorm(h_ref[...], g_ref[...]).astype(BF16)
    gmat = gm_ref[...]
    q = _dot(a, wq_ref[...])
    ms = _group_mean_sq(q, gmat)
    q_ref[...] = ((q * lax.rsqrt(ms + EPS) * qg_ref[...]) * (HEAD_DIM ** -0.5)).astype(q_ref.dtype)
    kv = _dot(a, wkv_ref[...])
    kvc = kv[:, :KV_COLS]
    kvs = kv[:, KV_COLS:2 * KV_COLS]
    kvw = kv[:, 2 * KV_COLS:]
    kvc_ref[...] = kvc
    kvs_ref[...] = kvs
    kvw_ref[...] = kvw
    ksn_ref[...] = _norm_keys(kvs, kgs_ref[...], gmat).astype(ksn_ref.dtype)
    kwn_ref[...] = _norm_keys(kvw, kgw_ref[...], gmat).astype(kwn_ref.dtype)
    gt_ref[...] = jax.nn.sigmoid(_dot(a, wgt_ref[...]))


def _nsa_proj(h2d, gain, wq, wkv, wgt, gmat, qg, kgs, kgw):
    n, d = h2d.shape
    tm = min(TOKEN_TILE, n)
    row = lambda w: pl.BlockSpec((tm, w), lambda i: (i, 0))
    return pl.pallas_call(
        _nsa_proj_kernel,
        grid=(n // tm,),
        in_specs=[
            row(d), _const_spec((1, d)),
            _const_spec(wq.shape), _const_spec(wkv.shape), _const_spec(wgt.shape),
            _const_spec(gmat.shape), _const_spec(qg.shape), _const_spec(kgs.shape),
            _const_spec(kgw.shape),
        ],
        out_specs=[row(d), row(KV_COLS), row(KV_COLS), row(KV_COLS), row(KV_COLS), row(KV_COLS),
                   row(LANES)],
        out_shape=[
            jax.ShapeDtypeStruct((n, d), BF16),
            jax.ShapeDtypeStruct((n, KV_COLS), F32),
            jax.ShapeDtypeStruct((n, KV_COLS), F32),
            jax.ShapeDtypeStruct((n, KV_COLS), F32),
            jax.ShapeDtypeStruct((n, KV_COLS), BF16),
            jax.ShapeDtypeStruct((n, KV_COLS), BF16),
            jax.ShapeDtypeStruct((n, LANES), F32),
        ],
        compiler_params=_cparams(("arbitrary",)),
        name="nsa_proj",
    )(h2d, gain, wq, wkv, wgt, gmat, qg, kgs, kgw)


SLABS = KV_COLS // LANES


def _compress_rows(read_slab, n_blocks, w_ref, gmat, kg_row):
    half = KV_COLS // 2
    per_half = SLABS // 2
    acc = [jnp.zeros((n_blocks, half), F32) for _ in range(2)]
    for l in range(CMP_BLOCK):
        for hf in range(2):
            xl = jnp.concatenate(
                [read_slab(l * SLABS + hf * per_half + c) for c in range(per_half)], axis=1)
            acc[hf] = acc[hf] + _dot(xl.astype(BF16), w_ref[l, hf])
    kv = jnp.concatenate(acc, axis=1)
    return _norm_keys(kv, kg_row, gmat).astype(BF16)


def _compress_kernel(x_ref, w_ref, gm_ref, kg_ref, o_ref):
    nb = o_ref.shape[0]
    read = lambda r: x_ref[pl.ds(r, nb, stride=CMP_BLOCK * SLABS), :]
    o_ref[...] = _compress_rows(read, nb, w_ref, gm_ref[...], kg_ref[...])


def _compress_prompt(rows2d, w2, gmat, kg):
    n = rows2d.shape[0]
    tr = min(4096, n)
    nb = tr // CMP_BLOCK
    return pl.pallas_call(
        _compress_kernel,
        grid=(n // tr,),
        in_specs=[pl.BlockSpec((tr * SLABS, LANES), lambda i: (i, 0)),
                  _const_spec(w2.shape), _const_spec(gmat.shape), _const_spec(kg.shape)],
        out_specs=pl.BlockSpec((nb, KV_COLS), lambda i: (i, 0)),
        out_shape=jax.ShapeDtypeStruct((n // CMP_BLOCK, KV_COLS), BF16),
        compiler_params=_cparams(("arbitrary",)),
        name="compress_prompt",
    )(rows2d.reshape(n * SLABS, LANES), w2, gmat, kg)


def _compress_paged_kernel(pt_ref, pool_ref, w_ref, gm_ref, kg_ref, o_ref, buf_ref, sem_ref,
                           *, pages_per_step, page):
    i = pl.program_id(0)
    n = pl.num_programs(0)
    rows = pages_per_step * page
    page_slabs = page * SLABS

    def page_copy(step, p, slot):
        phys = pt_ref[step * pages_per_step + p]
        return pltpu.make_async_copy(pool_ref.at[phys],
                                     buf_ref.at[slot, pl.ds(p * page_slabs, page_slabs)],
                                     sem_ref.at[slot])

    def start(step, slot):
        def issue(p, c):
            page_copy(step, p, slot).start()
            return c
        lax.fori_loop(0, pages_per_step, issue, 0)

    @pl.when(i == 0)
    def _():
        start(0, 0)

    @pl.when(i + 1 < n)
    def _():
        start(i + 1, (i + 1) % 2)

    slot = i % 2

    def wait(p, c):
        page_copy(i, p, slot).wait()
        return c
    lax.fori_loop(0, pages_per_step, wait, 0)

    nb = rows // CMP_BLOCK
    read = lambda r: buf_ref[slot, pl.ds(r, nb, stride=CMP_BLOCK * SLABS), :]
    o_ref[...] = _compress_rows(read, nb, w_ref, gm_ref[...], kg_ref[...])


def _compress_paged(page_table_flat, pool, w2, gmat, kg, pages_per_step):
    n_pages = page_table_flat.shape[0]
    page = pool.shape[1]
    assert n_pages % pages_per_step == 0
    steps = n_pages // pages_per_step
    nb = pages_per_step * page // CMP_BLOCK
    grid_spec = pltpu.PrefetchScalarGridSpec(
        num_scalar_prefetch=1,
        grid=(steps,),
        in_specs=[pl.BlockSpec(memory_space=pl.ANY),
                  pl.BlockSpec(w2.shape, lambda i, pt: (0, 0, 0, 0)),
                  pl.BlockSpec(gmat.shape, lambda i, pt: (0, 0)),
                  pl.BlockSpec(kg.shape, lambda i, pt: (0, 0))],
        out_specs=pl.BlockSpec((nb, KV_COLS), lambda i, pt: (i, 0)),
        scratch_shapes=[pltpu.VMEM((2, pages_per_step * page * SLABS, LANES), F32),
                        pltpu.SemaphoreType.DMA((2,))],
    )
    pool = pool.reshape(pool.shape[0], page * SLABS, LANES)
    return pl.pallas_call(
        functools.partial(_compress_paged_kernel, pages_per_step=pages_per_step, page=page),
        grid_spec=grid_spec,
        out_shape=jax.ShapeDtypeStruct((steps * nb, KV_COLS), BF16),
        compiler_params=_cparams(("arbitrary",)),
        name="compress_paged",
    )(page_table_flat, pool, w2, gmat, kg)


def _flash_update(qr, kt, vt, bias, mask, m, l, acc):
    s = _dot_nt(qr, kt) + bias
    sm = jnp.where(mask, s, NEG)
    m_new = jnp.maximum(m, jnp.max(sm, axis=-1, keepdims=True))
    alpha = jnp.exp(m - m_new)
    e = jnp.where(mask, jnp.exp(sm - m_new), 0.0)
    l_new = alpha * l + jnp.sum(e, axis=-1, keepdims=True)
    acc_new = alpha * acc + _dot(e.astype(BF16), vt)
    return m_new, l_new, acc_new


def _nsa_prompt_kernel(tbl_ref, thr_ref, q_ref, kc_ref, ks_ref, kw_ref, gt_ref, o_ref, bias_ref,
                       *, n_top):
    g = pl.program_id(0)
    b = pl.program_id(1)
    qi = pl.program_id(2)
    tq = q_ref.shape[0]
    tk = tq
    hpg = HEADS_PER_GROUP
    nc = kc_ref.shape[0]
    ns = nc // CMP_PER_SEL
    blocks_per_tile = tk // SEL_BLOCK

    row_i = lax.broadcasted_iota(jnp.int32, (tq, tk), 0)
    col_i = lax.broadcasted_iota(jnp.int32, (tq, tk), 1)

    @pl.when((b == 0) & (qi == 0))
    def _():
        for r in range(hpg):
            h = g * hpg + r
            for off in range(2):
                dist = row_i - col_i + off * tk
                bias_ref[r, off] = _bucket_bias(dist, thr_ref, lambda k: tbl_ref[k, h])
            bias_ref[r, 2] = jnp.full((tq, tk), tbl_ref[N_BUCKETS - 1, h], F32)

    q = q_ref[...]
    qpos = qi * tq + lax.broadcasted_iota(jnp.int32, (tq, 1), 0)

    kc = kc_ref[...]
    kck = kc[:, :HEAD_DIM]
    kcv = kc[:, HEAD_DIM:]
    n_io = lax.broadcasted_iota(jnp.int32, (tq, nc), 1)
    relc = qpos - ((n_io + 1) * CMP_BLOCK - 1)
    maskc = relc >= 0
    imp = None
    oc = []
    for r in range(hpg):
        h = g * hpg + r
        s = _dot_nt(q[:, r * HEAD_DIM:(r + 1) * HEAD_DIM], kck)
        s = s + _bucket_bias(relc, thr_ref, lambda k: tbl_ref[k, h])
        sm = jnp.where(maskc, s, NEG)
        e = jnp.exp(sm - jnp.max(sm, axis=-1, keepdims=True))
        p = jnp.where(maskc, e / jnp.sum(e, axis=-1, keepdims=True), 0.0)
        imp = p if imp is None else imp + p
        oc.append(_dot(p.astype(BF16), kcv))

    pair = (lax.broadcasted_iota(jnp.int32, (nc, ns), 0) // CMP_PER_SEL
            == lax.broadcasted_iota(jnp.int32, (nc, ns), 1)).astype(BF16)
    imp_sel = _exact_dot(imp, pair)
    blk = lax.broadcasted_iota(jnp.int32, (tq, ns), 1)
    cur = qpos // SEL_BLOCK
    forced = (blk == 0) | (blk == cur) | (blk == cur - 1)
    score = jnp.where(forced, FORCED_SCORE, jnp.where(blk <= cur, imp_sel, -1.0))
    rank = jnp.zeros((tq, ns), F32)
    for i in range(ns):
        ci = score[:, i:i + 1]
        beats = (ci > score) | ((ci == score) & (blk > i))
        rank = rank + jnp.where(beats, 1.0, 0.0)
    sel = jnp.where(rank < n_top, 1.0, 0.0).astype(BF16)

    def init():
        out = []
        for _ in range(hpg):
            out += [jnp.full((tq, 1), NEG, F32), jnp.zeros((tq, 1), F32),
                    jnp.zeros((tq, HEAD_DIM), F32)]
        return tuple(out)

    def run_branch(kv_ref, kt_lo, mask_fn):
        def body(kt, carry):
            kv = kv_ref[pl.ds(pl.multiple_of(kt * tk, tk), tk), :]
            k = kv[:, :HEAD_DIM]
            v = kv[:, HEAD_DIM:]
            kpos = kt * tk + lax.broadcasted_iota(jnp.int32, (1, tk), 1)
            mask = mask_fn(kt, kpos)
            bidx = jnp.minimum(qi - kt, 2)
            new = []
            for r in range(hpg):
                m, l, acc = carry[3 * r:3 * r + 3]
                new += list(_flash_update(q[:, r * HEAD_DIM:(r + 1) * HEAD_DIM], k, v,
                                          bias_ref[r, bidx], mask, m, l, acc))
            return tuple(new)
        carry = lax.fori_loop(kt_lo, qi + 1, body, init())
        return [carry[3 * r + 2] / carry[3 * r + 1] for r in range(hpg)]

    def sel_mask(kt, kpos):
        expand = (lax.broadcasted_iota(jnp.int32, (ns, tk), 0)
                  == kt * blocks_per_tile + lax.broadcasted_iota(jnp.int32, (ns, tk), 1) // SEL_BLOCK)
        chosen = _dot(sel, expand.astype(BF16)) > 0.5
        return chosen & (kpos <= qpos)

    def win_mask(kt, kpos):
        dist = qpos - kpos
        return (dist >= 0) & (dist < WINDOW)

    o_sel = run_branch(ks_ref, 0, sel_mask)
    o_win = run_branch(kw_ref, jnp.maximum(qi - WINDOW // tk, 0), win_mask)

    gates = gt_ref[...]
    n_gate = gates.shape[1]
    idx = lax.broadcasted_iota(jnp.int32, (n_gate, hpg * HEAD_DIM), 0)
    head = g * hpg + lax.broadcasted_iota(jnp.int32, (n_gate, hpg * HEAD_DIM), 1) // HEAD_DIM
    out = None
    for c, branch in enumerate((oc, o_sel, o_win)):
        gate = _exact_dot(gates, (idx == c * N_HEADS + head).astype(BF16))
        term = gate * jnp.concatenate(branch, axis=1)
        out = term if out is None else out + term
    o_ref[...] = out.astype(o_ref.dtype)


def _nsa_prompt_attn(tbl, thr, q, kcn, ksn, kwn, gates, b, s):
    tq = min(ATTN_TILE, s)
    assert WINDOW % tq == 0 and tq + 1 >= MAX_DISTANCE
    nq = s // tq
    nc = s // CMP_BLOCK
    ns = s // SEL_BLOCK
    n_top = min(N_SELECTED, ns)
    return pl.pallas_call(
        functools.partial(_nsa_prompt_kernel, n_top=n_top),
        grid=(N_KV_HEADS, b, nq),
        in_specs=[
            _smem_spec(), _smem_spec(),
            pl.BlockSpec((tq, Q_GROUP_COLS), lambda g, i, j: (i * nq + j, g)),
            pl.BlockSpec((nc, GROUP_COLS), lambda g, i, j: (i, g)),
            pl.BlockSpec((s, GROUP_COLS), lambda g, i, j: (i, g)),
            pl.BlockSpec((s, GROUP_COLS), lambda g, i, j: (i, g)),
            pl.BlockSpec((tq, LANES), lambda g, i, j: (i * nq + j, 0)),
        ],
        out_specs=pl.BlockSpec((tq, Q_GROUP_COLS), lambda g, i, j: (i * nq + j, g)),
        out_shape=jax.ShapeDtypeStruct((b * s, D_MODEL), BF16),
        scratch_shapes=[pltpu.VMEM((HEADS_PER_GROUP, 3, tq, tq), F32)],
        compiler_params=_cparams(("arbitrary", "arbitrary", "arbitrary")),
        name="nsa_prompt_attn",
    )(tbl, thr, q, kcn, ksn, kwn, gates)


def _row_bias(dist_row, thr_ref, tbl_rows):
    n = dist_row.shape[1]
    bucket = jnp.zeros(dist_row.shape, jnp.int32)
    for k in range(1, N_BUCKETS):
        bucket = bucket + jnp.where(dist_row >= thr_ref[k], 1, 0)
    onehot = (lax.broadcasted_iota(jnp.int32, (N_BUCKETS, n), 0) == bucket).astype(BF16)
    return _exact_dot(tbl_rows, onehot)


def _sample_cmp_kernel(thr_ref, q_ref, kc_ref, tbl_ref, oc_ref, idx_ref, *, past_len, seq, n_top):
    nc = kc_ref.shape[0]
    ns = past_len // SEL_BLOCK
    hpg = HEADS_PER_GROUP
    rows = hpg * seq
    kc = kc_ref[...]
    pair = (lax.broadcasted_iota(jnp.int32, (nc, ns), 0) // CMP_PER_SEL
            == lax.broadcasted_iota(jnp.int32, (nc, ns), 1)).astype(BF16)
    eye = (lax.broadcasted_iota(jnp.int32, (ns, ns), 0)
           == lax.broadcasted_iota(jnp.int32, (ns, ns), 1))
    eye_bf = eye.astype(BF16)
    ii = lax.broadcasted_iota(jnp.int32, (ns, ns), 0)
    jj = lax.broadcasted_iota(jnp.int32, (ns, ns), 1)
    t_row = lax.broadcasted_iota(jnp.int32, (rows, nc), 0) % seq
    n_io = lax.broadcasted_iota(jnp.int32, (rows, nc), 1)
    rel = past_len + t_row - ((n_io + 1) * CMP_BLOCK - 1)
    mask = rel >= 0
    blk = lax.broadcasted_iota(jnp.int32, (SUBLANES, ns), 1)
    forced = (blk == 0) | (blk == ns - 1)
    rank_lane = lax.broadcasted_iota(jnp.int32, (ns, LANES), 1).astype(F32)
    blk_col = lax.broadcasted_iota(jnp.int32, (ns, LANES), 0).astype(F32)
    for g in range(N_KV_HEADS):
        kck = kc[:, g * GROUP_COLS:g * GROUP_COLS + HEAD_DIM]
        kcv = kc[:, g * GROUP_COLS + HEAD_DIM:(g + 1) * GROUP_COLS]
        qg = q_ref[g].astype(BF16)
        tbl_g = tbl_ref[g]
        s = _dot_nt(qg, kck)
        s = s + _bucket_bias(rel, thr_ref, lambda k: tbl_g[:, k:k + 1])
        sm = jnp.where(mask, s, NEG)
        e = jnp.exp(sm - jnp.max(sm, axis=-1, keepdims=True))
        p = jnp.where(mask, e / jnp.sum(e, axis=-1, keepdims=True), 0.0)
        oc_ref[g] = _dot(p.astype(BF16), kcv)
        imp = p[0:seq]
        for r in range(1, hpg):
            imp = imp + p[r * seq:(r + 1) * seq]
        imp = jnp.concatenate([imp, jnp.zeros((SUBLANES - seq, nc), F32)], axis=0)
        score = jnp.where(forced, FORCED_SCORE, _exact_dot(imp, pair))
        hi, mid, lo = _split3(score)
        score_t = (_dot_nt(eye_bf, hi) + _dot_nt(eye_bf, mid)) + _dot_nt(eye_bf, lo)
        for t in range(seq):
            s_row = score[t:t + 1, :]
            s_col = score_t[:, t:t + 1]
            beats = (ii != jj) & ((s_row > s_col) | ((s_row == s_col) & (jj < ii)))
            rank_col = jnp.sum(jnp.where(beats, 1.0, 0.0), axis=1, keepdims=True)
            onehot = rank_col == rank_lane
            idx_row = jnp.sum(jnp.where(onehot, blk_col, 0.0), axis=0, keepdims=True)
            idx_ref[g * seq + t:g * seq + t + 1, :] = idx_row.astype(jnp.int32)


def _sample_cmp(thr, q_rt, kcn, tbl_rt, b, past_len, seq):
    nc = kcn.shape[0] // b
    ns = past_len // SEL_BLOCK
    n_top = min(N_SELECTED - 1, ns)
    rows = HEADS_PER_GROUP * seq
    return pl.pallas_call(
        functools.partial(_sample_cmp_kernel, past_len=past_len, seq=seq, n_top=n_top),
        grid=(b,),
        in_specs=[
            _smem_spec(),
            pl.BlockSpec((None, N_KV_HEADS, rows, HEAD_DIM), lambda i: (i, 0, 0, 0)),
            pl.BlockSpec((nc, KV_COLS), lambda i: (i, 0)),
            _const_spec(tbl_rt.shape),
        ],
        out_specs=[
            pl.BlockSpec((None, N_KV_HEADS, rows, HEAD_DIM), lambda i: (i, 0, 0, 0)),
            pl.BlockSpec((None, N_KV_HEADS * seq, LANES), lambda i: (i, 0, 0)),
        ],
        out_shape=[
            jax.ShapeDtypeStruct((b, N_KV_HEADS, rows, HEAD_DIM), F32),
            jax.ShapeDtypeStruct((b, N_KV_HEADS * seq, LANES), jnp.int32),
        ],
        compiler_params=_cparams(("arbitrary",)),
        name="sample_cmp",
    )(thr, q_rt, kcn, tbl_rt)


def _sample_sel_win_kernel(idx_ref, pt_ref, thr_ref, pool_ref, q_ref, ksnew_ref, kwnew_ref,
                           win_ref, tbl_ref, gm_ref, kgs_ref, kgw_ref, os_ref, ow_ref,
                           buf_ref, sem_ref, *, past_len, seq, n_top, page):
    i = pl.program_id(0)
    hpg = HEADS_PER_GROUP
    n_pages = past_len // page
    blocks_per_page = page // SEL_BLOCK
    n_sel = n_top * SEL_BLOCK
    n_idx = N_KV_HEADS * seq * (n_top + 1)

    def block_copy(g, t, j):
        blk = idx_ref[(i * N_KV_HEADS * seq + g * seq + t) * (n_top + 1) + j]
        phys = pt_ref[i * n_pages + blk // blocks_per_page]
        off = (blk % blocks_per_page) * SEL_BLOCK
        return pltpu.make_async_copy(
            pool_ref.at[phys, pl.ds(off, SEL_BLOCK), pl.ds(g * GROUP_COLS, GROUP_COLS)],
            buf_ref.at[g * seq + t, pl.ds(j * SEL_BLOCK, SEL_BLOCK)],
            sem_ref.at[0])

    def for_all_blocks(fn):
        for g in range(N_KV_HEADS):
            def per_tj(tj, c):
                fn(block_copy(g, tj // n_top, tj % n_top))
                return c
            lax.fori_loop(0, seq * n_top, per_tj, 0)

    for_all_blocks(lambda cp: cp.start())

    gmat = gm_ref[...]
    g128 = gmat[:GROUP_COLS, :GROUP_COLS]

    wb = win_ref.shape[0]
    wrows = _norm_keys(win_ref[...], kgw_ref[...], gmat).astype(BF16)
    j_io = lax.broadcasted_iota(jnp.int32, (1, wb), 1)
    tnew = lax.broadcasted_iota(jnp.int32, (1, SUBLANES), 1)

    def window(t, c):
        dist = wb + t - j_io
        maskw = (dist >= 0) & (dist < WINDOW)
        distn = t - tnew
        maskn = (distn >= 0) & (distn < WINDOW)
        for g in range(N_KV_HEADS):
            q = q_ref[g, t].astype(BF16)
            tbl_g = tbl_ref[g]
            k = wrows[:, g * GROUP_COLS:g * GROUP_COLS + HEAD_DIM]
            v = wrows[:, g * GROUP_COLS + HEAD_DIM:(g + 1) * GROUP_COLS]
            knew = kwnew_ref[g]
            s1 = jnp.where(maskw, _dot_nt(q, k) + _row_bias(dist, thr_ref, tbl_g), NEG)
            s2 = jnp.where(maskn, _dot_nt(q, knew[:, :HEAD_DIM].astype(BF16))
                           + _row_bias(distn, thr_ref, tbl_g), NEG)
            m = jnp.maximum(jnp.max(s1, axis=-1, keepdims=True), jnp.max(s2, axis=-1, keepdims=True))
            e1 = jnp.where(maskw, jnp.exp(s1 - m), 0.0)
            e2 = jnp.where(maskn, jnp.exp(s2 - m), 0.0)
            l = jnp.sum(e1, axis=-1, keepdims=True) + jnp.sum(e2, axis=-1, keepdims=True)
            o = _dot(e1.astype(BF16), v) + _dot(e2.astype(BF16), knew[:, HEAD_DIM:].astype(BF16))
            ow_ref[g, t] = o / l
        return c
    lax.fori_loop(0, seq, window, 0)

    for_all_blocks(lambda cp: cp.wait())

    lane_sel = lax.broadcasted_iota(jnp.int32, (1, n_sel), 1)

    def selected(t, c):
        distn = t - tnew
        maskn = distn >= 0
        for g in range(N_KV_HEADS):
            q = q_ref[g, t].astype(BF16)
            tbl_g = tbl_ref[g]
            rows = buf_ref[g * seq + t]
            rows = _norm_keys(rows, kgs_ref[...], g128).astype(BF16)
            base = jnp.zeros((1, n_sel), jnp.int32)
            for j in range(n_top):
                blk = idx_ref[(i * N_KV_HEADS * seq + g * seq + t) * (n_top + 1) + j]
                base = jnp.where(lane_sel // SEL_BLOCK == j, blk * SEL_BLOCK, base)
            dist = past_len + t - (base + lane_sel % SEL_BLOCK)
            knew = ksnew_ref[g]
            s1 = _dot_nt(q, rows[:, :HEAD_DIM]) + _row_bias(dist, thr_ref, tbl_g)
            s2 = jnp.where(maskn, _dot_nt(q, knew[:, :HEAD_DIM].astype(BF16))
                           + _row_bias(distn, thr_ref, tbl_g), NEG)
            m = jnp.maximum(jnp.max(s1, axis=-1, keepdims=True), jnp.max(s2, axis=-1, keepdims=True))
            e1 = jnp.exp(s1 - m)
            e2 = jnp.where(maskn, jnp.exp(s2 - m), 0.0)
            l = jnp.sum(e1, axis=-1, keepdims=True) + jnp.sum(e2, axis=-1, keepdims=True)
            o = _dot(e1.astype(BF16), rows[:, HEAD_DIM:]) + _dot(
                e2.astype(BF16), knew[:, HEAD_DIM:].astype(BF16))
            os_ref[g, t] = o / l
        return c
    lax.fori_loop(0, seq, selected, 0)


def _sample_sel_win(idx_flat, pt_flat, thr, pool, q_tr, ksnew, kwnew, win_rows, tbl_r8, gmat,
                    kgs, kgw, b, past_len, seq):
    ns = past_len // SEL_BLOCK
    n_top = min(N_SELECTED - 1, ns)
    page = pool.shape[1]
    wb = win_rows.shape[0] // b
    o_shape = jax.ShapeDtypeStruct((b, N_KV_HEADS, seq, SUBLANES, HEAD_DIM), F32)
    o_spec = pl.BlockSpec((None, N_KV_HEADS, seq, SUBLANES, HEAD_DIM),
                          lambda i, *_: (i, 0, 0, 0, 0))
    grid_spec = pltpu.PrefetchScalarGridSpec(
        num_scalar_prefetch=3,
        grid=(b,),
        in_specs=[
            pl.BlockSpec(memory_space=pl.ANY),
            pl.BlockSpec((None, N_KV_HEADS, seq, SUBLANES, HEAD_DIM), lambda i, *_: (i, 0, 0, 0, 0)),
            pl.BlockSpec((None, N_KV_HEADS, SUBLANES, GROUP_COLS), lambda i, *_: (i, 0, 0, 0)),
            pl.BlockSpec((None, N_KV_HEADS, SUBLANES, GROUP_COLS), lambda i, *_: (i, 0, 0, 0)),
            pl.BlockSpec((wb, KV_COLS), lambda i, *_: (i, 0)),
            pl.BlockSpec(tbl_r8.shape, lambda i, *_: (0, 0, 0)),
            pl.BlockSpec(gmat.shape, lambda i, *_: (0, 0)),
            pl.BlockSpec(kgs.shape, lambda i, *_: (0, 0)),
            pl.BlockSpec(kgw.shape, lambda i, *_: (0, 0)),
        ],
        out_specs=[o_spec, o_spec],
        scratch_shapes=[pltpu.VMEM((N_KV_HEADS * seq, n_top * SEL_BLOCK, GROUP_COLS), F32),
                        pltpu.SemaphoreType.DMA((1,))],
    )
    return pl.pallas_call(
        functools.partial(_sample_sel_win_kernel, past_len=past_len, seq=seq, n_top=n_top,
                          page=page),
        grid_spec=grid_spec,
        out_shape=[o_shape, o_shape],
        compiler_params=_cparams(("arbitrary",)),
        name="sample_sel_win",
    )(idx_flat, pt_flat, thr, pool, q_tr, ksnew, kwnew, win_rows, tbl_r8, gmat, kgs, kgw)


def _route(f, rhi_ref, rlo_ref):
    f_hi = f.astype(BF16)
    f_lo = (f - f_hi.astype(F32)).astype(BF16)
    logits = (_dot(f_hi, rhi_ref[...]) + _dot(f_lo, rhi_ref[...])) + _dot(f_hi, rlo_ref[...])
    lane = lax.broadcasted_iota(jnp.int32, logits.shape, 1).astype(F32)
    logits = jnp.where(lane < N_EXPERTS, logits, -jnp.inf)
    m1 = jnp.max(logits, axis=-1, keepdims=True)
    i1 = jnp.min(jnp.where(logits == m1, lane, float(LANES)), axis=-1, keepdims=True)
    rest = jnp.where(lane == i1, -jnp.inf, logits)
    m2 = jnp.max(rest, axis=-1, keepdims=True)
    i2 = jnp.min(jnp.where(rest == m2, lane, float(LANES)), axis=-1, keepdims=True)
    e2 = jnp.exp(m2 - m1)
    denom = 1.0 + e2
    return jnp.where(lane == i1, 1.0 / denom, 0.0) + jnp.where(lane == i2, e2 / denom, 0.0)


def _attn_out_kernel(h_ref, o_ref, wout_ref, gf_ref, rhi_ref, rlo_ref, h2_ref, f_ref, gate_ref):
    h2 = h_ref[...] + _dot(o_ref[...], wout_ref[...])
    h2_ref[...] = h2
    f = _rmsnorm(h2, gf_ref[...])
    f_ref[...] = f.astype(f_ref.dtype)
    gate_ref[...] = _route(f, rhi_ref, rlo_ref)


def _attn_out_merge_kernel(h_ref, oc_ref, os_ref, ow_ref, gt_ref, wout_ref, gf_ref, rhi_ref,
                           rlo_ref, h2_ref, f_ref, gate_ref):
    gates = gt_ref[...]
    n_gate = gates.shape[1]
    idx = lax.broadcasted_iota(jnp.int32, (n_gate, D_MODEL), 0)
    head = lax.broadcasted_iota(jnp.int32, (n_gate, D_MODEL), 1) // HEAD_DIM
    o = None
    for c, ref in enumerate((oc_ref, os_ref, ow_ref)):
        gate = _exact_dot(gates, (idx == c * N_HEADS + head).astype(BF16))
        term = gate * ref[...]
        o = term if o is None else o + term
    h2 = h_ref[...] + _dot(o.astype(BF16), wout_ref[...])
    h2_ref[...] = h2
    f = _rmsnorm(h2, gf_ref[...])
    f_ref[...] = f.astype(f_ref.dtype)
    gate_ref[...] = _route(f, rhi_ref, rlo_ref)


def _attn_out(h2d, o_list, gates, wout, gf, rhi, rlo):
    n, d = h2d.shape
    tm = min(TOKEN_TILE, n)
    row = lambda w: pl.BlockSpec((tm, w), lambda i: (i, 0))
    merged = len(o_list) == 1
    body = _attn_out_kernel if merged else _attn_out_merge_kernel
    in_specs = [row(d)] + [row(d)] * len(o_list) + ([] if merged else [row(LANES)])
    in_specs += [_const_spec(wout.shape), _const_spec((1, d)), _const_spec(rhi.shape),
                 _const_spec(rlo.shape)]
    args = [h2d] + list(o_list) + ([] if merged else [gates]) + [wout, gf, rhi, rlo]
    return pl.pallas_call(
        body,
        grid=(n // tm,),
        in_specs=in_specs,
        out_specs=[row(d), row(d), row(LANES)],
        out_shape=[jax.ShapeDtypeStruct((n, d), F32), jax.ShapeDtypeStruct((n, d), BF16),
                   jax.ShapeDtypeStruct((n, LANES), F32)],
        compiler_params=_cparams(("arbitrary",)),
        name="attn_out_route",
    )(*args)


def _moe_kernel(f_ref, h2_ref, gate_ref, p_ref, wg_ref, wu_ref, wd_ref, gp_ref, wpg_ref, wpp_ref,
                o_ref, acc_ref):
    e = pl.program_id(1)

    @pl.when(e == 0)
    def _():
        acc_ref[...] = jnp.zeros_like(acc_ref)

    f = f_ref[...]
    gate = gate_ref[...]
    lane = lax.broadcasted_iota(jnp.int32, gate.shape, 1)
    ge = jnp.sum(jnp.where(lane == e, gate, 0.0), axis=-1, keepdims=True)
    hid = _silu(_dot(f, wg_ref[...])) * _dot(f, wu_ref[...])
    acc_ref[...] += ge * _dot(hid.astype(BF16), wd_ref[...])

    @pl.when(e == pl.num_programs(1) - 1)
    def _():
        o_ref[...] = _ple(h2_ref[...] + acc_ref[...], p_ref[...], gp_ref[...], wpg_ref, wpp_ref)


def _moe(f2d, h2, gate, p2d, wg, wu, wd, gp, wpg, wpp):
    n, d = h2.shape
    tm = min(TOKEN_TILE, n)
    n_e, _, dff = wg.shape
    row = lambda w: pl.BlockSpec((tm, w), lambda i, e: (i, 0))
    return pl.pallas_call(
        _moe_kernel,
        grid=(n // tm, n_e),
        in_specs=[
            row(d), row(d), row(LANES), row(p2d.shape[1]),
            pl.BlockSpec((None, d, dff), lambda i, e: (e, 0, 0)),
            pl.BlockSpec((None, d, dff), lambda i, e: (e, 0, 0)),
            pl.BlockSpec((None, dff, d), lambda i, e: (e, 0, 0)),
            _const_spec((1, d)), _const_spec(wpg.shape), _const_spec(wpp.shape),
        ],
        out_specs=row(d),
        out_shape=jax.ShapeDtypeStruct((n, d), F32),
        scratch_shapes=[pltpu.VMEM((tm, d), F32)],
        compiler_params=_cparams(("arbitrary", "arbitrary")),
        name="moe_ple",
    )(f2d, h2, gate, p2d, wg, wu, wd, gp, wpg, wpp)


def _bucket_thresholds():
    n = jnp.arange(MAX_DISTANCE + 1)
    max_exact = N_BUCKETS // 2
    nf = jnp.maximum(n, 1).astype(F32)
    large = max_exact + (jnp.log(nf / max_exact) / math.log(MAX_DISTANCE / max_exact)
                         * (N_BUCKETS - max_exact)).astype(jnp.int32)
    bucket = jnp.where(n < max_exact, n, jnp.minimum(large, N_BUCKETS - 1))
    return jnp.sum(bucket[None, :] < jnp.arange(N_BUCKETS)[:, None], axis=1).astype(jnp.int32)


def _row(v):
    return v.reshape(1, -1).astype(F32)


def _key_gain_row(k_gain, cols):
    g = jnp.concatenate([k_gain, jnp.ones_like(k_gain)])
    return jnp.tile(g, cols // GROUP_COLS).reshape(1, cols).astype(F32)


def kernel(x_prompt, x_sample, state_conv, cache_cmp, cache_sel, state_win, page_table,
           p_prompt, p_sample, norm_mix, norm_ffn, norm_ple, conv_w_in, conv_w, conv_w_out,
           nsa_w_in, nsa_w_cmp, nsa_q_norm, nsa_k_norm, nsa_w_out, rel_bias,
           ffn_w_gate, ffn_w_up, ffn_w_down, moe_router, moe_w_gate, moe_w_up, moe_w_down,
           ple_w_proj, ple_w_gate):
    bp, sp, d = x_prompt.shape
    bs, ss, _ = x_sample.shape
    n_p = bp * sp
    n_s = bs * ss
    page = cache_cmp.shape[2]
    past_len = page_table.shape[1] * page
    bf = lambda w: w.astype(BF16)

    cw = jnp.zeros((SUBLANES, d), F32).at[:CONV_WIDTH].set(conv_w[0])
    w_in0, w_out0 = bf(conv_w_in[0]), bf(conv_w_out[0])
    g_mix0 = _row(norm_mix[0])
    h_p, tail_p = _mix0_prompt(x_prompt, g_mix0, w_in0, cw, w_out0)
    conv_prompt = tail_p[:, SUBLANES - (CONV_WIDTH - 1):][None]

    st = state_conv[0]
    zeros = jnp.zeros((bs, ss, d), F32)
    s1 = zeros.at[:, 0].set(st[:, 1]).reshape(n_s, d)
    s2 = zeros.at[:, 0].set(st[:, 0]).at[:, 1].set(st[:, 1]).reshape(n_s, d)
    h_s, u_s = _mix0_sample(x_sample.reshape(n_s, d), g_mix0, w_in0, cw, w_out0, s1, s2, ss)
    conv_sample = u_s.reshape(bs, ss, d)[:, ss - (CONV_WIDTH - 1):][None]

    ffn0_w = (_row(norm_ffn[0]), bf(ffn_w_gate[0]), bf(ffn_w_up[0]), bf(ffn_w_down[0]),
              _row(norm_ple[0]), bf(ple_w_gate[0]), bf(ple_w_proj[0]))
    h_p = _ffn0(h_p.reshape(n_p, d), p_prompt[0].reshape(n_p, -1), *ffn0_w)
    h_s = _ffn0(h_s, p_sample[0].reshape(n_s, -1), *ffn0_w)

    q_cols = N_HEADS * HEAD_DIM
    kv_cols = N_BRANCHES * KV_COLS
    w_in1 = nsa_w_in[0]
    wq = bf(w_in1[:, :q_cols])
    wkv = bf(w_in1[:, q_cols:q_cols + kv_cols])
    n_gate = N_BRANCHES * N_HEADS
    wgt = bf(jnp.zeros((d, LANES), F32).at[:, :n_gate].set(w_in1[:, q_cols + kv_cols:]))
    gi = jnp.arange(MXU_DIM) // HEAD_DIM
    gmat = (gi[:, None] == gi[None, :]).astype(BF16)
    qg = jnp.tile(nsa_q_norm[0], N_HEADS).reshape(1, q_cols).astype(F32)
    kgc = _key_gain_row(nsa_k_norm[0, 0], KV_COLS)
    kgs = _key_gain_row(nsa_k_norm[0, 1], KV_COLS)
    kgw = _key_gain_row(nsa_k_norm[0, 2], KV_COLS)
    kgs_group = _key_gain_row(nsa_k_norm[0, 1], GROUP_COLS)
    g_mix1 = _row(norm_mix[1])
    proj_w = (g_mix1, wq, wkv, wgt, gmat, qg, kgs, kgw)

    wc = nsa_w_cmp[0].reshape(CMP_BLOCK, N_KV_HEADS * 2, HEAD_DIM, HEAD_DIM)
    per_tile = MXU_DIM // HEAD_DIM
    wc = wc.reshape(CMP_BLOCK, 2, per_tile, HEAD_DIM, HEAD_DIM)
    eye = jnp.eye(per_tile, dtype=F32)
    w2 = bf(jnp.einsum('lhade,ab->lhadbe', wc, eye).reshape(CMP_BLOCK, 2, MXU_DIM, MXU_DIM))

    thr = _bucket_thresholds()
    tbl = rel_bias.astype(F32)

    q_p, kvc_p, kvs_p, kvw_p, ksn_p, kwn_p, gt_p = _nsa_proj(h_p, *proj_w)
    kcn_p = _compress_prompt(kvc_p, w2, gmat, kgc)
    o_p = _nsa_prompt_attn(tbl, thr, q_p, kcn_p, ksn_p, kwn_p, gt_p, bp, sp)

    q_s, kvc_s, kvs_s, kvw_s, ksn_s, kwn_s, gt_s = _nsa_proj(h_s, *proj_w)
    pt_flat = page_table.reshape(-1).astype(jnp.int32)
    pool_cmp = cache_cmp[0].reshape(cache_cmp.shape[1], page, KV_COLS)
    pool_sel = cache_sel[0].reshape(cache_sel.shape[1], page, KV_COLS)
    kcn_s = _compress_paged(pt_flat, pool_cmp, w2, gmat, kgc, PAGES_PER_STEP)
    hpg = HEADS_PER_GROUP
    q5 = q_s.astype(F32).reshape(bs, ss, N_KV_HEADS, hpg, HEAD_DIM)
    q_rt = q5.transpose(0, 2, 3, 1, 4).reshape(bs, N_KV_HEADS, hpg * ss, HEAD_DIM)
    q_tr = jnp.zeros((bs, N_KV_HEADS, ss, SUBLANES, HEAD_DIM), F32).at[:, :, :, :hpg].set(
        q5.transpose(0, 2, 1, 3, 4))
    tbl_gr = tbl.T.reshape(N_KV_HEADS, hpg, N_BUCKETS)
    tbl_rt = jnp.repeat(tbl_gr, ss, axis=1)
    tbl_r8 = jnp.zeros((N_KV_HEADS, SUBLANES, N_BUCKETS), F32).at[:, :hpg].set(tbl_gr)
    oc_s, idx = _sample_cmp(thr, q_rt, kcn_s, tbl_rt, bs, past_len, ss)
    n_top_s = min(N_SELECTED - 1, past_len // SEL_BLOCK)
    idx_flat = idx[:, :, :n_top_s + 1].reshape(-1)

    def new_rows(kn):
        r = kn.astype(F32).reshape(bs, ss, N_KV_HEADS, GROUP_COLS).transpose(0, 2, 1, 3)
        return jnp.zeros((bs, N_KV_HEADS, SUBLANES, GROUP_COLS), F32).at[:, :, :ss].set(r)

    wb = state_win.shape[2]
    win_rows = state_win[0].reshape(bs * wb, KV_COLS)
    os_s, ow_s = _sample_sel_win(idx_flat, pt_flat, thr, pool_sel, q_tr, new_rows(ksn_s),
                                 new_rows(kwn_s), win_rows, tbl_r8, gmat, kgs_group, kgw,
                                 bs, past_len, ss)
    oc_s2 = oc_s.reshape(bs, N_KV_HEADS, hpg, ss, HEAD_DIM).transpose(0, 3, 1, 2, 4).reshape(n_s, d)
    to2d = lambda o: o[:, :, :, :hpg].transpose(0, 2, 1, 3, 4).reshape(n_s, d)

    w_out1 = bf(nsa_w_out[0])
    router = jnp.zeros((d, LANES), F32).at[:, :N_EXPERTS].set(moe_router[0])
    r_hi = bf(router)
    r_lo = bf(router - r_hi.astype(F32))
    gf1 = _row(norm_ffn[1])
    moe_w = (bf(moe_w_gate[0]), bf(moe_w_up[0]), bf(moe_w_down[0]), _row(norm_ple[1]),
             bf(ple_w_gate[1]), bf(ple_w_proj[1]))
    h2_p, f_p, gate_p = _attn_out(h_p, [o_p], None, w_out1, gf1, r_hi, r_lo)
    y_p = _moe(f_p, h2_p, gate_p, p_prompt[1].reshape(n_p, -1), *moe_w)
    h2_s, f_s, gate_s = _attn_out(h_s, [oc_s2, to2d(os_s), to2d(ow_s)], gt_s, w_out1, gf1,
                                  r_hi, r_lo)
    y_s = _moe(f_s, h2_s, gate_s, p_sample[1].reshape(n_s, -1), *moe_w)

    kv_shape = lambda b, s: (1, b, s, N_KV_HEADS, 2, HEAD_DIM)
    win_p = kvw_p.reshape(bp, sp, KV_COLS)[:, sp - min(WINDOW, sp):]
    win_s = jnp.concatenate([state_win[0].reshape(bs, wb, KV_COLS),
                             kvw_s.reshape(bs, ss, KV_COLS)], axis=1)[:, ss:]
    return (y_p.reshape(bp, sp, d), y_s.reshape(bs, ss, d), conv_prompt, conv_sample,
            kvc_p.reshape(kv_shape(bp, sp)), kvc_s.reshape(kv_shape(bs, ss)),
            kvs_p.reshape(kv_shape(bp, sp)), kvs_s.reshape(kv_shape(bs, ss)),
            win_p.reshape(kv_shape(bp, win_p.shape[1])),
            win_s.reshape(kv_shape(bs, win_s.shape[1])))
```

```python
import functools
import math

import jax
import jax.numpy as jnp
from jax import lax
from jax.experimental import pallas as pl
from jax.experimental.pallas import tpu as pltpu

F32 = jnp.float32
BF16 = jnp.bfloat16

D_MODEL = 1024
N_HEADS = 16
HEAD_DIM = 64
N_KV_HEADS = 4
HEADS_PER_GROUP = 4
N_BRANCHES = 3
CMP_BLOCK = 32
SEL_BLOCK = 64
CMP_PER_SEL = SEL_BLOCK // CMP_BLOCK
N_SELECTED = 16
WINDOW = 512
N_BUCKETS = 32
MAX_DISTANCE = 128
N_EXPERTS = 8
CONV_WIDTH = 3
EPS = 1e-6
FORCED_SCORE = 1e4
NEG = -1e30
KV_COLS = N_KV_HEADS * 2 * HEAD_DIM
GROUP_COLS = 2 * HEAD_DIM
Q_GROUP_COLS = HEADS_PER_GROUP * HEAD_DIM

LANES = 128
SUBLANES = 8
MXU_DIM = 256
VMEM_LIMIT = 56 * 1024 * 1024

TOKEN_TILE = 512
ATTN_TILE = 256
PAGES_PER_STEP = 32


def _cparams(sem, vmem=VMEM_LIMIT):
    return pltpu.CompilerParams(dimension_semantics=sem, vmem_limit_bytes=vmem)


def _const_spec(shape):
    nd = len(shape)
    return pl.BlockSpec(shape, lambda *_: (0,) * nd, pipeline_mode=pl.Buffered(1))


def _smem_spec():
    return pl.BlockSpec(memory_space=pltpu.SMEM)


def _dot(a, b):
    return jnp.dot(a, b, preferred_element_type=F32)


def _dot_nt(a, b):
    return lax.dot_general(a, b, (((1,), (1,)), ((), ())), preferred_element_type=F32)


def _split3(x):
    hi = x.astype(BF16)
    r1 = x - hi.astype(F32)
    mid = r1.astype(BF16)
    lo = (r1 - mid.astype(F32)).astype(BF16)
    return hi, mid, lo


def _exact_dot(x, m01):
    hi, mid, lo = _split3(x)
    return (_dot(hi, m01) + _dot(mid, m01)) + _dot(lo, m01)


def _rmsnorm(x, g):
    ms = jnp.mean(x * x, axis=-1, keepdims=True)
    return x * lax.rsqrt(ms + EPS) * g


def _group_mean_sq(x, gmat):
    n = x.shape[1]
    w = gmat.shape[0]
    outs = []
    for c in range(n // w):
        blk = x[:, c * w:(c + 1) * w]
        sq = blk * blk
        hi = sq.astype(BF16)
        lo = (sq - hi.astype(F32)).astype(BF16)
        outs.append(_dot(hi, gmat) + _dot(lo, gmat))
    out = outs[0] if len(outs) == 1 else jnp.concatenate(outs, axis=1)
    return out * (1.0 / HEAD_DIM)


def _norm_keys(kv, gain_row, gmat):
    ms = _group_mean_sq(kv, gmat)
    lane = lax.broadcasted_iota(jnp.int32, kv.shape, 1)
    is_k = (lane % GROUP_COLS) < HEAD_DIM
    return jnp.where(is_k, kv * lax.rsqrt(ms + EPS) * gain_row, kv)


def _bucket_bias(dist, thr_ref, value_of_bucket):
    val = value_of_bucket(0)
    val = jnp.broadcast_to(val, dist.shape).astype(F32)
    for k in range(1, N_BUCKETS):
        val = jnp.where(dist >= thr_ref[k], value_of_bucket(k), val)
    return val


def _silu(x):
    return x * jax.nn.sigmoid(x)


def _ple(h, p, gain, wpg_ref, wpp_ref):
    r = _rmsnorm(h, gain).astype(BF16)
    g = jax.nn.sigmoid(_dot(r, wpg_ref[...]))
    return h + g * _dot(p.astype(BF16), wpp_ref[...])


def _conv_mix_tail(x, gb, u, um1, um2, cw_ref, wout_ref):
    conv = cw_ref[0:1, :] * um2 + cw_ref[1:2, :] * um1 + cw_ref[2:3, :] * u
    y = _dot((gb * conv).astype(BF16), wout_ref[...])
    return x + y


def _mix0_prompt_kernel(x_ref, g_ref, win_ref, cw_ref, wout_ref, h_ref, st_ref, carry_ref):
    j = pl.program_id(1)
    tm = x_ref.shape[0]

    @pl.when(j == 0)
    def _():
        carry_ref[...] = jnp.zeros_like(carry_ref)

    x = x_ref[...]
    a = _rmsnorm(x, g_ref[...]).astype(BF16)
    proj = _dot(a, win_ref[...])
    gb = proj[:, :D_MODEL]
    u = proj[:, D_MODEL:2 * D_MODEL] * proj[:, 2 * D_MODEL:]
    c0 = carry_ref[SUBLANES - 2:SUBLANES - 1, :]
    c1 = carry_ref[SUBLANES - 1:SUBLANES, :]
    row = lax.broadcasted_iota(jnp.int32, u.shape, 0)
    um1 = jnp.where(row == 0, c1, pltpu.roll(u, 1, 0))
    um2 = jnp.where(row == 0, c0, jnp.where(row == 1, c1, pltpu.roll(u, 2, 0)))
    h_ref[...] = _conv_mix_tail(x, gb, u, um1, um2, cw_ref, wout_ref)
    tail = u[tm - SUBLANES:, :]
    carry_ref[...] = tail
    st_ref[...] = tail


def _mix0_sample_kernel(x_ref, g_ref, win_ref, cw_ref, wout_ref, s1_ref, s2_ref, h_ref, u_ref,
                        *, seq):
    x = x_ref[...]
    a = _rmsnorm(x, g_ref[...]).astype(BF16)
    proj = _dot(a, win_ref[...])
    gb = proj[:, :D_MODEL]
    u = proj[:, D_MODEL:2 * D_MODEL] * proj[:, 2 * D_MODEL:]
    t = lax.broadcasted_iota(jnp.int32, u.shape, 0) % seq
    um1 = jnp.where(t >= 1, pltpu.roll(u, 1, 0), s1_ref[...])
    um2 = jnp.where(t >= 2, pltpu.roll(u, 2, 0), s2_ref[...])
    h_ref[...] = _conv_mix_tail(x, gb, u, um1, um2, cw_ref, wout_ref)
    u_ref[...] = u


def _mix0_prompt(x, gain, w_in, cw, w_out):
    b, s, d = x.shape
    tm = min(TOKEN_TILE, s)
    grid = (b, s // tm)
    return pl.pallas_call(
        _mix0_prompt_kernel,
        grid=grid,
        in_specs=[
            pl.BlockSpec((None, tm, d), lambda i, j: (i, j, 0)),
            _const_spec((1, d)),
            _const_spec(w_in.shape),
            _const_spec(cw.shape),
            _const_spec(w_out.shape),
        ],
        out_specs=[
            pl.BlockSpec((None, tm, d), lambda i, j: (i, j, 0)),
            pl.BlockSpec((None, SUBLANES, d), lambda i, j: (i, 0, 0)),
        ],
        out_shape=[
            jax.ShapeDtypeStruct((b, s, d), F32),
            jax.ShapeDtypeStruct((b, SUBLANES, d), F32),
        ],
        scratch_shapes=[pltpu.VMEM((SUBLANES, d), F32)],
        compiler_params=_cparams(("arbitrary", "arbitrary")),
        name="mix0_prompt",
    )(x, gain, w_in, cw, w_out)


def _mix0_sample(x2d, gain, w_in, cw, w_out, s1, s2, seq):
    n, d = x2d.shape
    return pl.pallas_call(
        functools.partial(_mix0_sample_kernel, seq=seq),
        out_shape=[jax.ShapeDtypeStruct((n, d), F32), jax.ShapeDtypeStruct((n, d), F32)],
        compiler_params=_cparams(None),
        name="mix0_sample",
    )(x2d, gain, w_in, cw, w_out, s1, s2)


def _ffn0_kernel(h_ref, p_ref, gf_ref, wg_ref, wu_ref, wd_ref, gp_ref, wpg_ref, wpp_ref, o_ref,
                 *, chunk):
    h = h_ref[...]
    f = _rmsnorm(h, gf_ref[...]).astype(BF16)
    d_ff = wg_ref.shape[1]
    acc = None
    for c in range(d_ff // chunk):
        sl = slice(c * chunk, (c + 1) * chunk)
        hid = _silu(_dot(f, wg_ref[:, sl])) * _dot(f, wu_ref[:, sl])
        part = _dot(hid.astype(BF16), wd_ref[sl, :])
        acc = part if acc is None else acc + part
    o_ref[...] = _ple(h + acc, p_ref[...], gp_ref[...], wpg_ref, wpp_ref)


def _ffn0(h2d, p2d, gf, wg, wu, wd, gp, wpg, wpp):
    n, d = h2d.shape
    tm = min(TOKEN_TILE, n)
    d_ff = wg.shape[1]
    chunk = d_ff // 2
    assert chunk % LANES == 0
    return pl.pallas_call(
        functools.partial(_ffn0_kernel, chunk=chunk),
        grid=(n // tm,),
        in_specs=[
            pl.BlockSpec((tm, d), lambda i: (i, 0)),
            pl.BlockSpec((tm, p2d.shape[1]), lambda i: (i, 0)),
            _const_spec((1, d)),
            _const_spec(wg.shape), _const_spec(wu.shape), _const_spec(wd.shape),
            _const_spec((1, d)),
            _const_spec(wpg.shape), _const_spec(wpp.shape),
        ],
        out_specs=pl.BlockSpec((tm, d), lambda i: (i, 0)),
        out_shape=jax.ShapeDtypeStruct((n, d), F32),
        compiler_params=_cparams(("arbitrary",)),
        name="ffn0_ple",
    )(h2d, p2d, gf, wg, wu, wd, gp, wpg, wpp)


def _norm_keys_t(kvt, gain_col):
    parts = []
    for gi in range(KV_COLS // HEAD_DIM):
        x = kvt[gi * HEAD_DIM:(gi + 1) * HEAD_DIM, :]
        if gi % 2 == 0:
            ms = jnp.mean(x * x, axis=0, keepdims=True)
            x = x * lax.rsqrt(ms + EPS) * gain_col
        parts.append(x)
    return jnp.concatenate(parts, axis=0)


def _nsa_proj_kernel(h_ref, g_ref, wq_ref, wkc_ref, wkvt_ref, wgt_ref, gm_ref, qg_ref, kgs_ref,
                     kgw_ref, q_ref, kvc_ref, kvct_ref, kvst_ref, kvwt_ref, ksn_ref, kwn_ref,
                     gt_ref):
    a = _rmsnorm(h_ref[...], g_ref[...]).astype(BF16)
    q = _dot(a, wq_ref[...])
    ms = _group_mean_sq(q, gm_ref[...])
    q_ref[...] = ((q * lax.rsqrt(ms + EPS) * qg_ref[...]) * (HEAD_DIM ** -0.5)).astype(q_ref.dtype)
    kvc_ref[...] = _dot(a, wkc_ref[...])
    kvt = _dot_nt(wkvt_ref[...], a)
    kvst = kvt[KV_COLS:2 * KV_COLS]
    kvwt = kvt[2 * KV_COLS:]
    kvct_ref[...] = kvt[:KV_COLS]
    kvst_ref[...] = kvst
    kvwt_ref[...] = kvwt
    ksn_ref[...] = _norm_keys_t(kvst, kgs_ref[...]).astype(ksn_ref.dtype)
    kwn_ref[...] = _norm_keys_t(kvwt, kgw_ref[...]).astype(kwn_ref.dtype)
    gt_ref[...] = jax.nn.sigmoid(_dot(a, wgt_ref[...]))


def _nsa_proj(h2d, n_seq, gain, wq, wkc, wkvt, wgt, gmat, qg, kgs_col, kgw_col):
    n, d = h2d.shape
    s = n // n_seq
    tm = min(TOKEN_TILE, s)
    tps = s // tm
    row = lambda w: pl.BlockSpec((tm, w), lambda i: (i, 0))
    col = pl.BlockSpec((None, KV_COLS, tm), lambda i: (i // tps, 0, i % tps))
    tshape = lambda dt: jax.ShapeDtypeStruct((n_seq, KV_COLS, s), dt)
    n_gate_cols = wgt.shape[1]
    return pl.pallas_call(
        _nsa_proj_kernel,
        grid=(n // tm,),
        in_specs=[
            row(d), _const_spec((1, d)),
            _const_spec(wq.shape), _const_spec(wkc.shape), _const_spec(wkvt.shape),
            _const_spec(wgt.shape), _const_spec(gmat.shape), _const_spec(qg.shape),
            _const_spec(kgs_col.shape), _const_spec(kgw_col.shape),
        ],
        out_specs=[row(d), row(KV_COLS), col, col, col, col, col, row(n_gate_cols)],
        out_shape=[
            jax.ShapeDtypeStruct((n, d), BF16),
            jax.ShapeDtypeStruct((n, KV_COLS), F32),
            tshape(F32), tshape(F32), tshape(F32), tshape(BF16), tshape(BF16),
            jax.ShapeDtypeStruct((n, n_gate_cols), F32),
        ],
        compiler_params=_cparams(("arbitrary",)),
        name="nsa_proj",
    )(h2d, gain, wq, wkc, wkvt, wgt, gmat, qg, kgs_col, kgw_col)


SLABS = KV_COLS // LANES


def _compress_rows(read_slab, n_blocks, w_ref, gmat, kg_row):
    half = KV_COLS // 2
    per_half = SLABS // 2
    acc = [jnp.zeros((n_blocks, half), F32) for _ in range(2)]
    for l in range(CMP_BLOCK):
        for hf in range(2):
            xl = jnp.concatenate([read_slab(l, hf * per_half + c) for c in range(per_half)], axis=1)
            acc[hf] = acc[hf] + _dot(xl.astype(BF16), w_ref[l, hf])
    kv = jnp.concatenate(acc, axis=1)
    return _norm_keys(kv, kg_row, gmat).astype(BF16)


def _compress_kernel(x_ref, w_ref, gm_ref, kg_ref, o_ref):
    nb = o_ref.shape[0]
    read = lambda l, g: x_ref[pl.ds(l * SLABS + g, nb, stride=CMP_BLOCK * SLABS), :]
    o_ref[...] = _compress_rows(read, nb, w_ref, gm_ref[...], kg_ref[...])


def _compress_prompt(rows2d, w2, gmat, kg):
    n = rows2d.shape[0]
    tr = min(4096, n)
    nb = tr // CMP_BLOCK
    return pl.pallas_call(
        _compress_kernel,
        grid=(n // tr,),
        in_specs=[pl.BlockSpec((tr * SLABS, LANES), lambda i: (i, 0)),
                  _const_spec(w2.shape), _const_spec(gmat.shape), _const_spec(kg.shape)],
        out_specs=pl.BlockSpec((nb, KV_COLS), lambda i: (i, 0)),
        out_shape=jax.ShapeDtypeStruct((n // CMP_BLOCK, KV_COLS), BF16),
        compiler_params=_cparams(("arbitrary",)),
        name="compress_prompt",
    )(rows2d.reshape(n * SLABS, LANES), w2, gmat, kg)


def _compress_paged_kernel(pt_ref, pool_ref, w_ref, gm_ref, kg_ref, o_ref, buf_ref, st_ref, sem_ref,
                           *, pages_per_step, page):
    i = pl.program_id(0)
    n = pl.num_programs(0)

    def page_copy(step, p, slot):
        phys = pt_ref[step * pages_per_step + p]
        return pltpu.make_async_copy(pool_ref.at[phys], buf_ref.at[slot, p], sem_ref.at[slot])

    def start(step, slot):
        def issue(p, c):
            page_copy(step, p, slot).start()
            return c
        lax.fori_loop(0, pages_per_step, issue, 0)

    @pl.when(i == 0)
    def _():
        start(0, 0)

    @pl.when(i + 1 < n)
    def _():
        start(i + 1, (i + 1) % 2)

    slot = i % 2

    def wait(p, c):
        page_copy(i, p, slot).wait()
        return c
    lax.fori_loop(0, pages_per_step, wait, 0)

    def to_rows(p, c):
        for g in range(N_KV_HEADS):
            st_ref[g, pl.ds(pl.multiple_of(p * page, page), page), :] = buf_ref[slot, p, g].T
        return c
    lax.fori_loop(0, pages_per_step, to_rows, 0)

    nb = pages_per_step * page // CMP_BLOCK
    read = lambda l, g: st_ref[g, pl.ds(l, nb, stride=CMP_BLOCK), :]
    o_ref[...] = _compress_rows(read, nb, w_ref, gm_ref[...], kg_ref[...])


def _compress_paged(page_table_flat, pool_t, w2, gmat, kg, pages_per_step):
    n_pages = page_table_flat.shape[0]
    page = pool_t.shape[3]
    assert n_pages % pages_per_step == 0 and page == LANES and pool_t.shape[2] == GROUP_COLS
    steps = n_pages // pages_per_step
    nb = pages_per_step * page // CMP_BLOCK
    grid_spec = pltpu.PrefetchScalarGridSpec(
        num_scalar_prefetch=1,
        grid=(steps,),
        in_specs=[pl.BlockSpec(memory_space=pl.ANY),
                  pl.BlockSpec(w2.shape, lambda i, pt: (0, 0, 0, 0), pipeline_mode=pl.Buffered(1)),
                  pl.BlockSpec(gmat.shape, lambda i, pt: (0, 0)),
                  pl.BlockSpec(kg.shape, lambda i, pt: (0, 0))],
        out_specs=pl.BlockSpec((nb, KV_COLS), lambda i, pt: (i, 0)),
        scratch_shapes=[pltpu.VMEM((2, pages_per_step, N_KV_HEADS, GROUP_COLS, page), F32),
                        pltpu.VMEM((N_KV_HEADS, pages_per_step * page, GROUP_COLS), F32),
                        pltpu.SemaphoreType.DMA((2,))],
    )
    return pl.pallas_call(
        functools.partial(_compress_paged_kernel, pages_per_step=pages_per_step, page=page),
        grid_spec=grid_spec,
        out_shape=jax.ShapeDtypeStruct((steps * nb, KV_COLS), BF16),
        compiler_params=_cparams(("arbitrary",)),
        name="compress_paged",
    )(page_table_flat, pool_t, w2, gmat, kg)


def _flash_tile(q, kv_ref, kt, tk, carry, bias_of_head, mask):
    cols = pl.ds(pl.multiple_of(kt * tk, tk), tk)
    k = kv_ref[0:HEAD_DIM, cols]
    v = kv_ref[HEAD_DIM:GROUP_COLS, cols]
    new = []
    for r in range(HEADS_PER_GROUP):
        m, l, acc = carry[3 * r:3 * r + 3]
        s = _dot(q[:, r * HEAD_DIM:(r + 1) * HEAD_DIM], k)
        if bias_of_head is not None:
            s = s + bias_of_head(r)
        if mask is not None:
            s = jnp.where(mask, s, NEG)
        m_new = jnp.maximum(m, jnp.max(s, axis=-1, keepdims=True))
        alpha = jnp.exp(m - m_new)
        e = jnp.exp(s - m_new)
        new += [m_new, alpha * l + jnp.sum(e, axis=-1, keepdims=True),
                alpha * acc + _dot_nt(e.astype(BF16), v)]
    return tuple(new)


def _nsa_prompt_kernel(tbl_ref, thr_ref, q_ref, kc_ref, ks_ref, kw_ref, gt_ref, o_ref,
                       bias_ref, cbias_ref, *, n_top):
    g = pl.program_id(0)
    b = pl.program_id(1)
    qi = pl.program_id(2)
    tq = q_ref.shape[0]
    tk = tq
    hpg = HEADS_PER_GROUP
    nc = kc_ref.shape[0]
    ns = nc // CMP_PER_SEL
    blocks_per_tile = tk // SEL_BLOCK
    far = N_BUCKETS - 1

    qpos = qi * tq + lax.broadcasted_iota(jnp.int32, (tq, 1), 0)
    n_io = lax.broadcasted_iota(jnp.int32, (tq, nc), 1)
    relc = qpos - ((n_io + 1) * CMP_BLOCK - 1)

    @pl.when((b == 0) & (qi == 0))
    def _():
        row_i = lax.broadcasted_iota(jnp.int32, (tq, tk), 0)
        col_i = lax.broadcasted_iota(jnp.int32, (tq, tk), 1)
        for r in range(hpg):
            h = g * hpg + r
            for off in range(2):
                dist = row_i - col_i + off * tk
                bias_ref[r, off] = _bucket_bias(dist, thr_ref,
                                                lambda k: tbl_ref[k, h] - tbl_ref[far, h])

    @pl.when(b == 0)
    def _():
        for r in range(hpg):
            h = g * hpg + r
            cbias_ref[r, qi] = _bucket_bias(relc, thr_ref, lambda k: tbl_ref[k, h])

    q = q_ref[...]

    kc = kc_ref[...]
    kck = kc[:, :HEAD_DIM]
    kcv = kc[:, HEAD_DIM:]
    maskc = relc >= 0
    imp = None
    oc = []
    for r in range(hpg):
        s = _dot_nt(q[:, r * HEAD_DIM:(r + 1) * HEAD_DIM], kck) + cbias_ref[r, qi]
        sm = jnp.where(maskc, s, NEG)
        e = jnp.exp(sm - jnp.max(sm, axis=-1, keepdims=True))
        p = jnp.where(maskc, e / jnp.sum(e, axis=-1, keepdims=True), 0.0)
        imp = p if imp is None else imp + p
        oc.append(_dot(p.astype(BF16), kcv))

    pair_t = (lax.broadcasted_iota(jnp.int32, (ns, nc), 0)
              == lax.broadcasted_iota(jnp.int32, (ns, nc), 1) // CMP_PER_SEL).astype(BF16)
    hi, mid, lo = _split3(imp)
    imp_sel = (_dot_nt(pair_t, hi) + _dot_nt(pair_t, mid)) + _dot_nt(pair_t, lo)
    blk = lax.broadcasted_iota(jnp.int32, (ns, tq), 0)
    cur = (qi * tq + lax.broadcasted_iota(jnp.int32, (1, tq), 1)) // SEL_BLOCK
    forced = (blk == 0) | (blk == cur) | (blk == cur - 1)
    score = jnp.where(forced, FORCED_SCORE, jnp.where(blk <= cur, imp_sel, -1.0))
    rank = jnp.zeros((ns, tq), F32)
    for i in range(ns):
        si = score[i:i + 1, :]
        beats = (si > score) | ((si == score) & (blk > i))
        rank = rank + jnp.where(beats, 1.0, 0.0)
    sel_t = jnp.where(rank < n_top, 1.0, 0.0).astype(BF16)
    eye = (lax.broadcasted_iota(jnp.int32, (tq, tq), 0)
           == lax.broadcasted_iota(jnp.int32, (tq, tq), 1)).astype(BF16)
    sel = _dot_nt(eye, sel_t).astype(BF16)

    def init():
        out = []
        for _ in range(hpg):
            out += [jnp.full((tq, 1), NEG, F32), jnp.zeros((tq, 1), F32),
                    jnp.zeros((tq, HEAD_DIM), F32)]
        return tuple(out)

    def chosen(kt):
        expand = (lax.broadcasted_iota(jnp.int32, (ns, tk), 0)
                  == kt * blocks_per_tile + lax.broadcasted_iota(jnp.int32, (ns, tk), 1) // SEL_BLOCK)
        return _dot(sel, expand.astype(BF16)) > 0.5

    diag = lax.broadcasted_iota(jnp.int32, (tq, tk), 0) - lax.broadcasted_iota(jnp.int32, (tq, tk), 1)
    near = lambda off: (lambda r: bias_ref[r, off])

    def finish(carry):
        return [carry[3 * r + 2] / carry[3 * r + 1] for r in range(hpg)]

    carry = lax.fori_loop(
        0, jnp.maximum(qi - 1, 0),
        lambda kt, c: _flash_tile(q, ks_ref, kt, tk, c, None, chosen(kt)), init())
    carry = lax.cond(
        qi >= 1,
        lambda c: _flash_tile(q, ks_ref, qi - 1, tk, c, near(1), chosen(qi - 1)),
        lambda c: c, carry)
    carry = _flash_tile(q, ks_ref, qi, tk, carry, near(0), chosen(qi) & (diag >= 0))
    o_sel = finish(carry)

    n_back = WINDOW // tk
    carry = lax.cond(
        qi >= n_back,
        lambda c: _flash_tile(q, kw_ref, qi - n_back, tk, c, None, diag + n_back * tk < WINDOW),
        lambda c: c, init())
    carry = lax.cond(
        qi >= 1,
        lambda c: _flash_tile(q, kw_ref, qi - 1, tk, c, near(1), None),
        lambda c: c, carry)
    carry = _flash_tile(q, kw_ref, qi, tk, carry, near(0), diag >= 0)
    o_win = finish(carry)

    gates = gt_ref[...]
    heads = []
    for r in range(hpg):
        o = None
        for c, branch in enumerate((oc, o_sel, o_win)):
            term = gates[:, c * hpg + r:c * hpg + r + 1] * branch[r]
            o = term if o is None else o + term
        heads.append(o)
    o_ref[...] = jnp.concatenate(heads, axis=1).astype(o_ref.dtype)


def _nsa_prompt_attn(tbl, thr, q, kcn, ksn_t, kwn_t, gates, b, s):
    tq = min(ATTN_TILE, s)
    assert WINDOW == 2 * tq and tq + 1 >= MAX_DISTANCE
    nq = s // tq
    nc = s // CMP_BLOCK
    ns = s // SEL_BLOCK
    n_top = min(N_SELECTED, ns)
    kv_spec = pl.BlockSpec((None, GROUP_COLS, s), lambda g, i, j: (i, g, 0))
    return pl.pallas_call(
        functools.partial(_nsa_prompt_kernel, n_top=n_top),
        grid=(N_KV_HEADS, b, nq),
        in_specs=[
            _smem_spec(), _smem_spec(),
            pl.BlockSpec((tq, Q_GROUP_COLS), lambda g, i, j: (i * nq + j, g)),
            pl.BlockSpec((nc, GROUP_COLS), lambda g, i, j: (i, g)),
            kv_spec, kv_spec,
            pl.BlockSpec((tq, LANES), lambda g, i, j: (i * nq + j, g)),
        ],
        out_specs=pl.BlockSpec((tq, Q_GROUP_COLS), lambda g, i, j: (i * nq + j, g)),
        out_shape=jax.ShapeDtypeStruct((b * s, D_MODEL), BF16),
        scratch_shapes=[pltpu.VMEM((HEADS_PER_GROUP, 2, tq, tq), F32),
                        pltpu.VMEM((HEADS_PER_GROUP, nq, tq, nc), F32)],
        compiler_params=_cparams(("arbitrary", "arbitrary", "arbitrary")),
        name="nsa_prompt_attn",
    )(tbl, thr, q, kcn, ksn_t, kwn_t, gates)


def _row_bias(dist_row, thr_ref, tbl_rows):
    n = dist_row.shape[1]
    bucket = jnp.zeros(dist_row.shape, jnp.int32)
    for k in range(1, N_BUCKETS):
        bucket = bucket + jnp.where(dist_row >= thr_ref[k], 1, 0)
    onehot = (lax.broadcasted_iota(jnp.int32, (N_BUCKETS, n), 0) == bucket).astype(BF16)
    return _exact_dot(tbl_rows, onehot)


def _sample_cmp_kernel(thr_ref, q_ref, kc_ref, tbl_ref, oc_ref, idx_ref, *, past_len, seq, n_top):
    nc = kc_ref.shape[0]
    ns = past_len // SEL_BLOCK
    hpg = HEADS_PER_GROUP
    rows = hpg * seq
    kc = kc_ref[...]
    pair = (lax.broadcasted_iota(jnp.int32, (nc, ns), 0) // CMP_PER_SEL
            == lax.broadcasted_iota(jnp.int32, (nc, ns), 1)).astype(BF16)
    eye = (lax.broadcasted_iota(jnp.int32, (ns, ns), 0)
           == lax.broadcasted_iota(jnp.int32, (ns, ns), 1))
    eye_bf = eye.astype(BF16)
    ii = lax.broadcasted_iota(jnp.int32, (ns, ns), 0)
    jj = lax.broadcasted_iota(jnp.int32, (ns, ns), 1)
    t_row = lax.broadcasted_iota(jnp.int32, (rows, nc), 0) % seq
    n_io = lax.broadcasted_iota(jnp.int32, (rows, nc), 1)
    rel = past_len + t_row - ((n_io + 1) * CMP_BLOCK - 1)
    mask = rel >= 0
    blk = lax.broadcasted_iota(jnp.int32, (SUBLANES, ns), 1)
    forced = (blk == 0) | (blk == ns - 1)
    rank_lane = lax.broadcasted_iota(jnp.int32, (ns, LANES), 1).astype(F32)
    blk_col = lax.broadcasted_iota(jnp.int32, (ns, LANES), 0).astype(F32)
    for g in range(N_KV_HEADS):
        kck = kc[:, g * GROUP_COLS:g * GROUP_COLS + HEAD_DIM]
        kcv = kc[:, g * GROUP_COLS + HEAD_DIM:(g + 1) * GROUP_COLS]
        qg = q_ref[g].astype(BF16)
        tbl_g = tbl_ref[g]
        s = _dot_nt(qg, kck)
        s = s + _bucket_bias(rel, thr_ref, lambda k: tbl_g[:, k:k + 1])
        sm = jnp.where(mask, s, NEG)
        e = jnp.exp(sm - jnp.max(sm, axis=-1, keepdims=True))
        p = jnp.where(mask, e / jnp.sum(e, axis=-1, keepdims=True), 0.0)
        oc_ref[g] = _dot(p.astype(BF16), kcv)
        imp = p[0:seq]
        for r in range(1, hpg):
            imp = imp + p[r * seq:(r + 1) * seq]
        imp = jnp.concatenate([imp, jnp.zeros((SUBLANES - seq, nc), F32)], axis=0)
        score = jnp.where(forced, FORCED_SCORE, _exact_dot(imp, pair))
        hi, mid, lo = _split3(score)
        score_t = (_dot_nt(eye_bf, hi) + _dot_nt(eye_bf, mid)) + _dot_nt(eye_bf, lo)
        for t in range(seq):
            s_row = score[t:t + 1, :]
            s_col = score_t[:, t:t + 1]
            beats = (ii != jj) & ((s_row > s_col) | ((s_row == s_col) & (jj < ii)))
            rank_col = jnp.sum(jnp.where(beats, 1.0, 0.0), axis=1, keepdims=True)
            onehot = rank_col == rank_lane
            idx_row = jnp.sum(jnp.where(onehot, blk_col, 0.0), axis=0, keepdims=True)
            idx_ref[g * seq + t:g * seq + t + 1, :] = idx_row.astype(jnp.int32)


def _sample_cmp(thr, q_rt, kcn, tbl_rt, b, past_len, seq):
    nc = kcn.shape[0] // b
    ns = past_len // SEL_BLOCK
    n_top = min(N_SELECTED - 1, ns)
    rows = HEADS_PER_GROUP * seq
    return pl.pallas_call(
        functools.partial(_sample_cmp_kernel, past_len=past_len, seq=seq, n_top=n_top),
        grid=(b,),
        in_specs=[
            _smem_spec(),
            pl.BlockSpec((None, N_KV_HEADS, rows, HEAD_DIM), lambda i: (i, 0, 0, 0)),
            pl.BlockSpec((nc, KV_COLS), lambda i: (i, 0)),
            _const_spec(tbl_rt.shape),
        ],
        out_specs=[
            pl.BlockSpec((None, N_KV_HEADS, rows, HEAD_DIM), lambda i: (i, 0, 0, 0)),
            pl.BlockSpec((None, N_KV_HEADS * seq, LANES), lambda i: (i, 0, 0)),
        ],
        out_shape=[
            jax.ShapeDtypeStruct((b, N_KV_HEADS, rows, HEAD_DIM), F32),
            jax.ShapeDtypeStruct((b, N_KV_HEADS * seq, LANES), jnp.int32),
        ],
        compiler_params=_cparams(("arbitrary",)),
        name="sample_cmp",
    )(thr, q_rt, kcn, tbl_rt)


def _norm_k_cols(kt, gain_col):
    ms = jnp.mean(kt * kt, axis=0, keepdims=True)
    return (kt * lax.rsqrt(ms + EPS) * gain_col).astype(BF16)


def _sample_sel_win_kernel(idx_ref, pt_ref, thr_ref, pool_ref, q_ref, ksnew_ref, kwnew_ref,
                           win_ref, tbl_ref, kgs_ref, kgw_ref, os_ref, ow_ref,
                           buf_ref, sem_ref, *, past_len, seq, n_top, page):
    i = pl.program_id(0)
    n_pages = past_len // page
    blocks_per_page = page // SEL_BLOCK
    n_keys = n_top * page
    stride = n_top + 1

    def block_of(g, t, j):
        return idx_ref[(i * N_KV_HEADS * seq + g * seq + t) * stride + j]

    def tile_copy(g, t, j):
        phys = pt_ref[i * n_pages + block_of(g, t, j) // blocks_per_page]
        return pltpu.make_async_copy(
            pool_ref.at[phys, g],
            buf_ref.at[g * seq + t, :, pl.ds(pl.multiple_of(j * page, page), page)],
            sem_ref.at[0])

    def for_all_tiles(fn):
        for g in range(N_KV_HEADS):
            def per_tj(tj, c):
                fn(tile_copy(g, tj // n_top, tj % n_top))
                return c
            lax.fori_loop(0, seq * n_top, per_tj, 0)

    for_all_tiles(lambda cp: cp.start())

    wb = win_ref.shape[2]
    wk = [_norm_k_cols(win_ref[g, 0:HEAD_DIM, :], kgw_ref[...]) for g in range(N_KV_HEADS)]
    wv = [win_ref[g, HEAD_DIM:GROUP_COLS, :].astype(BF16) for g in range(N_KV_HEADS)]
    j_io = lax.broadcasted_iota(jnp.int32, (1, wb), 1)
    tnew = lax.broadcasted_iota(jnp.int32, (1, SUBLANES), 1)

    def attend(q, tbl_g, k_past, v_past, dist, mask, knew, distn, maskn):
        s1 = jnp.where(mask, _dot(q, k_past) + _row_bias(dist, thr_ref, tbl_g), NEG)
        s2 = jnp.where(maskn, _dot_nt(q, knew[:, :HEAD_DIM].astype(BF16))
                       + _row_bias(distn, thr_ref, tbl_g), NEG)
        m = jnp.maximum(jnp.max(s1, axis=-1, keepdims=True), jnp.max(s2, axis=-1, keepdims=True))
        e1 = jnp.where(mask, jnp.exp(s1 - m), 0.0)
        e2 = jnp.where(maskn, jnp.exp(s2 - m), 0.0)
        l = jnp.sum(e1, axis=-1, keepdims=True) + jnp.sum(e2, axis=-1, keepdims=True)
        o = _dot_nt(e1.astype(BF16), v_past) + _dot(e2.astype(BF16),
                                                    knew[:, HEAD_DIM:].astype(BF16))
        return o / l

    def window(t, c):
        dist = wb + t - j_io
        maskw = (dist >= 0) & (dist < WINDOW)
        distn = t - tnew
        maskn = (distn >= 0) & (distn < WINDOW)
        for g in range(N_KV_HEADS):
            ow_ref[g, t] = attend(q_ref[g, t].astype(BF16), tbl_ref[g], wk[g], wv[g], dist, maskw,
                                  kwnew_ref[g], distn, maskn)
        return c
    lax.fori_loop(0, seq, window, 0)

    for_all_tiles(lambda cp: cp.wait())

    lane = lax.broadcasted_iota(jnp.int32, (1, n_keys), 1)
    tile_of_lane = lane // page
    row_in_page = lane % page

    def selected(t, c):
        distn = t - tnew
        maskn = distn >= 0
        for g in range(N_KV_HEADS):
            kvt = buf_ref[g * seq + t]
            page_base = jnp.zeros((1, n_keys), jnp.int32)
            half = jnp.zeros((1, n_keys), jnp.int32)
            for j in range(n_top):
                blk = block_of(g, t, j)
                page_base = jnp.where(tile_of_lane == j, (blk // blocks_per_page) * page, page_base)
                half = jnp.where(tile_of_lane == j, blk % blocks_per_page, half)
            mask = (row_in_page // SEL_BLOCK) == half
            dist = past_len + t - (page_base + row_in_page)
            os_ref[g, t] = attend(q_ref[g, t].astype(BF16), tbl_ref[g],
                                  _norm_k_cols(kvt[0:HEAD_DIM], kgs_ref[...]),
                                  kvt[HEAD_DIM:GROUP_COLS].astype(BF16), dist, mask,
                                  ksnew_ref[g], distn, maskn)
        return c
    lax.fori_loop(0, seq, selected, 0)


def _sample_sel_win(idx_flat, pt_flat, thr, pool_t, q_tr, ksnew, kwnew, win_t, tbl_r8,
                    kgs_col, kgw_col, b, past_len, seq):
    ns = past_len // SEL_BLOCK
    n_top = min(N_SELECTED - 1, ns)
    page = pool_t.shape[3]
    wb = win_t.shape[3]
    o_shape = jax.ShapeDtypeStruct((b, N_KV_HEADS, seq, SUBLANES, HEAD_DIM), F32)
    o_spec = pl.BlockSpec((None, N_KV_HEADS, seq, SUBLANES, HEAD_DIM),
                          lambda i, *_: (i, 0, 0, 0, 0))
    new_spec = pl.BlockSpec((None, N_KV_HEADS, SUBLANES, GROUP_COLS), lambda i, *_: (i, 0, 0, 0))
    grid_spec = pltpu.PrefetchScalarGridSpec(
        num_scalar_prefetch=3,
        grid=(b,),
        in_specs=[
            pl.BlockSpec(memory_space=pl.ANY),
            pl.BlockSpec((None, N_KV_HEADS, seq, SUBLANES, HEAD_DIM), lambda i, *_: (i, 0, 0, 0, 0)),
            new_spec, new_spec,
            pl.BlockSpec((None, N_KV_HEADS, GROUP_COLS, wb), lambda i, *_: (i, 0, 0, 0)),
            pl.BlockSpec(tbl_r8.shape, lambda i, *_: (0, 0, 0)),
            pl.BlockSpec(kgs_col.shape, lambda i, *_: (0, 0)),
            pl.BlockSpec(kgw_col.shape, lambda i, *_: (0, 0)),
        ],
        out_specs=[o_spec, o_spec],
        scratch_shapes=[pltpu.VMEM((N_KV_HEADS * seq, GROUP_COLS, n_top * page), F32),
                        pltpu.SemaphoreType.DMA((1,))],
    )
    return pl.pallas_call(
        functools.partial(_sample_sel_win_kernel, past_len=past_len, seq=seq, n_top=n_top,
                          page=page),
        grid_spec=grid_spec,
        out_shape=[o_shape, o_shape],
        compiler_params=_cparams(("arbitrary",)),
        name="sample_sel_win",
    )(idx_flat, pt_flat, thr, pool_t, q_tr, ksnew, kwnew, win_t, tbl_r8, kgs_col, kgw_col)


def _route(f, rhi_ref, rlo_ref):
    f_hi = f.astype(BF16)
    f_lo = (f - f_hi.astype(F32)).astype(BF16)
    logits = (_dot(f_hi, rhi_ref[...]) + _dot(f_lo, rhi_ref[...])) + _dot(f_hi, rlo_ref[...])
    lane = lax.broadcasted_iota(jnp.int32, logits.shape, 1).astype(F32)
    logits = jnp.where(lane < N_EXPERTS, logits, -jnp.inf)
    m1 = jnp.max(logits, axis=-1, keepdims=True)
    i1 = jnp.min(jnp.where(logits == m1, lane, float(LANES)), axis=-1, keepdims=True)
    rest = jnp.where(lane == i1, -jnp.inf, logits)
    m2 = jnp.max(rest, axis=-1, keepdims=True)
    i2 = jnp.min(jnp.where(rest == m2, lane, float(LANES)), axis=-1, keepdims=True)
    e2 = jnp.exp(m2 - m1)
    denom = 1.0 + e2
    return jnp.where(lane == i1, 1.0 / denom, 0.0) + jnp.where(lane == i2, e2 / denom, 0.0)


def _attn_out_kernel(h_ref, o_ref, wout_ref, gf_ref, rhi_ref, rlo_ref, h2_ref, f_ref, gate_ref):
    h2 = h_ref[...] + _dot(o_ref[...], wout_ref[...])
    h2_ref[...] = h2
    f = _rmsnorm(h2, gf_ref[...])
    f_ref[...] = f.astype(f_ref.dtype)
    gate_ref[...] = _route(f, rhi_ref, rlo_ref)


def _attn_out_merge_kernel(h_ref, oc_ref, os_ref, ow_ref, gt_ref, wout_ref, gf_ref, rhi_ref,
                           rlo_ref, h2_ref, f_ref, gate_ref):
    gates = gt_ref[...]
    n_gate = gates.shape[1]
    hpg = HEADS_PER_GROUP
    idx = lax.broadcasted_iota(jnp.int32, (n_gate, D_MODEL), 0)
    head = lax.broadcasted_iota(jnp.int32, (n_gate, D_MODEL), 1) // HEAD_DIM
    o = None
    for c, ref in enumerate((oc_ref, os_ref, ow_ref)):
        lane_of_gate = (head // hpg) * LANES + c * hpg + head % hpg
        gate = _exact_dot(gates, (idx == lane_of_gate).astype(BF16))
        term = gate * ref[...]
        o = term if o is None else o + term
    h2 = h_ref[...] + _dot(o.astype(BF16), wout_ref[...])
    h2_ref[...] = h2
    f = _rmsnorm(h2, gf_ref[...])
    f_ref[...] = f.astype(f_ref.dtype)
    gate_ref[...] = _route(f, rhi_ref, rlo_ref)


def _attn_out(h2d, o_list, gates, wout, gf, rhi, rlo):
    n, d = h2d.shape
    tm = min(TOKEN_TILE, n)
    row = lambda w: pl.BlockSpec((tm, w), lambda i: (i, 0))
    merged = len(o_list) == 1
    body = _attn_out_kernel if merged else _attn_out_merge_kernel
    in_specs = [row(d)] + [row(d)] * len(o_list) + ([] if merged else [row(gates.shape[1])])
    in_specs += [_const_spec(wout.shape), _const_spec((1, d)), _const_spec(rhi.shape),
                 _const_spec(rlo.shape)]
    args = [h2d] + list(o_list) + ([] if merged else [gates]) + [wout, gf, rhi, rlo]
    return pl.pallas_call(
        body,
        grid=(n // tm,),
        in_specs=in_specs,
        out_specs=[row(d), row(d), row(LANES)],
        out_shape=[jax.ShapeDtypeStruct((n, d), F32), jax.ShapeDtypeStruct((n, d), BF16),
                   jax.ShapeDtypeStruct((n, LANES), F32)],
        compiler_params=_cparams(("arbitrary",)),
        name="attn_out_route",
    )(*args)


def _moe_kernel(f_ref, h2_ref, gate_ref, p_ref, wg_ref, wu_ref, wd_ref, gp_ref, wpg_ref, wpp_ref,
                o_ref, acc_ref):
    e = pl.program_id(1)

    @pl.when(e == 0)
    def _():
        acc_ref[...] = jnp.zeros_like(acc_ref)

    f = f_ref[...]
    gate = gate_ref[...]
    lane = lax.broadcasted_iota(jnp.int32, gate.shape, 1)
    ge = jnp.sum(jnp.where(lane == e, gate, 0.0), axis=-1, keepdims=True)
    hid = _silu(_dot(f, wg_ref[...])) * _dot(f, wu_ref[...])
    acc_ref[...] += ge * _dot(hid.astype(BF16), wd_ref[...])

    @pl.when(e == pl.num_programs(1) - 1)
    def _():
        o_ref[...] = _ple(h2_ref[...] + acc_ref[...], p_ref[...], gp_ref[...], wpg_ref, wpp_ref)


def _moe(f2d, h2, gate, p2d, wg, wu, wd, gp, wpg, wpp):
    n, d = h2.shape
    tm = min(TOKEN_TILE, n)
    n_e, _, dff = wg.shape
    row = lambda w: pl.BlockSpec((tm, w), lambda i, e: (i, 0))
    return pl.pallas_call(
        _moe_kernel,
        grid=(n // tm, n_e),
        in_specs=[
            row(d), row(d), row(LANES), row(p2d.shape[1]),
            pl.BlockSpec((None, d, dff), lambda i, e: (e, 0, 0)),
            pl.BlockSpec((None, d, dff), lambda i, e: (e, 0, 0)),
            pl.BlockSpec((None, dff, d), lambda i, e: (e, 0, 0)),
            _const_spec((1, d)), _const_spec(wpg.shape), _const_spec(wpp.shape),
        ],
        out_specs=row(d),
        out_shape=jax.ShapeDtypeStruct((n, d), F32),
        scratch_shapes=[pltpu.VMEM((tm, d), F32)],
        compiler_params=_cparams(("arbitrary", "arbitrary")),
        name="moe_ple",
    )(f2d, h2, gate, p2d, wg, wu, wd, gp, wpg, wpp)


def _bucket_thresholds():
    n = jnp.arange(MAX_DISTANCE + 1)
    max_exact = N_BUCKETS // 2
    nf = jnp.maximum(n, 1).astype(F32)
    large = max_exact + (jnp.log(nf / max_exact) / math.log(MAX_DISTANCE / max_exact)
                         * (N_BUCKETS - max_exact)).astype(jnp.int32)
    bucket = jnp.where(n < max_exact, n, jnp.minimum(large, N_BUCKETS - 1))
    return jnp.sum(bucket[None, :] < jnp.arange(N_BUCKETS)[:, None], axis=1).astype(jnp.int32)


def _row(v):
    return v.reshape(1, -1).astype(F32)


def _key_gain_row(k_gain, cols):
    g = jnp.concatenate([k_gain, jnp.ones_like(k_gain)])
    return jnp.tile(g, cols // GROUP_COLS).reshape(1, cols).astype(F32)


def _rows_on_lanes(x):
    lead = x.shape[:-4]
    rows = x.shape[-4]
    nl = len(lead)
    perm = tuple(range(nl)) + (nl + 1, nl + 2, nl + 3, nl)
    return jnp.transpose(x, perm).reshape(*lead, N_KV_HEADS, GROUP_COLS, rows)


def _rows_on_lanes_inverse(xt):
    b, _, rows = xt.shape
    x = xt.reshape(b, N_KV_HEADS, 2, HEAD_DIM, rows)
    return jnp.transpose(x, (0, 4, 1, 2, 3))[None]


def kernel(x_prompt, x_sample, state_conv, cache_cmp, cache_sel, state_win, page_table,
           p_prompt, p_sample, norm_mix, norm_ffn, norm_ple, conv_w_in, conv_w, conv_w_out,
           nsa_w_in, nsa_w_cmp, nsa_q_norm, nsa_k_norm, nsa_w_out, rel_bias,
           ffn_w_gate, ffn_w_up, ffn_w_down, moe_router, moe_w_gate, moe_w_up, moe_w_down,
           ple_w_proj, ple_w_gate):
    bp, sp, d = x_prompt.shape
    bs, ss, _ = x_sample.shape
    n_p = bp * sp
    n_s = bs * ss
    page = cache_cmp.shape[2]
    past_len = page_table.shape[1] * page
    hpg = HEADS_PER_GROUP
    bf = lambda w: w.astype(BF16)

    cw = jnp.zeros((SUBLANES, d), F32).at[:CONV_WIDTH].set(conv_w[0])
    w_in0, w_out0 = bf(conv_w_in[0]), bf(conv_w_out[0])
    g_mix0 = _row(norm_mix[0])
    h_p, tail_p = _mix0_prompt(x_prompt, g_mix0, w_in0, cw, w_out0)
    conv_prompt = tail_p[:, SUBLANES - (CONV_WIDTH - 1):][None]

    st = state_conv[0]
    zeros = jnp.zeros((bs, ss, d), F32)
    s1 = zeros.at[:, 0].set(st[:, 1]).reshape(n_s, d)
    s2 = zeros.at[:, 0].set(st[:, 0]).at[:, 1].set(st[:, 1]).reshape(n_s, d)
    h_s, u_s = _mix0_sample(x_sample.reshape(n_s, d), g_mix0, w_in0, cw, w_out0, s1, s2, ss)
    conv_sample = u_s.reshape(bs, ss, d)[:, ss - (CONV_WIDTH - 1):][None]

    ffn0_w = (_row(norm_ffn[0]), bf(ffn_w_gate[0]), bf(ffn_w_up[0]), bf(ffn_w_down[0]),
              _row(norm_ple[0]), bf(ple_w_gate[0]), bf(ple_w_proj[0]))
    h_p = _ffn0(h_p.reshape(n_p, d), p_prompt[0].reshape(n_p, -1), *ffn0_w)
    h_s = _ffn0(h_s, p_sample[0].reshape(n_s, -1), *ffn0_w)

    q_cols = N_HEADS * HEAD_DIM
    kv_cols = N_BRANCHES * KV_COLS
    w_in1 = nsa_w_in[0]
    wq = bf(w_in1[:, :q_cols])
    wkv = w_in1[:, q_cols:q_cols + kv_cols]
    wkc = bf(wkv[:, :KV_COLS])
    wkvt = bf(wkv.T)
    wg_src = w_in1[:, q_cols + kv_cols:].reshape(d, N_BRANCHES, N_KV_HEADS, hpg)
    wgt = jnp.zeros((d, N_KV_HEADS, LANES), F32).at[:, :, :N_BRANCHES * hpg].set(
        wg_src.transpose(0, 2, 1, 3).reshape(d, N_KV_HEADS, N_BRANCHES * hpg))
    wgt = bf(wgt.reshape(d, N_KV_HEADS * LANES))
    gi = jnp.arange(MXU_DIM) // HEAD_DIM
    gmat = (gi[:, None] == gi[None, :]).astype(BF16)
    qg = jnp.tile(nsa_q_norm[0], N_HEADS).reshape(1, q_cols).astype(F32)
    kgc = _key_gain_row(nsa_k_norm[0, 0], KV_COLS)
    kgs_col = nsa_k_norm[0, 1].reshape(HEAD_DIM, 1).astype(F32)
    kgw_col = nsa_k_norm[0, 2].reshape(HEAD_DIM, 1).astype(F32)
    proj_w = (_row(norm_mix[1]), wq, wkc, wkvt, wgt, gmat, qg, kgs_col, kgw_col)

    per_tile = MXU_DIM // HEAD_DIM
    wc = nsa_w_cmp[0].reshape(CMP_BLOCK, 2, per_tile, HEAD_DIM, HEAD_DIM)
    w2 = bf(jnp.einsum('lhade,ab->lhadbe', wc, jnp.eye(per_tile, dtype=F32)).reshape(
        CMP_BLOCK, 2, MXU_DIM, MXU_DIM))

    thr = _bucket_thresholds()
    tbl = rel_bias.astype(F32)

    q_p, kvc_p, kvct_p, kvst_p, kvwt_p, ksn_p, kwn_p, gt_p = _nsa_proj(h_p, bp, *proj_w)
    kcn_p = _compress_prompt(kvc_p, w2, gmat, kgc)
    o_p = _nsa_prompt_attn(tbl, thr, q_p, kcn_p, ksn_p, kwn_p, gt_p, bp, sp)

    q_s, _, kvct_s, kvst_s, kvwt_s, ksn_s, kwn_s, gt_s = _nsa_proj(h_s, 1, *proj_w)
    pt_flat = page_table.reshape(-1).astype(jnp.int32)
    kcn_s = _compress_paged(pt_flat, _rows_on_lanes(cache_cmp[0]), w2, gmat, kgc, PAGES_PER_STEP)
    q5 = q_s.astype(F32).reshape(bs, ss, N_KV_HEADS, hpg, HEAD_DIM)
    q_rt = q5.transpose(0, 2, 3, 1, 4).reshape(bs, N_KV_HEADS, hpg * ss, HEAD_DIM)
    q_tr = jnp.zeros((bs, N_KV_HEADS, ss, SUBLANES, HEAD_DIM), F32).at[:, :, :, :hpg].set(
        q5.transpose(0, 2, 1, 3, 4))
    tbl_gr = tbl.T.reshape(N_KV_HEADS, hpg, N_BUCKETS)
    tbl_rt = jnp.repeat(tbl_gr, ss, axis=1)
    tbl_r8 = jnp.zeros((N_KV_HEADS, SUBLANES, N_BUCKETS), F32).at[:, :hpg].set(tbl_gr)
    oc_s, idx = _sample_cmp(thr, q_rt, kcn_s, tbl_rt, bs, past_len, ss)
    n_top_s = min(N_SELECTED - 1, past_len // SEL_BLOCK)
    idx_flat = idx[:, :, :n_top_s + 1].reshape(-1)

    def new_rows(kn_t):
        r = kn_t[0].astype(F32).reshape(N_KV_HEADS, GROUP_COLS, bs, ss).transpose(2, 0, 3, 1)
        return jnp.zeros((bs, N_KV_HEADS, SUBLANES, GROUP_COLS), F32).at[:, :, :ss].set(r)

    win_t = _rows_on_lanes(state_win[0])
    os_s, ow_s = _sample_sel_win(idx_flat, pt_flat, thr, _rows_on_lanes(cache_sel[0]), q_tr,
                                 new_rows(ksn_s), new_rows(kwn_s), win_t, tbl_r8,
                                 kgs_col, kgw_col, bs, past_len, ss)
    oc_s2 = oc_s.reshape(bs, N_KV_HEADS, hpg, ss, HEAD_DIM).transpose(0, 3, 1, 2, 4).reshape(n_s, d)
    to2d = lambda o: o[:, :, :, :hpg].transpose(0, 2, 1, 3, 4).reshape(n_s, d)

    w_out1 = bf(nsa_w_out[0])
    router = jnp.zeros((d, LANES), F32).at[:, :N_EXPERTS].set(moe_router[0])
    r_hi = bf(router)
    r_lo = bf(router - r_hi.astype(F32))
    gf1 = _row(norm_ffn[1])
    moe_w = (bf(moe_w_gate[0]), bf(moe_w_up[0]), bf(moe_w_down[0]), _row(norm_ple[1]),
             bf(ple_w_gate[1]), bf(ple_w_proj[1]))
    h2_p, f_p, gate_p = _attn_out(h_p, [o_p], None, w_out1, gf1, r_hi, r_lo)
    y_p = _moe(f_p, h2_p, gate_p, p_prompt[1].reshape(n_p, -1), *moe_w)
    h2_s, f_s, gate_s = _attn_out(h_s, [oc_s2, to2d(os_s), to2d(ow_s)], gt_s, w_out1, gf1,
                                  r_hi, r_lo)
    y_s = _moe(f_s, h2_s, gate_s, p_sample[1].reshape(n_s, -1), *moe_w)

    sample_rows = lambda xt: _rows_on_lanes_inverse(
        xt[0].reshape(KV_COLS, bs, ss).transpose(1, 0, 2))
    wbp = min(WINDOW, sp)
    win_s_t = jnp.concatenate(
        [win_t, kvwt_s[0].reshape(N_KV_HEADS, GROUP_COLS, bs, ss).transpose(2, 0, 1, 3)],
        axis=3)[..., ss:]
    return (y_p.reshape(bp, sp, d), y_s.reshape(bs, ss, d), conv_prompt, conv_sample,
            _rows_on_lanes_inverse(kvct_p), sample_rows(kvct_s),
            _rows_on_lanes_inverse(kvst_p), sample_rows(kvst_s),
            _rows_on_lanes_inverse(kvwt_p[:, :, sp - wbp:]),
            _rows_on_lanes_inverse(win_s_t.reshape(bs, KV_COLS, -1)))
```

```python
import functools
import math

import jax
import jax.numpy as jnp
from jax import lax
from jax.experimental import pallas as pl
from jax.experimental.pallas import tpu as pltpu

F32 = jnp.float32
BF16 = jnp.bfloat16

D_MODEL = 1024
N_HEADS = 16
HEAD_DIM = 64
N_KV_HEADS = 4
HEADS_PER_GROUP = 4
N_BRANCHES = 3
CMP_BLOCK = 32
SEL_BLOCK = 64
CMP_PER_SEL = SEL_BLOCK // CMP_BLOCK
N_SELECTED = 16
WINDOW = 512
N_BUCKETS = 32
MAX_DISTANCE = 128
N_EXPERTS = 8
CONV_WIDTH = 3
EPS = 1e-6
FORCED_SCORE = 1e4
NEG = -1e30
KV_COLS = N_KV_HEADS * 2 * HEAD_DIM
GROUP_COLS = 2 * HEAD_DIM
Q_GROUP_COLS = HEADS_PER_GROUP * HEAD_DIM

LANES = 128
SUBLANES = 8
MXU_DIM = 256
VMEM_LIMIT = 56 * 1024 * 1024

TOKEN_TILE = 512
ATTN_TILE = 256
PAGES_PER_STEP = 32


def _cparams(sem, vmem=VMEM_LIMIT):
    return pltpu.CompilerParams(dimension_semantics=sem, vmem_limit_bytes=vmem)


def _const_spec(shape):
    nd = len(shape)
    return pl.BlockSpec(shape, lambda *_: (0,) * nd, pipeline_mode=pl.Buffered(1))


def _smem_spec():
    return pl.BlockSpec(memory_space=pltpu.SMEM)


def _dot(a, b):
    return jnp.dot(a, b, preferred_element_type=F32)


def _dot_nt(a, b):
    return lax.dot_general(a, b, (((1,), (1,)), ((), ())), preferred_element_type=F32)


def _split3(x):
    hi = x.astype(BF16)
    r1 = x - hi.astype(F32)
    mid = r1.astype(BF16)
    lo = (r1 - mid.astype(F32)).astype(BF16)
    return hi, mid, lo


def _exact_dot(x, m01):
    hi, mid, lo = _split3(x)
    return (_dot(hi, m01) + _dot(mid, m01)) + _dot(lo, m01)


def _rmsnorm(x, g):
    ms = jnp.mean(x * x, axis=-1, keepdims=True)
    return x * lax.rsqrt(ms + EPS) * g


def _group_mean_sq(x, gmat):
    n = x.shape[1]
    w = gmat.shape[0]
    outs = []
    for c in range(n // w):
        blk = x[:, c * w:(c + 1) * w]
        sq = blk * blk
        hi = sq.astype(BF16)
        lo = (sq - hi.astype(F32)).astype(BF16)
        outs.append(_dot(hi, gmat) + _dot(lo, gmat))
    out = outs[0] if len(outs) == 1 else jnp.concatenate(outs, axis=1)
    return out * (1.0 / HEAD_DIM)


def _norm_keys(kv, gain_row, gmat):
    ms = _group_mean_sq(kv, gmat)
    lane = lax.broadcasted_iota(jnp.int32, kv.shape, 1)
    is_k = (lane % GROUP_COLS) < HEAD_DIM
    return jnp.where(is_k, kv * lax.rsqrt(ms + EPS) * gain_row, kv)


def _bucket_bias(dist, thr_ref, value_of_bucket):
    val = value_of_bucket(0)
    val = jnp.broadcast_to(val, dist.shape).astype(F32)
    for k in range(1, N_BUCKETS):
        val = jnp.where(dist >= thr_ref[k], value_of_bucket(k), val)
    return val


def _silu(x):
    return x * jax.nn.sigmoid(x)


def _ple(h, p, gain, wpg_ref, wpp_ref):
    r = _rmsnorm(h, gain).astype(BF16)
    g = jax.nn.sigmoid(_dot(r, wpg_ref[...]))
    return h + g * _dot(p.astype(BF16), wpp_ref[...])


def _conv_mix_tail(x, gb, u, um1, um2, cw_ref, wout_ref):
    conv = cw_ref[0:1, :] * um2 + cw_ref[1:2, :] * um1 + cw_ref[2:3, :] * u
    y = _dot((gb * conv).astype(BF16), wout_ref[...])
    return x + y


def _mix0_prompt_kernel(x_ref, g_ref, win_ref, cw_ref, wout_ref, h_ref, st_ref, carry_ref):
    j = pl.program_id(1)
    tm = x_ref.shape[0]

    @pl.when(j == 0)
    def _():
        carry_ref[...] = jnp.zeros_like(carry_ref)

    x = x_ref[...]
    a = _rmsnorm(x, g_ref[...]).astype(BF16)
    proj = _dot(a, win_ref[...])
    gb = proj[:, :D_MODEL]
    u = proj[:, D_MODEL:2 * D_MODEL] * proj[:, 2 * D_MODEL:]
    c0 = carry_ref[SUBLANES - 2:SUBLANES - 1, :]
    c1 = carry_ref[SUBLANES - 1:SUBLANES, :]
    row = lax.broadcasted_iota(jnp.int32, u.shape, 0)
    um1 = jnp.where(row == 0, c1, pltpu.roll(u, 1, 0))
    um2 = jnp.where(row == 0, c0, jnp.where(row == 1, c1, pltpu.roll(u, 2, 0)))
    h_ref[...] = _conv_mix_tail(x, gb, u, um1, um2, cw_ref, wout_ref)
    tail = u[tm - SUBLANES:, :]
    carry_ref[...] = tail
    st_ref[...] = tail


def _mix0_sample_kernel(x_ref, g_ref, win_ref, cw_ref, wout_ref, s1_ref, s2_ref, h_ref, u_ref,
                        *, seq):
    x = x_ref[...]
    a = _rmsnorm(x, g_ref[...]).astype(BF16)
    proj = _dot(a, win_ref[...])
    gb = proj[:, :D_MODEL]
    u = proj[:, D_MODEL:2 * D_MODEL] * proj[:, 2 * D_MODEL:]
    t = lax.broadcasted_iota(jnp.int32, u.shape, 0) % seq
    um1 = jnp.where(t >= 1, pltpu.roll(u, 1, 0), s1_ref[...])
    um2 = jnp.where(t >= 2, pltpu.roll(u, 2, 0), s2_ref[...])
    h_ref[...] = _conv_mix_tail(x, gb, u, um1, um2, cw_ref, wout_ref)
    u_ref[...] = u


def _mix0_prompt(x, gain, w_in, cw, w_out):
    b, s, d = x.shape
    tm = min(TOKEN_TILE, s)
    grid = (b, s // tm)
    return pl.pallas_call(
        _mix0_prompt_kernel,
        grid=grid,
        in_specs=[
            pl.BlockSpec((None, tm, d), lambda i, j: (i, j, 0)),
            _const_spec((1, d)),
            _const_spec(w_in.shape),
            _const_spec(cw.shape),
            _const_spec(w_out.shape),
        ],
        out_specs=[
            pl.BlockSpec((None, tm, d), lambda i, j: (i, j, 0)),
            pl.BlockSpec((None, SUBLANES, d), lambda i, j: (i, 0, 0)),
        ],
        out_shape=[
            jax.ShapeDtypeStruct((b, s, d), F32),
            jax.ShapeDtypeStruct((b, SUBLANES, d), F32),
        ],
        scratch_shapes=[pltpu.VMEM((SUBLANES, d), F32)],
        compiler_params=_cparams(("arbitrary", "arbitrary")),
        name="mix0_prompt",
    )(x, gain, w_in, cw, w_out)


def _mix0_sample(x2d, gain, w_in, cw, w_out, s1, s2, seq):
    n, d = x2d.shape
    return pl.pallas_call(
        functools.partial(_mix0_sample_kernel, seq=seq),
        out_shape=[jax.ShapeDtypeStruct((n, d), F32), jax.ShapeDtypeStruct((n, d), F32)],
        compiler_params=_cparams(None),
        name="mix0_sample",
    )(x2d, gain, w_in, cw, w_out, s1, s2)


def _ffn0_kernel(h_ref, p_ref, gf_ref, wg_ref, wu_ref, wd_ref, gp_ref, wpg_ref, wpp_ref, o_ref,
                 *, chunk):
    h = h_ref[...]
    f = _rmsnorm(h, gf_ref[...]).astype(BF16)
    d_ff = wg_ref.shape[1]
    acc = None
    for c in range(d_ff // chunk):
        sl = slice(c * chunk, (c + 1) * chunk)
        hid = _silu(_dot(f, wg_ref[:, sl])) * _dot(f, wu_ref[:, sl])
        part = _dot(hid.astype(BF16), wd_ref[sl, :])
        acc = part if acc is None else acc + part
    o_ref[...] = _ple(h + acc, p_ref[...], gp_ref[...], wpg_ref, wpp_ref)


def _ffn0(h2d, p2d, gf, wg, wu, wd, gp, wpg, wpp):
    n, d = h2d.shape
    tm = min(TOKEN_TILE, n)
    d_ff = wg.shape[1]
    chunk = d_ff // 2
    assert chunk % LANES == 0
    return pl.pallas_call(
        functools.partial(_ffn0_kernel, chunk=chunk),
        grid=(n // tm,),
        in_specs=[
            pl.BlockSpec((tm, d), lambda i: (i, 0)),
            pl.BlockSpec((tm, p2d.shape[1]), lambda i: (i, 0)),
            _const_spec((1, d)),
            _const_spec(wg.shape), _const_spec(wu.shape), _const_spec(wd.shape),
            _const_spec((1, d)),
            _const_spec(wpg.shape), _const_spec(wpp.shape),
        ],
        out_specs=pl.BlockSpec((tm, d), lambda i: (i, 0)),
        out_shape=jax.ShapeDtypeStruct((n, d), F32),
        compiler_params=_cparams(("arbitrary",)),
        name="ffn0_ple",
    )(h2d, p2d, gf, wg, wu, wd, gp, wpg, wpp)


def _norm_keys_t(kvt, gain_col):
    parts = []
    for gi in range(KV_COLS // HEAD_DIM):
        x = kvt[gi * HEAD_DIM:(gi + 1) * HEAD_DIM, :]
        if gi % 2 == 0:
            ms = jnp.mean(x * x, axis=0, keepdims=True)
            x = x * lax.rsqrt(ms + EPS) * gain_col
        parts.append(x)
    return jnp.concatenate(parts, axis=0)


def _nsa_proj_kernel(h_ref, g_ref, wq_ref, wkc_ref, wkvt_ref, wv_ref, wgt_ref, gm_ref, qg_ref,
                     kgs_ref, kgw_ref, q_ref, kvc_ref, kvct_ref, kvst_ref, kvwt_ref, ksn_ref,
                     kwn_ref, vr_ref, gt_ref):
    a = _rmsnorm(h_ref[...], g_ref[...]).astype(BF16)
    vr = _dot(a, wv_ref[...])
    lane = lax.broadcasted_iota(jnp.int32, vr.shape, 1)
    vr_ref[...] = jnp.where(lane % LANES < HEAD_DIM, vr, 1.0).astype(vr_ref.dtype)
    q = _dot(a, wq_ref[...])
    ms = _group_mean_sq(q, gm_ref[...])
    q_ref[...] = ((q * lax.rsqrt(ms + EPS) * qg_ref[...]) * (HEAD_DIM ** -0.5)).astype(q_ref.dtype)
    kvc_ref[...] = _dot(a, wkc_ref[...])
    kvt = _dot_nt(wkvt_ref[...], a)
    kvst = kvt[KV_COLS:2 * KV_COLS]
    kvwt = kvt[2 * KV_COLS:]
    kvct_ref[...] = kvt[:KV_COLS]
    kvst_ref[...] = kvst
    kvwt_ref[...] = kvwt
    ksn_ref[...] = _norm_keys_t(kvst, kgs_ref[...]).astype(ksn_ref.dtype)
    kwn_ref[...] = _norm_keys_t(kvwt, kgw_ref[...]).astype(kwn_ref.dtype)
    gt_ref[...] = jax.nn.sigmoid(_dot(a, wgt_ref[...]))


def _nsa_proj(h2d, n_seq, gain, wq, wkc, wkvt, wv, wgt, gmat, qg, kgs_col, kgw_col):
    n, d = h2d.shape
    s = n // n_seq
    tm = min(TOKEN_TILE, s)
    tps = s // tm
    row = lambda w: pl.BlockSpec((tm, w), lambda i: (i, 0))
    col = pl.BlockSpec((None, KV_COLS, tm), lambda i: (i // tps, 0, i % tps))
    tshape = lambda dt: jax.ShapeDtypeStruct((n_seq, KV_COLS, s), dt)
    n_gate_cols = wgt.shape[1]
    return pl.pallas_call(
        _nsa_proj_kernel,
        grid=(n // tm,),
        in_specs=[
            row(d), _const_spec((1, d)),
            _const_spec(wq.shape), _const_spec(wkc.shape), _const_spec(wkvt.shape),
            _const_spec(wv.shape), _const_spec(wgt.shape), _const_spec(gmat.shape),
            _const_spec(qg.shape), _const_spec(kgs_col.shape), _const_spec(kgw_col.shape),
        ],
        out_specs=[row(d), row(KV_COLS), col, col, col, col, col, row(wv.shape[1]),
                   row(n_gate_cols)],
        out_shape=[
            jax.ShapeDtypeStruct((n, d), BF16),
            jax.ShapeDtypeStruct((n, KV_COLS), F32),
            tshape(F32), tshape(F32), tshape(F32), tshape(BF16), tshape(BF16),
            jax.ShapeDtypeStruct((n, wv.shape[1]), BF16),
            jax.ShapeDtypeStruct((n, n_gate_cols), F32),
        ],
        compiler_params=_cparams(("arbitrary",)),
        name="nsa_proj",
    )(h2d, gain, wq, wkc, wkvt, wv, wgt, gmat, qg, kgs_col, kgw_col)


SLABS = KV_COLS // LANES


def _compress_rows(read_slab, n_blocks, w_ref, gmat, kg_row):
    half = KV_COLS // 2
    per_half = SLABS // 2
    acc = [jnp.zeros((n_blocks, half), F32) for _ in range(2)]
    for l in range(CMP_BLOCK):
        for hf in range(2):
            xl = jnp.concatenate([read_slab(l, hf * per_half + c) for c in range(per_half)], axis=1)
            acc[hf] = acc[hf] + _dot(xl.astype(BF16), w_ref[l, hf])
    kv = jnp.concatenate(acc, axis=1)
    return _norm_keys(kv, kg_row, gmat).astype(BF16)


def _compress_kernel(x_ref, w_ref, gm_ref, kg_ref, o_ref):
    nb = o_ref.shape[0]
    read = lambda l, g: x_ref[pl.ds(l * SLABS + g, nb, stride=CMP_BLOCK * SLABS), :]
    o_ref[...] = _compress_rows(read, nb, w_ref, gm_ref[...], kg_ref[...])


def _compress_prompt(rows2d, w2, gmat, kg):
    n = rows2d.shape[0]
    tr = min(4096, n)
    nb = tr // CMP_BLOCK
    return pl.pallas_call(
        _compress_kernel,
        grid=(n // tr,),
        in_specs=[pl.BlockSpec((tr * SLABS, LANES), lambda i: (i, 0)),
                  _const_spec(w2.shape), _const_spec(gmat.shape), _const_spec(kg.shape)],
        out_specs=pl.BlockSpec((nb, KV_COLS), lambda i: (i, 0)),
        out_shape=jax.ShapeDtypeStruct((n // CMP_BLOCK, KV_COLS), BF16),
        compiler_params=_cparams(("arbitrary",)),
        name="compress_prompt",
    )(rows2d.reshape(n * SLABS, LANES), w2, gmat, kg)


def _compress_paged_kernel(pt_ref, pool_ref, w_ref, gm_ref, kg_ref, o_ref, buf_ref, st_ref, sem_ref,
                           *, pages_per_step, page):
    i = pl.program_id(0)
    n = pl.num_programs(0)

    def page_copy(step, p, slot):
        phys = pt_ref[step * pages_per_step + p]
        return pltpu.make_async_copy(pool_ref.at[phys], buf_ref.at[slot, p], sem_ref.at[slot])

    def start(step, slot):
        def issue(p, c):
            page_copy(step, p, slot).start()
            return c
        lax.fori_loop(0, pages_per_step, issue, 0)

    @pl.when(i == 0)
    def _():
        start(0, 0)

    @pl.when(i + 1 < n)
    def _():
        start(i + 1, (i + 1) % 2)

    slot = i % 2

    def wait(p, c):
        page_copy(i, p, slot).wait()
        return c
    lax.fori_loop(0, pages_per_step, wait, 0)

    def to_rows(p, c):
        for g in range(N_KV_HEADS):
            st_ref[g, pl.ds(pl.multiple_of(p * page, page), page), :] = buf_ref[slot, p, g].T
        return c
    lax.fori_loop(0, pages_per_step, to_rows, 0)

    nb = pages_per_step * page // CMP_BLOCK
    read = lambda l, g: st_ref[g, pl.ds(l, nb, stride=CMP_BLOCK), :]
    o_ref[...] = _compress_rows(read, nb, w_ref, gm_ref[...], kg_ref[...])


def _compress_paged(page_table_flat, pool_t, w2, gmat, kg, pages_per_step):
    n_pages = page_table_flat.shape[0]
    page = pool_t.shape[3]
    assert n_pages % pages_per_step == 0 and page == LANES and pool_t.shape[2] == GROUP_COLS
    steps = n_pages // pages_per_step
    nb = pages_per_step * page // CMP_BLOCK
    grid_spec = pltpu.PrefetchScalarGridSpec(
        num_scalar_prefetch=1,
        grid=(steps,),
        in_specs=[pl.BlockSpec(memory_space=pl.ANY),
                  pl.BlockSpec(w2.shape, lambda i, pt: (0, 0, 0, 0), pipeline_mode=pl.Buffered(1)),
                  pl.BlockSpec(gmat.shape, lambda i, pt: (0, 0)),
                  pl.BlockSpec(kg.shape, lambda i, pt: (0, 0))],
        out_specs=pl.BlockSpec((nb, KV_COLS), lambda i, pt: (i, 0)),
        scratch_shapes=[pltpu.VMEM((2, pages_per_step, N_KV_HEADS, GROUP_COLS, page), F32),
                        pltpu.VMEM((N_KV_HEADS, pages_per_step * page, GROUP_COLS), F32),
                        pltpu.SemaphoreType.DMA((2,))],
    )
    return pl.pallas_call(
        functools.partial(_compress_paged_kernel, pages_per_step=pages_per_step, page=page),
        grid_spec=grid_spec,
        out_shape=jax.ShapeDtypeStruct((steps * nb, KV_COLS), BF16),
        compiler_params=_cparams(("arbitrary",)),
        name="compress_paged",
    )(page_table_flat, pool_t, w2, gmat, kg)


FLASH_ROWS = 128


def _flash_tile(qh_ref, kt_ref, v_ref, kt, tk, state, terms):
    m_ref, acc_ref = state
    start = pl.multiple_of(kt * tk, tk)
    k = kt_ref[:, pl.ds(start, tk)]
    v = v_ref[pl.ds(start, tk), :]
    tq = qh_ref.shape[1]
    blocks = [(r, pl.ds(rb * FLASH_ROWS, FLASH_ROWS))
              for r in range(HEADS_PER_GROUP) for rb in range(tq // FLASH_ROWS)]
    old = [(m_ref[r, rows, :], acc_ref[r, rows, :]) for r, rows in blocks]
    new = []
    for (r, rows), (m_old, acc_old) in zip(blocks, old):
        s = _dot(qh_ref[r, rows, :], k)
        for term in terms(r, rows):
            s = s + term
        m_new = jnp.maximum(m_old, jnp.max(s, axis=-1, keepdims=True))
        e = jnp.exp(s - jnp.concatenate([m_new] * (tk // LANES), axis=1))
        new.append((m_new, jnp.exp(m_old - m_new) * acc_old + _dot(e.astype(BF16), v)))
    for (r, rows), (m_new, acc_new) in zip(blocks, new):
        m_ref[r, rows, :] = m_new
        acc_ref[r, rows, :] = acc_new


def _nsa_prompt_kernel(tbl_ref, thr_ref, q_ref, kc_ref, ks_ref, vs_ref, kw_ref, vw_ref, gt_ref, o_ref,
                       bias_ref, edge_ref, cbias_ref, amask_ref, qh_ref, m_ref, acc_ref, *, n_top):
    g = pl.program_id(0)
    b = pl.program_id(1)
    qi = pl.program_id(2)
    tq = q_ref.shape[0]
    tk = tq
    hpg = HEADS_PER_GROUP
    nc = kc_ref.shape[0]
    ns = nc // CMP_PER_SEL
    blocks_per_tile = tk // SEL_BLOCK
    far = N_BUCKETS - 1

    qpos = qi * tq + lax.broadcasted_iota(jnp.int32, (tq, 1), 0)
    n_io = lax.broadcasted_iota(jnp.int32, (tq, nc), 1)
    relc = qpos - ((n_io + 1) * CMP_BLOCK - 1)

    n_back = WINDOW // tk

    @pl.when((b == 0) & (qi == 0))
    def _():
        diag = (lax.broadcasted_iota(jnp.int32, (tq, tk), 0)
                - lax.broadcasted_iota(jnp.int32, (tq, tk), 1))
        for r in range(hpg):
            h = g * hpg + r
            shifted = lambda k: tbl_ref[k, h] - tbl_ref[far, h]
            bias_ref[r, 0] = jnp.where(diag >= 0, _bucket_bias(diag, thr_ref, shifted), NEG)
            bias_ref[r, 1] = _bucket_bias(diag + tk, thr_ref, shifted)
        edge_ref[...] = jnp.where(diag + n_back * tk < WINDOW, 0.0, NEG)

    @pl.when(b == 0)
    def _():
        for r in range(hpg):
            h = g * hpg + r
            cbias_ref[r, qi] = _bucket_bias(relc, thr_ref, lambda k: tbl_ref[k, h])

    q = q_ref[...]
    for r in range(hpg):
        qh_ref[r] = q[:, r * HEAD_DIM:(r + 1) * HEAD_DIM]

    kc = kc_ref[...]
    kck = kc[:, :HEAD_DIM]
    kcv = kc[:, HEAD_DIM:]
    maskc = relc >= 0
    imp = None
    oc = []
    for r in range(hpg):
        s = _dot_nt(q[:, r * HEAD_DIM:(r + 1) * HEAD_DIM], kck) + cbias_ref[r, qi]
        sm = jnp.where(maskc, s, NEG)
        e = jnp.exp(sm - jnp.max(sm, axis=-1, keepdims=True))
        p = jnp.where(maskc, e / jnp.sum(e, axis=-1, keepdims=True), 0.0)
        imp = p if imp is None else imp + p
        oc.append(_dot(p.astype(BF16), kcv))

    pair_t = (lax.broadcasted_iota(jnp.int32, (ns, nc), 0)
              == lax.broadcasted_iota(jnp.int32, (ns, nc), 1) // CMP_PER_SEL).astype(BF16)
    hi, mid, lo = _split3(imp)
    imp_sel = (_dot_nt(pair_t, hi) + _dot_nt(pair_t, mid)) + _dot_nt(pair_t, lo)
    blk = lax.broadcasted_iota(jnp.int32, (ns, tq), 0)
    cur = (qi * tq + lax.broadcasted_iota(jnp.int32, (1, tq), 1)) // SEL_BLOCK
    forced = (blk == 0) | (blk == cur) | (blk == cur - 1)
    score = jnp.where(forced, FORCED_SCORE, jnp.where(blk <= cur, imp_sel, -1.0))
    rank = jnp.zeros((ns, tq), F32)
    for i in range(ns):
        si = score[i:i + 1, :]
        beats = (si > score) | ((si == score) & (blk > i))
        rank = rank + jnp.where(beats, 1.0, 0.0)
    sel_t = jnp.where(rank < n_top, 1.0, 0.0).astype(BF16)
    eye = (lax.broadcasted_iota(jnp.int32, (tq, tq), 0)
           == lax.broadcasted_iota(jnp.int32, (tq, tq), 1)).astype(BF16)
    sel = _dot_nt(eye, sel_t).astype(BF16)

    state = (m_ref, acc_ref)

    def init():
        m_ref[...] = jnp.full(m_ref.shape, NEG, F32)
        acc_ref[...] = jnp.zeros(acc_ref.shape, F32)

    def finish():
        out = []
        for r in range(hpg):
            acc = acc_ref[r]
            out.append((acc / pltpu.roll(acc, HEAD_DIM, 1))[:, :HEAD_DIM])
        return out

    def sel_tile(kt, tiles_back):
        expand = (lax.broadcasted_iota(jnp.int32, (ns, tk), 0)
                  == kt * blocks_per_tile + lax.broadcasted_iota(jnp.int32, (ns, tk), 1) // SEL_BLOCK)
        amask_ref[...] = jnp.where(_dot(sel, expand.astype(BF16)) > 0.5, 0.0, NEG)
        if tiles_back is None:
            terms = lambda r, rows: (amask_ref[rows, :],)
        else:
            terms = lambda r, rows: (bias_ref[r, tiles_back, rows, :], amask_ref[rows, :])
        _flash_tile(qh_ref,ks_ref, vs_ref, kt, tk, state, terms)

    init()

    def far_tile(kt, c):
        sel_tile(kt, None)
        return c
    lax.fori_loop(0, jnp.maximum(qi - 1, 0), far_tile, 0)

    @pl.when(qi >= 1)
    def _():
        sel_tile(qi - 1, 1)
    sel_tile(qi, 0)
    o_sel = finish()

    init()

    @pl.when(qi >= n_back)
    def _():
        _flash_tile(qh_ref,kw_ref, vw_ref, qi - n_back, tk, state,
                    lambda r, rows: (edge_ref[rows, :],))

    @pl.when(qi >= 1)
    def _():
        _flash_tile(qh_ref,kw_ref, vw_ref, qi - 1, tk, state,
                    lambda r, rows: (bias_ref[r, 1, rows, :],))
    _flash_tile(qh_ref,kw_ref, vw_ref, qi, tk, state, lambda r, rows: (bias_ref[r, 0, rows, :],))
    o_win = finish()

    width = hpg * HEAD_DIM
    idx = lax.broadcasted_iota(jnp.int32, (LANES, N_BRANCHES * width), 0)
    col = lax.broadcasted_iota(jnp.int32, (LANES, N_BRANCHES * width), 1)
    spread = (idx == (col // width) * hpg + (col % width) // HEAD_DIM).astype(BF16)
    gates = _exact_dot(gt_ref[...], spread)
    out = None
    for c, branch in enumerate((oc, o_sel, o_win)):
        term = gates[:, c * width:(c + 1) * width] * jnp.concatenate(branch, axis=1)
        out = term if out is None else out + term
    o_ref[...] = out.astype(o_ref.dtype)


def _nsa_prompt_attn(tbl, thr, q, kcn, ksn_t, kwn_t, v_rows, gates, b, s):
    tq = min(ATTN_TILE, s)
    assert WINDOW == 2 * tq and tq + 1 >= MAX_DISTANCE and tq % FLASH_ROWS == 0
    nq = s // tq
    nc = s // CMP_BLOCK
    ns = s // SEL_BLOCK
    n_top = min(N_SELECTED, ns)
    hpg = HEADS_PER_GROUP
    k_spec = pl.BlockSpec((None, HEAD_DIM, s), lambda g, i, j: (i, 2 * g, 0))
    v_spec = lambda branch: pl.BlockSpec((s, LANES), lambda g, i, j: (i, branch * N_KV_HEADS + g))
    return pl.pallas_call(
        functools.partial(_nsa_prompt_kernel, n_top=n_top),
        grid=(N_KV_HEADS, b, nq),
        in_specs=[
            _smem_spec(), _smem_spec(),
            pl.BlockSpec((tq, Q_GROUP_COLS), lambda g, i, j: (i * nq + j, g)),
            pl.BlockSpec((nc, GROUP_COLS), lambda g, i, j: (i, g)),
            k_spec, v_spec(0), k_spec, v_spec(1),
            pl.BlockSpec((tq, LANES), lambda g, i, j: (i * nq + j, g)),
        ],
        out_specs=pl.BlockSpec((tq, Q_GROUP_COLS), lambda g, i, j: (i * nq + j, g)),
        out_shape=jax.ShapeDtypeStruct((b * s, D_MODEL), BF16),
        scratch_shapes=[pltpu.VMEM((hpg, 2, tq, tq), F32),
                        pltpu.VMEM((tq, tq), F32),
                        pltpu.VMEM((hpg, nq, tq, nc), F32),
                        pltpu.VMEM((tq, tq), F32),
                        pltpu.VMEM((hpg, tq, HEAD_DIM), BF16),
                        pltpu.VMEM((hpg, tq, LANES), F32),
                        pltpu.VMEM((hpg, tq, LANES), F32)],
        compiler_params=_cparams(("arbitrary", "arbitrary", "arbitrary")),
        name="nsa_prompt_attn",
    )(tbl, thr, q, kcn, ksn_t, v_rows, kwn_t, v_rows, gates)


def _row_bias(dist_row, thr_ref, tbl_rows):
    n = dist_row.shape[1]
    bucket = jnp.zeros(dist_row.shape, jnp.int32)
    for k in range(1, N_BUCKETS):
        bucket = bucket + jnp.where(dist_row >= thr_ref[k], 1, 0)
    onehot = (lax.broadcasted_iota(jnp.int32, (N_BUCKETS, n), 0) == bucket).astype(BF16)
    return _exact_dot(tbl_rows, onehot)


def _sample_cmp_kernel(thr_ref, q_ref, kc_ref, tbl_ref, oc_ref, idx_ref, *, past_len, seq, n_top):
    nc = kc_ref.shape[0]
    ns = past_len // SEL_BLOCK
    hpg = HEADS_PER_GROUP
    rows = hpg * seq
    kc = kc_ref[...]
    pair = (lax.broadcasted_iota(jnp.int32, (nc, ns), 0) // CMP_PER_SEL
            == lax.broadcasted_iota(jnp.int32, (nc, ns), 1)).astype(BF16)
    eye = (lax.broadcasted_iota(jnp.int32, (ns, ns), 0)
           == lax.broadcasted_iota(jnp.int32, (ns, ns), 1))
    eye_bf = eye.astype(BF16)
    ii = lax.broadcasted_iota(jnp.int32, (ns, ns), 0)
    jj = lax.broadcasted_iota(jnp.int32, (ns, ns), 1)
    t_row = lax.broadcasted_iota(jnp.int32, (rows, nc), 0) % seq
    n_io = lax.broadcasted_iota(jnp.int32, (rows, nc), 1)
    rel = past_len + t_row - ((n_io + 1) * CMP_BLOCK - 1)
    mask = rel >= 0
    blk = lax.broadcasted_iota(jnp.int32, (SUBLANES, ns), 1)
    forced = (blk == 0) | (blk == ns - 1)
    rank_lane = lax.broadcasted_iota(jnp.int32, (ns, LANES), 1).astype(F32)
    blk_col = lax.broadcasted_iota(jnp.int32, (ns, LANES), 0).astype(F32)
    for g in range(N_KV_HEADS):
        kck = kc[:, g * GROUP_COLS:g * GROUP_COLS + HEAD_DIM]
        kcv = kc[:, g * GROUP_COLS + HEAD_DIM:(g + 1) * GROUP_COLS]
        qg = q_ref[g].astype(BF16)
        tbl_g = tbl_ref[g]
        s = _dot_nt(qg, kck)
        s = s + _bucket_bias(rel, thr_ref, lambda k: tbl_g[:, k:k + 1])
        sm = jnp.where(mask, s, NEG)
        e = jnp.exp(sm - jnp.max(sm, axis=-1, keepdims=True))
        p = jnp.where(mask, e / jnp.sum(e, axis=-1, keepdims=True), 0.0)
        oc_ref[g] = _dot(p.astype(BF16), kcv)
        imp = p[0:seq]
        for r in range(1, hpg):
            imp = imp + p[r * seq:(r + 1) * seq]
        imp = jnp.concatenate([imp, jnp.zeros((SUBLANES - seq, nc), F32)], axis=0)
        score = jnp.where(forced, FORCED_SCORE, _exact_dot(imp, pair))
        hi, mid, lo = _split3(score)
        score_t = (_dot_nt(eye_bf, hi) + _dot_nt(eye_bf, mid)) + _dot_nt(eye_bf, lo)
        for t in range(seq):
            s_row = score[t:t + 1, :]
            s_col = score_t[:, t:t + 1]
            beats = (ii != jj) & ((s_row > s_col) | ((s_row == s_col) & (jj < ii)))
            rank_col = jnp.sum(jnp.where(beats, 1.0, 0.0), axis=1, keepdims=True)
            onehot = rank_col == rank_lane
            idx_row = jnp.sum(jnp.where(onehot, blk_col, 0.0), axis=0, keepdims=True)
            idx_ref[g * seq + t:g * seq + t + 1, :] = idx_row.astype(jnp.int32)


def _sample_cmp(thr, q_rt, kcn, tbl_rt, b, past_len, seq):
    nc = kcn.shape[0] // b
    ns = past_len // SEL_BLOCK
    n_top = min(N_SELECTED - 1, ns)
    rows = HEADS_PER_GROUP * seq
    return pl.pallas_call(
        functools.partial(_sample_cmp_kernel, past_len=past_len, seq=seq, n_top=n_top),
        grid=(b,),
        in_specs=[
            _smem_spec(),
            pl.BlockSpec((None, N_KV_HEADS, rows, HEAD_DIM), lambda i: (i, 0, 0, 0)),
            pl.BlockSpec((nc, KV_COLS), lambda i: (i, 0)),
            _const_spec(tbl_rt.shape),
        ],
        out_specs=[
            pl.BlockSpec((None, N_KV_HEADS, rows, HEAD_DIM), lambda i: (i, 0, 0, 0)),
            pl.BlockSpec((None, N_KV_HEADS * seq, LANES), lambda i: (i, 0, 0)),
        ],
        out_shape=[
            jax.ShapeDtypeStruct((b, N_KV_HEADS, rows, HEAD_DIM), F32),
            jax.ShapeDtypeStruct((b, N_KV_HEADS * seq, LANES), jnp.int32),
        ],
        compiler_params=_cparams(("arbitrary",)),
        name="sample_cmp",
    )(thr, q_rt, kcn, tbl_rt)


def _norm_k_cols(kt, gain_col):
    ms = jnp.mean(kt * kt, axis=0, keepdims=True)
    return (kt * lax.rsqrt(ms + EPS) * gain_col).astype(BF16)


def _sample_sel_win_kernel(idx_ref, pt_ref, thr_ref, pool_ref, q_ref, ksnew_ref, kwnew_ref,
                           win_ref, tbl_ref, kgs_ref, kgw_ref, os_ref, ow_ref,
                           buf_ref, sem_ref, *, past_len, seq, n_top, page):
    i = pl.program_id(0)
    n_pages = past_len // page
    blocks_per_page = page // SEL_BLOCK
    n_keys = n_top * page
    stride = n_top + 1

    def block_of(g, t, j):
        return idx_ref[(i * N_KV_HEADS * seq + g * seq + t) * stride + j]

    def tile_copy(g, t, j):
        phys = pt_ref[i * n_pages + block_of(g, t, j) // blocks_per_page]
        return pltpu.make_async_copy(
            pool_ref.at[phys, g],
            buf_ref.at[g * seq + t, :, pl.ds(pl.multiple_of(j * page, page), page)],
            sem_ref.at[0])

    def for_all_tiles(fn):
        for g in range(N_KV_HEADS):
            def per_tj(tj, c):
                fn(tile_copy(g, tj // n_top, tj % n_top))
                return c
            lax.fori_loop(0, seq * n_top, per_tj, 0)

    for_all_tiles(lambda cp: cp.start())

    wb = win_ref.shape[2]
    wk = [_norm_k_cols(win_ref[g, 0:HEAD_DIM, :], kgw_ref[...]) for g in range(N_KV_HEADS)]
    wv = [win_ref[g, HEAD_DIM:GROUP_COLS, :].astype(BF16) for g in range(N_KV_HEADS)]
    j_io = lax.broadcasted_iota(jnp.int32, (1, wb), 1)
    tnew = lax.broadcasted_iota(jnp.int32, (1, SUBLANES), 1)

    def attend(q, tbl_g, k_past, v_past, dist, mask, knew, distn, maskn):
        s1 = jnp.where(mask, _dot(q, k_past) + _row_bias(dist, thr_ref, tbl_g), NEG)
        s2 = jnp.where(maskn, _dot_nt(q, knew[:, :HEAD_DIM].astype(BF16))
                       + _row_bias(distn, thr_ref, tbl_g), NEG)
        m = jnp.maximum(jnp.max(s1, axis=-1, keepdims=True), jnp.max(s2, axis=-1, keepdims=True))
        e1 = jnp.where(mask, jnp.exp(s1 - m), 0.0)
        e2 = jnp.where(maskn, jnp.exp(s2 - m), 0.0)
        l = jnp.sum(e1, axis=-1, keepdims=True) + jnp.sum(e2, axis=-1, keepdims=True)
        o = _dot_nt(e1.astype(BF16), v_past) + _dot(e2.astype(BF16),
                                                    knew[:, HEAD_DIM:].astype(BF16))
        return o / l

    def window(t, c):
        dist = wb + t - j_io
        maskw = (dist >= 0) & (dist < WINDOW)
        distn = t - tnew
        maskn = (distn >= 0) & (distn < WINDOW)
        for g in range(N_KV_HEADS):
            ow_ref[g, t] = attend(q_ref[g, t].astype(BF16), tbl_ref[g], wk[g], wv[g], dist, maskw,
                                  kwnew_ref[g], distn, maskn)
        return c
    lax.fori_loop(0, seq, window, 0)

    for_all_tiles(lambda cp: cp.wait())

    lane = lax.broadcasted_iota(jnp.int32, (1, n_keys), 1)
    tile_of_lane = lane // page
    row_in_page = lane % page

    def selected(t, c):
        distn = t - tnew
        maskn = distn >= 0
        for g in range(N_KV_HEADS):
            kvt = buf_ref[g * seq + t]
            page_base = jnp.zeros((1, n_keys), jnp.int32)
            half = jnp.zeros((1, n_keys), jnp.int32)
            for j in range(n_top):
                blk = block_of(g, t, j)
                page_base = jnp.where(tile_of_lane == j, (blk // blocks_per_page) * page, page_base)
                half = jnp.where(tile_of_lane == j, blk % blocks_per_page, half)
            mask = (row_in_page // SEL_BLOCK) == half
            dist = past_len + t - (page_base + row_in_page)
            os_ref[g, t] = attend(q_ref[g, t].astype(BF16), tbl_ref[g],
                                  _norm_k_cols(kvt[0:HEAD_DIM], kgs_ref[...]),
                                  kvt[HEAD_DIM:GROUP_COLS].astype(BF16), dist, mask,
                                  ksnew_ref[g], distn, maskn)
        return c
    lax.fori_loop(0, seq, selected, 0)


def _sample_sel_win(idx_flat, pt_flat, thr, pool_t, q_tr, ksnew, kwnew, win_t, tbl_r8,
                    kgs_col, kgw_col, b, past_len, seq):
    ns = past_len // SEL_BLOCK
    n_top = min(N_SELECTED - 1, ns)
    page = pool_t.shape[3]
    wb = win_t.shape[3]
    o_shape = jax.ShapeDtypeStruct((b, N_KV_HEADS, seq, SUBLANES, HEAD_DIM), F32)
    o_spec = pl.BlockSpec((None, N_KV_HEADS, seq, SUBLANES, HEAD_DIM),
                          lambda i, *_: (i, 0, 0, 0, 0))
    new_spec = pl.BlockSpec((None, N_KV_HEADS, SUBLANES, GROUP_COLS), lambda i, *_: (i, 0, 0, 0))
    grid_spec = pltpu.PrefetchScalarGridSpec(
        num_scalar_prefetch=3,
        grid=(b,),
        in_specs=[
            pl.BlockSpec(memory_space=pl.ANY),
            pl.BlockSpec((None, N_KV_HEADS, seq, SUBLANES, HEAD_DIM), lambda i, *_: (i, 0, 0, 0, 0)),
            new_spec, new_spec,
            pl.BlockSpec((None, N_KV_HEADS, GROUP_COLS, wb), lambda i, *_: (i, 0, 0, 0)),
            pl.BlockSpec(tbl_r8.shape, lambda i, *_: (0, 0, 0)),
            pl.BlockSpec(kgs_col.shape, lambda i, *_: (0, 0)),
            pl.BlockSpec(kgw_col.shape, lambda i, *_: (0, 0)),
        ],
        out_specs=[o_spec, o_spec],
        scratch_shapes=[pltpu.VMEM((N_KV_HEADS * seq, GROUP_COLS, n_top * page), F32),
                        pltpu.SemaphoreType.DMA((1,))],
    )
    return pl.pallas_call(
        functools.partial(_sample_sel_win_kernel, past_len=past_len, seq=seq, n_top=n_top,
                          page=page),
        grid_spec=grid_spec,
        out_shape=[o_shape, o_shape],
        compiler_params=_cparams(("arbitrary",)),
        name="sample_sel_win",
    )(idx_flat, pt_flat, thr, pool_t, q_tr, ksnew, kwnew, win_t, tbl_r8, kgs_col, kgw_col)


def _route(f, rhi_ref, rlo_ref):
    f_hi = f.astype(BF16)
    f_lo = (f - f_hi.astype(F32)).astype(BF16)
    logits = (_dot(f_hi, rhi_ref[...]) + _dot(f_lo, rhi_ref[...])) + _dot(f_hi, rlo_ref[...])
    lane = lax.broadcasted_iota(jnp.int32, logits.shape, 1).astype(F32)
    logits = jnp.where(lane < N_EXPERTS, logits, -jnp.inf)
    m1 = jnp.max(logits, axis=-1, keepdims=True)
    i1 = jnp.min(jnp.where(logits == m1, lane, float(LANES)), axis=-1, keepdims=True)
    rest = jnp.where(lane == i1, -jnp.inf, logits)
    m2 = jnp.max(rest, axis=-1, keepdims=True)
    i2 = jnp.min(jnp.where(rest == m2, lane, float(LANES)), axis=-1, keepdims=True)
    e2 = jnp.exp(m2 - m1)
    denom = 1.0 + e2
    return jnp.where(lane == i1, 1.0 / denom, 0.0) + jnp.where(lane == i2, e2 / denom, 0.0)


def _attn_out_kernel(h_ref, o_ref, wout_ref, gf_ref, rhi_ref, rlo_ref, h2_ref, f_ref, gate_ref):
    h2 = h_ref[...] + _dot(o_ref[...], wout_ref[...])
    h2_ref[...] = h2
    f = _rmsnorm(h2, gf_ref[...])
    f_ref[...] = f.astype(f_ref.dtype)
    gate_ref[...] = _route(f, rhi_ref, rlo_ref)


def _attn_out_merge_kernel(h_ref, oc_ref, os_ref, ow_ref, gt_ref, wout_ref, gf_ref, rhi_ref,
                           rlo_ref, h2_ref, f_ref, gate_ref):
    gates = gt_ref[...]
    n_gate = gates.shape[1]
    hpg = HEADS_PER_GROUP
    idx = lax.broadcasted_iota(jnp.int32, (n_gate, D_MODEL), 0)
    head = lax.broadcasted_iota(jnp.int32, (n_gate, D_MODEL), 1) // HEAD_DIM
    o = None
    for c, ref in enumerate((oc_ref, os_ref, ow_ref)):
        lane_of_gate = (head // hpg) * LANES + c * hpg + head % hpg
        gate = _exact_dot(gates, (idx == lane_of_gate).astype(BF16))
        term = gate * ref[...]
        o = term if o is None else o + term
    h2 = h_ref[...] + _dot(o.astype(BF16), wout_ref[...])
    h2_ref[...] = h2
    f = _rmsnorm(h2, gf_ref[...])
    f_ref[...] = f.astype(f_ref.dtype)
    gate_ref[...] = _route(f, rhi_ref, rlo_ref)


def _attn_out(h2d, o_list, gates, wout, gf, rhi, rlo):
    n, d = h2d.shape
    tm = min(TOKEN_TILE, n)
    row = lambda w: pl.BlockSpec((tm, w), lambda i: (i, 0))
    merged = len(o_list) == 1
    body = _attn_out_kernel if merged else _attn_out_merge_kernel
    in_specs = [row(d)] + [row(d)] * len(o_list) + ([] if merged else [row(gates.shape[1])])
    in_specs += [_const_spec(wout.shape), _const_spec((1, d)), _const_spec(rhi.shape),
                 _const_spec(rlo.shape)]
    args = [h2d] + list(o_list) + ([] if merged else [gates]) + [wout, gf, rhi, rlo]
    return pl.pallas_call(
        body,
        grid=(n // tm,),
        in_specs=in_specs,
        out_specs=[row(d), row(d), row(LANES)],
        out_shape=[jax.ShapeDtypeStruct((n, d), F32), jax.ShapeDtypeStruct((n, d), BF16),
                   jax.ShapeDtypeStruct((n, LANES), F32)],
        compiler_params=_cparams(("arbitrary",)),
        name="attn_out_route",
    )(*args)


def _moe_kernel(f_ref, h2_ref, gate_ref, p_ref, wg_ref, wu_ref, wd_ref, gp_ref, wpg_ref, wpp_ref,
                o_ref, acc_ref):
    e = pl.program_id(1)

    @pl.when(e == 0)
    def _():
        acc_ref[...] = jnp.zeros_like(acc_ref)

    f = f_ref[...]
    gate = gate_ref[...]
    lane = lax.broadcasted_iota(jnp.int32, gate.shape, 1)
    ge = jnp.sum(jnp.where(lane == e, gate, 0.0), axis=-1, keepdims=True)
    hid = _silu(_dot(f, wg_ref[...])) * _dot(f, wu_ref[...])
    acc_ref[...] += ge * _dot(hid.astype(BF16), wd_ref[...])

    @pl.when(e == pl.num_programs(1) - 1)
    def _():
        o_ref[...] = _ple(h2_ref[...] + acc_ref[...], p_ref[...], gp_ref[...], wpg_ref, wpp_ref)


def _moe(f2d, h2, gate, p2d, wg, wu, wd, gp, wpg, wpp):
    n, d = h2.shape
    tm = min(TOKEN_TILE, n)
    n_e, _, dff = wg.shape
    row = lambda w: pl.BlockSpec((tm, w), lambda i, e: (i, 0))
    return pl.pallas_call(
        _moe_kernel,
        grid=(n // tm, n_e),
        in_specs=[
            row(d), row(d), row(LANES), row(p2d.shape[1]),
            pl.BlockSpec((None, d, dff), lambda i, e: (e, 0, 0)),
            pl.BlockSpec((None, d, dff), lambda i, e: (e, 0, 0)),
            pl.BlockSpec((None, dff, d), lambda i, e: (e, 0, 0)),
            _const_spec((1, d)), _const_spec(wpg.shape), _const_spec(wpp.shape),
        ],
        out_specs=row(d),
        out_shape=jax.ShapeDtypeStruct((n, d), F32),
        scratch_shapes=[pltpu.VMEM((tm, d), F32)],
        compiler_params=_cparams(("arbitrary", "arbitrary")),
        name="moe_ple",
    )(f2d, h2, gate, p2d, wg, wu, wd, gp, wpg, wpp)


def _bucket_thresholds():
    n = jnp.arange(MAX_DISTANCE + 1)
    max_exact = N_BUCKETS // 2
    nf = jnp.maximum(n, 1).astype(F32)
    large = max_exact + (jnp.log(nf / max_exact) / math.log(MAX_DISTANCE / max_exact)
                         * (N_BUCKETS - max_exact)).astype(jnp.int32)
    bucket = jnp.where(n < max_exact, n, jnp.minimum(large, N_BUCKETS - 1))
    return jnp.sum(bucket[None, :] < jnp.arange(N_BUCKETS)[:, None], axis=1).astype(jnp.int32)


def _row(v):
    return v.reshape(1, -1).astype(F32)


def _key_gain_row(k_gain, cols):
    g = jnp.concatenate([k_gain, jnp.ones_like(k_gain)])
    return jnp.tile(g, cols // GROUP_COLS).reshape(1, cols).astype(F32)


def _rows_on_lanes(x):
    lead = x.shape[:-4]
    rows = x.shape[-4]
    nl = len(lead)
    perm = tuple(range(nl)) + (nl + 1, nl + 2, nl + 3, nl)
    return jnp.transpose(x, perm).reshape(*lead, N_KV_HEADS, GROUP_COLS, rows)


def _rows_on_lanes_inverse(xt):
    b, _, rows = xt.shape
    x = xt.reshape(b, N_KV_HEADS, 2, HEAD_DIM, rows)
    return jnp.transpose(x, (0, 4, 1, 2, 3))[None]


def kernel(x_prompt, x_sample, state_conv, cache_cmp, cache_sel, state_win, page_table,
           p_prompt, p_sample, norm_mix, norm_ffn, norm_ple, conv_w_in, conv_w, conv_w_out,
           nsa_w_in, nsa_w_cmp, nsa_q_norm, nsa_k_norm, nsa_w_out, rel_bias,
           ffn_w_gate, ffn_w_up, ffn_w_down, moe_router, moe_w_gate, moe_w_up, moe_w_down,
           ple_w_proj, ple_w_gate):
    bp, sp, d = x_prompt.shape
    bs, ss, _ = x_sample.shape
    n_p = bp * sp
    n_s = bs * ss
    page = cache_cmp.shape[2]
    past_len = page_table.shape[1] * page
    hpg = HEADS_PER_GROUP
    bf = lambda w: w.astype(BF16)

    cw = jnp.zeros((SUBLANES, d), F32).at[:CONV_WIDTH].set(conv_w[0])
    w_in0, w_out0 = bf(conv_w_in[0]), bf(conv_w_out[0])
    g_mix0 = _row(norm_mix[0])
    h_p, tail_p = _mix0_prompt(x_prompt, g_mix0, w_in0, cw, w_out0)
    conv_prompt = tail_p[:, SUBLANES - (CONV_WIDTH - 1):][None]

    st = state_conv[0]
    zeros = jnp.zeros((bs, ss, d), F32)
    s1 = zeros.at[:, 0].set(st[:, 1]).reshape(n_s, d)
    s2 = zeros.at[:, 0].set(st[:, 0]).at[:, 1].set(st[:, 1]).reshape(n_s, d)
    h_s, u_s = _mix0_sample(x_sample.reshape(n_s, d), g_mix0, w_in0, cw, w_out0, s1, s2, ss)
    conv_sample = u_s.reshape(bs, ss, d)[:, ss - (CONV_WIDTH - 1):][None]

    ffn0_w = (_row(norm_ffn[0]), bf(ffn_w_gate[0]), bf(ffn_w_up[0]), bf(ffn_w_down[0]),
              _row(norm_ple[0]), bf(ple_w_gate[0]), bf(ple_w_proj[0]))
    h_p = _ffn0(h_p.reshape(n_p, d), p_prompt[0].reshape(n_p, -1), *ffn0_w)
    h_s = _ffn0(h_s, p_sample[0].reshape(n_s, -1), *ffn0_w)

    q_cols = N_HEADS * HEAD_DIM
    kv_cols = N_BRANCHES * KV_COLS
    w_in1 = nsa_w_in[0]
    wq = bf(w_in1[:, :q_cols])
    wkv = w_in1[:, q_cols:q_cols + kv_cols]
    wkc = bf(wkv[:, :KV_COLS])
    wkvt = bf(wkv.T)
    wv_src = wkv[:, KV_COLS:].reshape(d, 2 * N_KV_HEADS, 2, HEAD_DIM)[:, :, 1]
    wv = bf(jnp.zeros((d, 2 * N_KV_HEADS, LANES), F32).at[:, :, :HEAD_DIM].set(wv_src).reshape(
        d, 2 * KV_COLS))
    wg_src = w_in1[:, q_cols + kv_cols:].reshape(d, N_BRANCHES, N_KV_HEADS, hpg)
    wgt = jnp.zeros((d, N_KV_HEADS, LANES), F32).at[:, :, :N_BRANCHES * hpg].set(
        wg_src.transpose(0, 2, 1, 3).reshape(d, N_KV_HEADS, N_BRANCHES * hpg))
    wgt = bf(wgt.reshape(d, N_KV_HEADS * LANES))
    gi = jnp.arange(MXU_DIM) // HEAD_DIM
    gmat = (gi[:, None] == gi[None, :]).astype(BF16)
    qg = jnp.tile(nsa_q_norm[0], N_HEADS).reshape(1, q_cols).astype(F32)
    kgc = _key_gain_row(nsa_k_norm[0, 0], KV_COLS)
    kgs_col = nsa_k_norm[0, 1].reshape(HEAD_DIM, 1).astype(F32)
    kgw_col = nsa_k_norm[0, 2].reshape(HEAD_DIM, 1).astype(F32)
    proj_w = (_row(norm_mix[1]), wq, wkc, wkvt, wv, wgt, gmat, qg, kgs_col, kgw_col)

    per_tile = MXU_DIM // HEAD_DIM
    wc = nsa_w_cmp[0].reshape(CMP_BLOCK, 2, per_tile, HEAD_DIM, HEAD_DIM)
    w2 = bf(jnp.einsum('lhade,ab->lhadbe', wc, jnp.eye(per_tile, dtype=F32)).reshape(
        CMP_BLOCK, 2, MXU_DIM, MXU_DIM))

    thr = _bucket_thresholds()
    tbl = rel_bias.astype(F32)

    q_p, kvc_p, kvct_p, kvst_p, kvwt_p, ksn_p, kwn_p, vr_p, gt_p = _nsa_proj(h_p, bp, *proj_w)
    kcn_p = _compress_prompt(kvc_p, w2, gmat, kgc)
    o_p = _nsa_prompt_attn(tbl, thr, q_p, kcn_p, ksn_p, kwn_p, vr_p, gt_p, bp, sp)

    q_s, _, kvct_s, kvst_s, kvwt_s, ksn_s, kwn_s, _, gt_s = _nsa_proj(h_s, 1, *proj_w)
    pt_flat = page_table.reshape(-1).astype(jnp.int32)
    kcn_s = _compress_paged(pt_flat, _rows_on_lanes(cache_cmp[0]), w2, gmat, kgc, PAGES_PER_STEP)
    q5 = q_s.astype(F32).reshape(bs, ss, N_KV_HEADS, hpg, HEAD_DIM)
    q_rt = q5.transpose(0, 2, 3, 1, 4).reshape(bs, N_KV_HEADS, hpg * ss, HEAD_DIM)
    q_tr = jnp.zeros((bs, N_KV_HEADS, ss, SUBLANES, HEAD_DIM), F32).at[:, :, :, :hpg].set(
        q5.transpose(0, 2, 1, 3, 4))
    tbl_gr = tbl.T.reshape(N_KV_HEADS, hpg, N_BUCKETS)
    tbl_rt = jnp.repeat(tbl_gr, ss, axis=1)
    tbl_r8 = jnp.zeros((N_KV_HEADS, SUBLANES, N_BUCKETS), F32).at[:, :hpg].set(tbl_gr)
    oc_s, idx = _sample_cmp(thr, q_rt, kcn_s, tbl_rt, bs, past_len, ss)
    n_top_s = min(N_SELECTED - 1, past_len // SEL_BLOCK)
    idx_flat = idx[:, :, :n_top_s + 1].reshape(-1)

    def new_rows(kn_t):
        r = kn_t[0].astype(F32).reshape(N_KV_HEADS, GROUP_COLS, bs, ss).transpose(2, 0, 3, 1)
        return jnp.zeros((bs, N_KV_HEADS, SUBLANES, GROUP_COLS), F32).at[:, :, :ss].set(r)

    win_t = _rows_on_lanes(state_win[0])
    os_s, ow_s = _sample_sel_win(idx_flat, pt_flat, thr, _rows_on_lanes(cache_sel[0]), q_tr,
                                 new_rows(ksn_s), new_rows(kwn_s), win_t, tbl_r8,
                                 kgs_col, kgw_col, bs, past_len, ss)
    oc_s2 = oc_s.reshape(bs, N_KV_HEADS, hpg, ss, HEAD_DIM).transpose(0, 3, 1, 2, 4).reshape(n_s, d)
    to2d = lambda o: o[:, :, :, :hpg].transpose(0, 2, 1, 3, 4).reshape(n_s, d)

    w_out1 = bf(nsa_w_out[0])
    router = jnp.zeros((d, LANES), F32).at[:, :N_EXPERTS].set(moe_router[0])
    r_hi = bf(router)
    r_lo = bf(router - r_hi.astype(F32))
    gf1 = _row(norm_ffn[1])
    moe_w = (bf(moe_w_gate[0]), bf(moe_w_up[0]), bf(moe_w_down[0]), _row(norm_ple[1]),
             bf(ple_w_gate[1]), bf(ple_w_proj[1]))
    h2_p, f_p, gate_p = _attn_out(h_p, [o_p], None, w_out1, gf1, r_hi, r_lo)
    y_p = _moe(f_p, h2_p, gate_p, p_prompt[1].reshape(n_p, -1), *moe_w)
    h2_s, f_s, gate_s = _attn_out(h_s, [oc_s2, to2d(os_s), to2d(ow_s)], gt_s, w_out1, gf1,
                                  r_hi, r_lo)
    y_s = _moe(f_s, h2_s, gate_s, p_sample[1].reshape(n_s, -1), *moe_w)

    sample_rows = lambda xt: _rows_on_lanes_inverse(
        xt[0].reshape(KV_COLS, bs, ss).transpose(1, 0, 2))
    wbp = min(WINDOW, sp)
    win_s_t = jnp.concatenate(
        [win_t, kvwt_s[0].reshape(N_KV_HEADS, GROUP_COLS, bs, ss).transpose(2, 0, 1, 3)],
        axis=3)[..., ss:]
    return (y_p.reshape(bp, sp, d), y_s.reshape(bs, ss, d), conv_prompt, conv_sample,
            _rows_on_lanes_inverse(kvct_p), sample_rows(kvct_s),
            _rows_on_lanes_inverse(kvst_p), sample_rows(kvst_s),
            _rows_on_lanes_inverse(kvwt_p[:, :, sp - wbp:]),
            _rows_on_lanes_inverse(win_s_t.reshape(bs, KV_COLS, -1)))
```

```python
import functools
import math

import jax
import jax.numpy as jnp
from jax import lax
from jax.experimental import pallas as pl
from jax.experimental.pallas import tpu as pltpu

F32 = jnp.float32
BF16 = jnp.bfloat16

D_MODEL = 1024
N_HEADS = 16
HEAD_DIM = 64
N_KV_HEADS = 4
HEADS_PER_GROUP = 4
N_BRANCHES = 3
CMP_BLOCK = 32
SEL_BLOCK = 64
CMP_PER_SEL = SEL_BLOCK // CMP_BLOCK
N_SELECTED = 16
WINDOW = 512
N_BUCKETS = 32
MAX_DISTANCE = 128
N_EXPERTS = 8
CONV_WIDTH = 3
EPS = 1e-6
FORCED_SCORE = 1e4
NEG = -1e30
KV_COLS = N_KV_HEADS * 2 * HEAD_DIM
GROUP_COLS = 2 * HEAD_DIM
Q_GROUP_COLS = HEADS_PER_GROUP * HEAD_DIM

LANES = 128
SUBLANES = 8
MXU_DIM = 256
VMEM_LIMIT = 56 * 1024 * 1024

TOKEN_TILE = 512
ATTN_TILE = 256
PAGES_PER_STEP = 32


def _cparams(sem, vmem=VMEM_LIMIT):
    return pltpu.CompilerParams(dimension_semantics=sem, vmem_limit_bytes=vmem)


def _const_spec(shape):
    nd = len(shape)
    return pl.BlockSpec(shape, lambda *_: (0,) * nd, pipeline_mode=pl.Buffered(1))


def _smem_spec():
    return pl.BlockSpec(memory_space=pltpu.SMEM)


def _dot(a, b):
    return jnp.dot(a, b, preferred_element_type=F32)


def _dot_nt(a, b):
    return lax.dot_general(a, b, (((1,), (1,)), ((), ())), preferred_element_type=F32)


def _split3(x):
    hi = x.astype(BF16)
    r1 = x - hi.astype(F32)
    mid = r1.astype(BF16)
    lo = (r1 - mid.astype(F32)).astype(BF16)
    return hi, mid, lo


def _exact_dot(x, m01):
    hi, mid, lo = _split3(x)
    return (_dot(hi, m01) + _dot(mid, m01)) + _dot(lo, m01)


def _rmsnorm(x, g):
    ms = jnp.mean(x * x, axis=-1, keepdims=True)
    return x * lax.rsqrt(ms + EPS) * g


def _group_mean_sq(x, gmat):
    n = x.shape[1]
    w = gmat.shape[0]
    outs = []
    for c in range(n // w):
        blk = x[:, c * w:(c + 1) * w]
        sq = blk * blk
        hi = sq.astype(BF16)
        lo = (sq - hi.astype(F32)).astype(BF16)
        outs.append(_dot(hi, gmat) + _dot(lo, gmat))
    out = outs[0] if len(outs) == 1 else jnp.concatenate(outs, axis=1)
    return out * (1.0 / HEAD_DIM)


def _norm_keys(kv, gain_row, gmat):
    ms = _group_mean_sq(kv, gmat)
    lane = lax.broadcasted_iota(jnp.int32, kv.shape, 1)
    is_k = (lane % GROUP_COLS) < HEAD_DIM
    return jnp.where(is_k, kv * lax.rsqrt(ms + EPS) * gain_row, kv)


def _bucket_bias(dist, thr_ref, value_of_bucket):
    val = value_of_bucket(0)
    val = jnp.broadcast_to(val, dist.shape).astype(F32)
    for k in range(1, N_BUCKETS):
        val = jnp.where(dist >= thr_ref[k], value_of_bucket(k), val)
    return val


def _silu(x):
    return x * jax.nn.sigmoid(x)


def _ple(h, p, gain, wpg_ref, wpp_ref):
    r = _rmsnorm(h, gain).astype(BF16)
    g = jax.nn.sigmoid(_dot(r, wpg_ref[...]))
    return h + g * _dot(p.astype(BF16), wpp_ref[...])


def _conv_mix_tail(x, gb, u, um1, um2, cw_ref, wout_ref):
    conv = cw_ref[0:1, :] * um2 + cw_ref[1:2, :] * um1 + cw_ref[2:3, :] * u
    y = _dot((gb * conv).astype(BF16), wout_ref[...])
    return x + y


def _mix0_prompt_kernel(x_ref, g_ref, win_ref, cw_ref, wout_ref, h_ref, st_ref, carry_ref):
    j = pl.program_id(1)
    tm = x_ref.shape[0]

    @pl.when(j == 0)
    def _():
        carry_ref[...] = jnp.zeros_like(carry_ref)

    x = x_ref[...]
    a = _rmsnorm(x, g_ref[...]).astype(BF16)
    proj = _dot(a, win_ref[...])
    gb = proj[:, :D_MODEL]
    u = proj[:, D_MODEL:2 * D_MODEL] * proj[:, 2 * D_MODEL:]
    c0 = carry_ref[SUBLANES - 2:SUBLANES - 1, :]
    c1 = carry_ref[SUBLANES - 1:SUBLANES, :]
    row = lax.broadcasted_iota(jnp.int32, u.shape, 0)
    um1 = jnp.where(row == 0, c1, pltpu.roll(u, 1, 0))
    um2 = jnp.where(row == 0, c0, jnp.where(row == 1, c1, pltpu.roll(u, 2, 0)))
    h_ref[...] = _conv_mix_tail(x, gb, u, um1, um2, cw_ref, wout_ref)
    tail = u[tm - SUBLANES:, :]
    carry_ref[...] = tail
    st_ref[...] = tail


def _mix0_sample_kernel(x_ref, g_ref, win_ref, cw_ref, wout_ref, s1_ref, s2_ref, h_ref, u_ref,
                        *, seq):
    x = x_ref[...]
    a = _rmsnorm(x, g_ref[...]).astype(BF16)
    proj = _dot(a, win_ref[...])
    gb = proj[:, :D_MODEL]
    u = proj[:, D_MODEL:2 * D_MODEL] * proj[:, 2 * D_MODEL:]
    t = lax.broadcasted_iota(jnp.int32, u.shape, 0) % seq
    um1 = jnp.where(t >= 1, pltpu.roll(u, 1, 0), s1_ref[...])
    um2 = jnp.where(t >= 2, pltpu.roll(u, 2, 0), s2_ref[...])
    h_ref[...] = _conv_mix_tail(x, gb, u, um1, um2, cw_ref, wout_ref)
    u_ref[...] = u


def _mix0_prompt(x, gain, w_in, cw, w_out):
    b, s, d = x.shape
    tm = min(TOKEN_TILE, s)
    grid = (b, s // tm)
    return pl.pallas_call(
        _mix0_prompt_kernel,
        grid=grid,
        in_specs=[
            pl.BlockSpec((None, tm, d), lambda i, j: (i, j, 0)),
            _const_spec((1, d)),
            _const_spec(w_in.shape),
            _const_spec(cw.shape),
            _const_spec(w_out.shape),
        ],
        out_specs=[
            pl.BlockSpec((None, tm, d), lambda i, j: (i, j, 0)),
            pl.BlockSpec((None, SUBLANES, d), lambda i, j: (i, 0, 0)),
        ],
        out_shape=[
            jax.ShapeDtypeStruct((b, s, d), F32),
            jax.ShapeDtypeStruct((b, SUBLANES, d), F32),
        ],
        scratch_shapes=[pltpu.VMEM((SUBLANES, d), F32)],
        compiler_params=_cparams(("arbitrary", "arbitrary")),
        name="mix0_prompt",
    )(x, gain, w_in, cw, w_out)


def _mix0_sample(x2d, gain, w_in, cw, w_out, s1, s2, seq):
    n, d = x2d.shape
    return pl.pallas_call(
        functools.partial(_mix0_sample_kernel, seq=seq),
        out_shape=[jax.ShapeDtypeStruct((n, d), F32), jax.ShapeDtypeStruct((n, d), F32)],
        compiler_params=_cparams(None),
        name="mix0_sample",
    )(x2d, gain, w_in, cw, w_out, s1, s2)


def _ffn0_kernel(h_ref, p_ref, gf_ref, wg_ref, wu_ref, wd_ref, gp_ref, wpg_ref, wpp_ref, o_ref,
                 *, chunk):
    h = h_ref[...]
    f = _rmsnorm(h, gf_ref[...]).astype(BF16)
    d_ff = wg_ref.shape[1]
    acc = None
    for c in range(d_ff // chunk):
        sl = slice(c * chunk, (c + 1) * chunk)
        hid = _silu(_dot(f, wg_ref[:, sl])) * _dot(f, wu_ref[:, sl])
        part = _dot(hid.astype(BF16), wd_ref[sl, :])
        acc = part if acc is None else acc + part
    o_ref[...] = _ple(h + acc, p_ref[...], gp_ref[...], wpg_ref, wpp_ref)


def _ffn0(h2d, p2d, gf, wg, wu, wd, gp, wpg, wpp):
    n, d = h2d.shape
    tm = min(TOKEN_TILE, n)
    d_ff = wg.shape[1]
    chunk = d_ff // 2
    assert chunk % LANES == 0
    return pl.pallas_call(
        functools.partial(_ffn0_kernel, chunk=chunk),
        grid=(n // tm,),
        in_specs=[
            pl.BlockSpec((tm, d), lambda i: (i, 0)),
            pl.BlockSpec((tm, p2d.shape[1]), lambda i: (i, 0)),
            _const_spec((1, d)),
            _const_spec(wg.shape), _const_spec(wu.shape), _const_spec(wd.shape),
            _const_spec((1, d)),
            _const_spec(wpg.shape), _const_spec(wpp.shape),
        ],
        out_specs=pl.BlockSpec((tm, d), lambda i: (i, 0)),
        out_shape=jax.ShapeDtypeStruct((n, d), F32),
        compiler_params=_cparams(("arbitrary",)),
        name="ffn0_ple",
    )(h2d, p2d, gf, wg, wu, wd, gp, wpg, wpp)


def _norm_keys_t(kvt, gain_col):
    parts = []
    for gi in range(KV_COLS // HEAD_DIM):
        x = kvt[gi * HEAD_DIM:(gi + 1) * HEAD_DIM, :]
        if gi % 2 == 0:
            ms = jnp.mean(x * x, axis=0, keepdims=True)
            x = x * lax.rsqrt(ms + EPS) * gain_col
        parts.append(x)
    return jnp.concatenate(parts, axis=0)


def _nsa_proj_kernel(h_ref, g_ref, wq_ref, wkc_ref, wkvt_ref, wv_ref, wgt_ref, gm_ref, qg_ref,
                     kgs_ref, kgw_ref, q_ref, kvc_ref, kvct_ref, kvst_ref, kvwt_ref, ksn_ref,
                     kwn_ref, vr_ref, gt_ref):
    a = _rmsnorm(h_ref[...], g_ref[...]).astype(BF16)
    vr = _dot(a, wv_ref[...])
    lane = lax.broadcasted_iota(jnp.int32, vr.shape, 1)
    vr_ref[...] = jnp.where(lane % LANES < HEAD_DIM, vr, 1.0).astype(vr_ref.dtype)
    q = _dot(a, wq_ref[...])
    ms = _group_mean_sq(q, gm_ref[...])
    q_ref[...] = ((q * lax.rsqrt(ms + EPS) * qg_ref[...]) * (HEAD_DIM ** -0.5)).astype(q_ref.dtype)
    kvc_ref[...] = _dot(a, wkc_ref[...])
    kvt = _dot_nt(wkvt_ref[...], a)
    kvst = kvt[KV_COLS:2 * KV_COLS]
    kvwt = kvt[2 * KV_COLS:]
    kvct_ref[...] = kvt[:KV_COLS]
    kvst_ref[...] = kvst
    kvwt_ref[...] = kvwt
    ksn_ref[...] = _norm_keys_t(kvst, kgs_ref[...]).astype(ksn_ref.dtype)
    kwn_ref[...] = _norm_keys_t(kvwt, kgw_ref[...]).astype(kwn_ref.dtype)
    gt_ref[...] = jax.nn.sigmoid(_dot(a, wgt_ref[...]))


def _nsa_proj(h2d, n_seq, gain, wq, wkc, wkvt, wv, wgt, gmat, qg, kgs_col, kgw_col):
    n, d = h2d.shape
    s = n // n_seq
    tm = min(TOKEN_TILE, s)
    tps = s // tm
    row = lambda w: pl.BlockSpec((tm, w), lambda i: (i, 0))
    col = pl.BlockSpec((None, KV_COLS, tm), lambda i: (i // tps, 0, i % tps))
    tshape = lambda dt: jax.ShapeDtypeStruct((n_seq, KV_COLS, s), dt)
    n_gate_cols = wgt.shape[1]
    return pl.pallas_call(
        _nsa_proj_kernel,
        grid=(n // tm,),
        in_specs=[
            row(d), _const_spec((1, d)),
            _const_spec(wq.shape), _const_spec(wkc.shape), _const_spec(wkvt.shape),
            _const_spec(wv.shape), _const_spec(wgt.shape), _const_spec(gmat.shape),
            _const_spec(qg.shape), _const_spec(kgs_col.shape), _const_spec(kgw_col.shape),
        ],
        out_specs=[row(d), row(KV_COLS), col, col, col, col, col, row(wv.shape[1]),
                   row(n_gate_cols)],
        out_shape=[
            jax.ShapeDtypeStruct((n, d), BF16),
            jax.ShapeDtypeStruct((n, KV_COLS), F32),
            tshape(F32), tshape(F32), tshape(F32), tshape(BF16), tshape(BF16),
            jax.ShapeDtypeStruct((n, wv.shape[1]), BF16),
            jax.ShapeDtypeStruct((n, n_gate_cols), F32),
        ],
        compiler_params=_cparams(("arbitrary",)),
        name="nsa_proj",
    )(h2d, gain, wq, wkc, wkvt, wv, wgt, gmat, qg, kgs_col, kgw_col)


SLABS = KV_COLS // LANES


def _compress_rows(read_slab, n_blocks, w_ref, gmat, kg_row):
    half = KV_COLS // 2
    per_half = SLABS // 2
    acc = [jnp.zeros((n_blocks, half), F32) for _ in range(2)]
    for l in range(CMP_BLOCK):
        for hf in range(2):
            xl = jnp.concatenate([read_slab(l, hf * per_half + c) for c in range(per_half)], axis=1)
            acc[hf] = acc[hf] + _dot(xl.astype(BF16), w_ref[l, hf])
    kv = jnp.concatenate(acc, axis=1)
    return _norm_keys(kv, kg_row, gmat).astype(BF16)


def _compress_kernel(x_ref, w_ref, gm_ref, kg_ref, o_ref):
    nb = o_ref.shape[0]
    read = lambda l, g: x_ref[pl.ds(l * SLABS + g, nb, stride=CMP_BLOCK * SLABS), :]
    o_ref[...] = _compress_rows(read, nb, w_ref, gm_ref[...], kg_ref[...])


def _compress_prompt(rows2d, w2, gmat, kg):
    n = rows2d.shape[0]
    tr = min(4096, n)
    nb = tr // CMP_BLOCK
    return pl.pallas_call(
        _compress_kernel,
        grid=(n // tr,),
        in_specs=[pl.BlockSpec((tr * SLABS, LANES), lambda i: (i, 0)),
                  _const_spec(w2.shape), _const_spec(gmat.shape), _const_spec(kg.shape)],
        out_specs=pl.BlockSpec((nb, KV_COLS), lambda i: (i, 0)),
        out_shape=jax.ShapeDtypeStruct((n // CMP_BLOCK, KV_COLS), BF16),
        compiler_params=_cparams(("arbitrary",)),
        name="compress_prompt",
    )(rows2d.reshape(n * SLABS, LANES), w2, gmat, kg)


def _compress_paged_kernel(pt_ref, pool_ref, w_ref, gm_ref, kg_ref, o_ref, buf_ref, st_ref, sem_ref,
                           *, pages_per_step, page):
    i = pl.program_id(0)
    n = pl.num_programs(0)

    def page_copy(step, p, slot):
        phys = pt_ref[step * pages_per_step + p]
        return pltpu.make_async_copy(pool_ref.at[phys], buf_ref.at[slot, p], sem_ref.at[slot])

    def start(step, slot):
        def issue(p, c):
            page_copy(step, p, slot).start()
            return c
        lax.fori_loop(0, pages_per_step, issue, 0)

    @pl.when(i == 0)
    def _():
        start(0, 0)

    @pl.when(i + 1 < n)
    def _():
        start(i + 1, (i + 1) % 2)

    slot = i % 2

    def wait(p, c):
        page_copy(i, p, slot).wait()
        return c
    lax.fori_loop(0, pages_per_step, wait, 0)

    blocks_per_page = page // CMP_BLOCK
    group = 2 * blocks_per_page
    assert group == SUBLANES
    ri = lax.broadcasted_iota(jnp.int32, (2 * page, 2 * page), 0)
    ci = lax.broadcasted_iota(jnp.int32, (2 * page, 2 * page), 1)
    src_lane = ((ri % group) // blocks_per_page) * page + (ri % blocks_per_page) * CMP_BLOCK + ri // group
    perm = (ci == src_lane).astype(BF16)

    def to_rows(pp, c):
        for g in range(N_KV_HEADS):
            pair = jnp.concatenate([buf_ref[slot, 2 * pp, g], buf_ref[slot, 2 * pp + 1, g]],
                                   axis=1).astype(BF16)
            rows = _dot_nt(perm, pair)
            for l in range(CMP_BLOCK):
                st_ref[g, l, pl.ds(pl.multiple_of(pp * group, group), group), :] = (
                    rows[l * group:(l + 1) * group])
        return c
    lax.fori_loop(0, pages_per_step // 2, to_rows, 0)

    nb = pages_per_step * page // CMP_BLOCK
    read = lambda l, g: st_ref[g, l]
    o_ref[...] = _compress_rows(read, nb, w_ref, gm_ref[...], kg_ref[...])


def _compress_paged(page_table_flat, pool_t, w2, gmat, kg, pages_per_step):
    n_pages = page_table_flat.shape[0]
    page = pool_t.shape[3]
    assert n_pages % pages_per_step == 0 and page == LANES and pool_t.shape[2] == GROUP_COLS
    steps = n_pages // pages_per_step
    nb = pages_per_step * page // CMP_BLOCK
    grid_spec = pltpu.PrefetchScalarGridSpec(
        num_scalar_prefetch=1,
        grid=(steps,),
        in_specs=[pl.BlockSpec(memory_space=pl.ANY),
                  pl.BlockSpec(w2.shape, lambda i, pt: (0, 0, 0, 0), pipeline_mode=pl.Buffered(1)),
                  pl.BlockSpec(gmat.shape, lambda i, pt: (0, 0)),
                  pl.BlockSpec(kg.shape, lambda i, pt: (0, 0))],
        out_specs=pl.BlockSpec((nb, KV_COLS), lambda i, pt: (i, 0)),
        scratch_shapes=[pltpu.VMEM((2, pages_per_step, N_KV_HEADS, GROUP_COLS, page), F32),
                        pltpu.VMEM((N_KV_HEADS, CMP_BLOCK, nb, GROUP_COLS), F32),
                        pltpu.SemaphoreType.DMA((2,))],
    )
    return pl.pallas_call(
        functools.partial(_compress_paged_kernel, pages_per_step=pages_per_step, page=page),
        grid_spec=grid_spec,
        out_shape=jax.ShapeDtypeStruct((steps * nb, KV_COLS), BF16),
        compiler_params=_cparams(("arbitrary",)),
        name="compress_paged",
    )(page_table_flat, pool_t, w2, gmat, kg)


FLASH_ROWS = 128


def _flash_tile(qh_ref, kt_ref, v_ref, kt, tk, state, terms):
    m_ref, acc_ref = state
    start = pl.multiple_of(kt * tk, tk)
    k = kt_ref[:, pl.ds(start, tk)]
    v = v_ref[pl.ds(start, tk), :]
    tq = qh_ref.shape[1]
    blocks = [(r, pl.ds(rb * FLASH_ROWS, FLASH_ROWS))
              for r in range(HEADS_PER_GROUP) for rb in range(tq // FLASH_ROWS)]
    old = [(m_ref[r, rows, :], acc_ref[r, rows, :]) for r, rows in blocks]
    new = []
    for (r, rows), (m_old, acc_old) in zip(blocks, old):
        s = _dot(qh_ref[r, rows, :], k)
        for term in terms(r, rows):
            s = s + term
        m_new = jnp.maximum(m_old, jnp.max(s, axis=-1, keepdims=True))
        e = jnp.exp(s - jnp.concatenate([m_new] * (tk // LANES), axis=1))
        new.append((m_new, jnp.exp(m_old - m_new) * acc_old + _dot(e.astype(BF16), v)))
    for (r, rows), (m_new, acc_new) in zip(blocks, new):
        m_ref[r, rows, :] = m_new
        acc_ref[r, rows, :] = acc_new


def _nsa_prompt_kernel(tbl_ref, thr_ref, q_ref, kc_ref, ks_ref, vs_ref, kw_ref, vw_ref, gt_ref, o_ref,
                       bias_ref, edge_ref, cbias_ref, amask_ref, qh_ref, m_ref, acc_ref, *, n_top):
    g = pl.program_id(0)
    b = pl.program_id(1)
    qi = pl.program_id(2)
    tq = q_ref.shape[0]
    tk = tq
    hpg = HEADS_PER_GROUP
    nc = kc_ref.shape[0]
    ns = nc // CMP_PER_SEL
    blocks_per_tile = tk // SEL_BLOCK
    far = N_BUCKETS - 1

    qpos = qi * tq + lax.broadcasted_iota(jnp.int32, (tq, 1), 0)
    n_io = lax.broadcasted_iota(jnp.int32, (tq, nc), 1)
    relc = qpos - ((n_io + 1) * CMP_BLOCK - 1)

    n_back = WINDOW // tk

    @pl.when((b == 0) & (qi == 0))
    def _():
        diag = (lax.broadcasted_iota(jnp.int32, (tq, tk), 0)
                - lax.broadcasted_iota(jnp.int32, (tq, tk), 1))
        for r in range(hpg):
            h = g * hpg + r
            shifted = lambda k: tbl_ref[k, h] - tbl_ref[far, h]
            bias_ref[r, 0] = jnp.where(diag >= 0, _bucket_bias(diag, thr_ref, shifted), NEG)
            bias_ref[r, 1] = _bucket_bias(diag + tk, thr_ref, shifted)
        edge_ref[...] = jnp.where(diag + n_back * tk < WINDOW, 0.0, NEG)

    @pl.when(b == 0)
    def _():
        for r in range(hpg):
            h = g * hpg + r
            cbias_ref[r, qi] = _bucket_bias(relc, thr_ref, lambda k: tbl_ref[k, h])

    q = q_ref[...]
    for r in range(hpg):
        qh_ref[r] = q[:, r * HEAD_DIM:(r + 1) * HEAD_DIM]

    kc = kc_ref[...]
    kck = kc[:, :HEAD_DIM]
    kcv = kc[:, HEAD_DIM:]
    maskc = relc >= 0
    imp = None
    oc = []
    for r in range(hpg):
        s = _dot_nt(q[:, r * HEAD_DIM:(r + 1) * HEAD_DIM], kck) + cbias_ref[r, qi]
        sm = jnp.where(maskc, s, NEG)
        e = jnp.exp(sm - jnp.max(sm, axis=-1, keepdims=True))
        p = jnp.where(maskc, e / jnp.sum(e, axis=-1, keepdims=True), 0.0)
        imp = p if imp is None else imp + p
        oc.append(_dot(p.astype(BF16), kcv))

    pair_t = (lax.broadcasted_iota(jnp.int32, (ns, nc), 0)
              == lax.broadcasted_iota(jnp.int32, (ns, nc), 1) // CMP_PER_SEL).astype(BF16)
    hi, mid, lo = _split3(imp)
    imp_sel = (_dot_nt(pair_t, hi) + _dot_nt(pair_t, mid)) + _dot_nt(pair_t, lo)
    blk = lax.broadcasted_iota(jnp.int32, (ns, tq), 0)
    cur = (qi * tq + lax.broadcasted_iota(jnp.int32, (1, tq), 1)) // SEL_BLOCK
    forced = (blk == 0) | (blk == cur) | (blk == cur - 1)
    score = jnp.where(forced, FORCED_SCORE, jnp.where(blk <= cur, imp_sel, -1.0))
    rank = jnp.zeros((ns, tq), F32)
    for i in range(ns):
        si = score[i:i + 1, :]
        beats = (si > score) | ((si == score) & (blk > i))
        rank = rank + jnp.where(beats, 1.0, 0.0)
    sel_t = jnp.where(rank < n_top, 1.0, 0.0).astype(BF16)
    eye = (lax.broadcasted_iota(jnp.int32, (tq, tq), 0)
           == lax.broadcasted_iota(jnp.int32, (tq, tq), 1)).astype(BF16)
    sel = _dot_nt(eye, sel_t).astype(BF16)

    state = (m_ref, acc_ref)

    def init():
        m_ref[...] = jnp.full(m_ref.shape, NEG, F32)
        acc_ref[...] = jnp.zeros(acc_ref.shape, F32)

    def finish():
        out = []
        for r in range(hpg):
            acc = acc_ref[r]
            out.append((acc / pltpu.roll(acc, HEAD_DIM, 1))[:, :HEAD_DIM])
        return out

    def sel_tile(kt, tiles_back):
        expand = (lax.broadcasted_iota(jnp.int32, (ns, tk), 0)
                  == kt * blocks_per_tile + lax.broadcasted_iota(jnp.int32, (ns, tk), 1) // SEL_BLOCK)
        amask_ref[...] = jnp.where(_dot(sel, expand.astype(BF16)) > 0.5, 0.0, NEG)
        if tiles_back is None:
            terms = lambda r, rows: (amask_ref[rows, :],)
        else:
            terms = lambda r, rows: (bias_ref[r, tiles_back, rows, :], amask_ref[rows, :])
        _flash_tile(qh_ref,ks_ref, vs_ref, kt, tk, state, terms)

    init()

    def far_tile(kt, c):
        sel_tile(kt, None)
        return c
    lax.fori_loop(0, jnp.maximum(qi - 1, 0), far_tile, 0)

    @pl.when(qi >= 1)
    def _():
        sel_tile(qi - 1, 1)
    sel_tile(qi, 0)
    o_sel = finish()

    init()

    @pl.when(qi >= n_back)
    def _():
        _flash_tile(qh_ref,kw_ref, vw_ref, qi - n_back, tk, state,
                    lambda r, rows: (edge_ref[rows, :],))

    @pl.when(qi >= 1)
    def _():
        _flash_tile(qh_ref,kw_ref, vw_ref, qi - 1, tk, state,
                    lambda r, rows: (bias_ref[r, 1, rows, :],))
    _flash_tile(qh_ref,kw_ref, vw_ref, qi, tk, state, lambda r, rows: (bias_ref[r, 0, rows, :],))
    o_win = finish()

    width = hpg * HEAD_DIM
    idx = lax.broadcasted_iota(jnp.int32, (LANES, N_BRANCHES * width), 0)
    col = lax.broadcasted_iota(jnp.int32, (LANES, N_BRANCHES * width), 1)
    spread = (idx == (col // width) * hpg + (col % width) // HEAD_DIM).astype(BF16)
    gates = _exact_dot(gt_ref[...], spread)
    out = None
    for c, branch in enumerate((oc, o_sel, o_win)):
        term = gates[:, c * width:(c + 1) * width] * jnp.concatenate(branch, axis=1)
        out = term if out is None else out + term
    o_ref[...] = out.astype(o_ref.dtype)


def _nsa_prompt_attn(tbl, thr, q, kcn, ksn_t, kwn_t, v_rows, gates, b, s):
    tq = min(ATTN_TILE, s)
    assert WINDOW == 2 * tq and tq + 1 >= MAX_DISTANCE and tq % FLASH_ROWS == 0
    nq = s // tq
    nc = s // CMP_BLOCK
    ns = s // SEL_BLOCK
    n_top = min(N_SELECTED, ns)
    hpg = HEADS_PER_GROUP
    k_spec = pl.BlockSpec((None, HEAD_DIM, s), lambda g, i, j: (i, 2 * g, 0))
    v_spec = lambda branch: pl.BlockSpec((s, LANES), lambda g, i, j: (i, branch * N_KV_HEADS + g))
    return pl.pallas_call(
        functools.partial(_nsa_prompt_kernel, n_top=n_top),
        grid=(N_KV_HEADS, b, nq),
        in_specs=[
            _smem_spec(), _smem_spec(),
            pl.BlockSpec((tq, Q_GROUP_COLS), lambda g, i, j: (i * nq + j, g)),
            pl.BlockSpec((nc, GROUP_COLS), lambda g, i, j: (i, g)),
            k_spec, v_spec(0), k_spec, v_spec(1),
            pl.BlockSpec((tq, LANES), lambda g, i, j: (i * nq + j, g)),
        ],
        out_specs=pl.BlockSpec((tq, Q_GROUP_COLS), lambda g, i, j: (i * nq + j, g)),
        out_shape=jax.ShapeDtypeStruct((b * s, D_MODEL), BF16),
        scratch_shapes=[pltpu.VMEM((hpg, 2, tq, tq), F32),
                        pltpu.VMEM((tq, tq), F32),
                        pltpu.VMEM((hpg, nq, tq, nc), F32),
                        pltpu.VMEM((tq, tq), F32),
                        pltpu.VMEM((hpg, tq, HEAD_DIM), BF16),
                        pltpu.VMEM((hpg, tq, LANES), F32),
                        pltpu.VMEM((hpg, tq, LANES), F32)],
        compiler_params=_cparams(("arbitrary", "arbitrary", "arbitrary")),
        name="nsa_prompt_attn",
    )(tbl, thr, q, kcn, ksn_t, v_rows, kwn_t, v_rows, gates)


def _bucket_onehot(dist_row, thr_ref):
    n = dist_row.shape[1]
    bucket = jnp.zeros(dist_row.shape, jnp.int32)
    for k in range(1, N_BUCKETS):
        bucket = bucket + jnp.where(dist_row >= thr_ref[k], 1, 0)
    return (lax.broadcasted_iota(jnp.int32, (N_BUCKETS, n), 0) == bucket).astype(BF16)


def _sample_cmp_kernel(thr_ref, q_ref, kc_ref, tbl_ref, oc_ref, idx_ref, *, past_len, seq, n_top):
    nc = kc_ref.shape[0]
    ns = past_len // SEL_BLOCK
    hpg = HEADS_PER_GROUP
    rows = hpg * seq
    kc = kc_ref[...]
    pair = (lax.broadcasted_iota(jnp.int32, (nc, ns), 0) // CMP_PER_SEL
            == lax.broadcasted_iota(jnp.int32, (nc, ns), 1)).astype(BF16)
    eye = (lax.broadcasted_iota(jnp.int32, (ns, ns), 0)
           == lax.broadcasted_iota(jnp.int32, (ns, ns), 1))
    eye_bf = eye.astype(BF16)
    ii = lax.broadcasted_iota(jnp.int32, (ns, ns), 0)
    jj = lax.broadcasted_iota(jnp.int32, (ns, ns), 1)
    t_row = lax.broadcasted_iota(jnp.int32, (rows, nc), 0) % seq
    n_io = lax.broadcasted_iota(jnp.int32, (rows, nc), 1)
    rel = past_len + t_row - ((n_io + 1) * CMP_BLOCK - 1)
    mask = rel >= 0
    blk = lax.broadcasted_iota(jnp.int32, (SUBLANES, ns), 1)
    forced = (blk == 0) | (blk == ns - 1)
    rank_lane = lax.broadcasted_iota(jnp.int32, (ns, LANES), 1).astype(F32)
    blk_col = lax.broadcasted_iota(jnp.int32, (ns, LANES), 0).astype(F32)
    for g in range(N_KV_HEADS):
        kck = kc[:, g * GROUP_COLS:g * GROUP_COLS + HEAD_DIM]
        kcv = kc[:, g * GROUP_COLS + HEAD_DIM:(g + 1) * GROUP_COLS]
        qg = q_ref[g].astype(BF16)
        tbl_g = tbl_ref[g]
        s = _dot_nt(qg, kck)
        s = s + _bucket_bias(rel, thr_ref, lambda k: tbl_g[:, k:k + 1])
        sm = jnp.where(mask, s, NEG)
        e = jnp.exp(sm - jnp.max(sm, axis=-1, keepdims=True))
        p = jnp.where(mask, e / jnp.sum(e, axis=-1, keepdims=True), 0.0)
        oc_ref[g] = _dot(p.astype(BF16), kcv)
        imp = p[0:seq]
        for r in range(1, hpg):
            imp = imp + p[r * seq:(r + 1) * seq]
        imp = jnp.concatenate([imp, jnp.zeros((SUBLANES - seq, nc), F32)], axis=0)
        score = jnp.where(forced, FORCED_SCORE, _exact_dot(imp, pair))
        hi, mid, lo = _split3(score)
        score_t = (_dot_nt(eye_bf, hi) + _dot_nt(eye_bf, mid)) + _dot_nt(eye_bf, lo)
        for t in range(seq):
            s_row = score[t:t + 1, :]
            s_col = score_t[:, t:t + 1]
            beats = (ii != jj) & ((s_row > s_col) | ((s_row == s_col) & (jj < ii)))
            rank_col = jnp.sum(jnp.where(beats, 1.0, 0.0), axis=1, keepdims=True)
            onehot = rank_col == rank_lane
            idx_row = jnp.sum(jnp.where(onehot, blk_col, 0.0), axis=0, keepdims=True)
            idx_ref[g * seq + t:g * seq + t + 1, :] = idx_row.astype(jnp.int32)


def _sample_cmp(thr, q_rt, kcn, tbl_rt, b, past_len, seq):
    nc = kcn.shape[0] // b
    ns = past_len // SEL_BLOCK
    n_top = min(N_SELECTED - 1, ns)
    rows = HEADS_PER_GROUP * seq
    return pl.pallas_call(
        functools.partial(_sample_cmp_kernel, past_len=past_len, seq=seq, n_top=n_top),
        grid=(b,),
        in_specs=[
            _smem_spec(),
            pl.BlockSpec((None, N_KV_HEADS, rows, HEAD_DIM), lambda i: (i, 0, 0, 0)),
            pl.BlockSpec((nc, KV_COLS), lambda i: (i, 0)),
            _const_spec(tbl_rt.shape),
        ],
        out_specs=[
            pl.BlockSpec((None, N_KV_HEADS, rows, HEAD_DIM), lambda i: (i, 0, 0, 0)),
            pl.BlockSpec((None, N_KV_HEADS * seq, LANES), lambda i: (i, 0, 0)),
        ],
        out_shape=[
            jax.ShapeDtypeStruct((b, N_KV_HEADS, rows, HEAD_DIM), F32),
            jax.ShapeDtypeStruct((b, N_KV_HEADS * seq, LANES), jnp.int32),
        ],
        compiler_params=_cparams(("arbitrary",)),
        name="sample_cmp",
    )(thr, q_rt, kcn, tbl_rt)


def _norm_k_cols(kt, gain_col):
    ms = jnp.mean(kt * kt, axis=0, keepdims=True)
    return (kt * lax.rsqrt(ms + EPS) * gain_col).astype(BF16)


def _sample_sel_win_kernel(idx_ref, pt_ref, thr_ref, pool_ref, q_ref, ksnew_ref, kwnew_ref,
                           win_ref, tbl_ref, kgs_ref, kgw_ref, os_ref, ow_ref,
                           buf_ref, sem_ref, *, past_len, seq, n_top, page):
    i = pl.program_id(0)
    n_pages = past_len // page
    blocks_per_page = page // SEL_BLOCK
    n_keys = n_top * page
    stride = n_top + 1

    def block_of(g, t, j):
        return idx_ref[(i * N_KV_HEADS * seq + g * seq + t) * stride + j]

    def tile_copy(g, t, j):
        phys = pt_ref[i * n_pages + block_of(g, t, j) // blocks_per_page]
        return pltpu.make_async_copy(
            pool_ref.at[phys, g],
            buf_ref.at[g * seq + t, :, pl.ds(pl.multiple_of(j * page, page), page)],
            sem_ref.at[0])

    def for_all_tiles(fn):
        for g in range(N_KV_HEADS):
            for t in range(seq):
                def per_tile(j, c):
                    fn(tile_copy(g, t, j))
                    return c
                lax.fori_loop(0, n_top, per_tile, 0)

    for_all_tiles(lambda cp: cp.start())

    wb = win_ref.shape[2]
    wk = [_norm_k_cols(win_ref[g, 0:HEAD_DIM, :], kgw_ref[...]) for g in range(N_KV_HEADS)]
    wv = [win_ref[g, HEAD_DIM:GROUP_COLS, :].astype(BF16) for g in range(N_KV_HEADS)]
    j_io = lax.broadcasted_iota(jnp.int32, (1, wb), 1)
    tnew = lax.broadcasted_iota(jnp.int32, (1, SUBLANES), 1)

    def attend(q, tbl_g, k_past, v_past, buckets, mask, knew, bucketsn, maskn):
        s1 = jnp.where(mask, _dot(q, k_past) + _exact_dot(tbl_g, buckets), NEG)
        s2 = jnp.where(maskn, _dot_nt(q, knew[:, :HEAD_DIM].astype(BF16))
                       + _exact_dot(tbl_g, bucketsn), NEG)
        m = jnp.maximum(jnp.max(s1, axis=-1, keepdims=True), jnp.max(s2, axis=-1, keepdims=True))
        e1 = jnp.where(mask, jnp.exp(s1 - m), 0.0)
        e2 = jnp.where(maskn, jnp.exp(s2 - m), 0.0)
        l = jnp.sum(e1, axis=-1, keepdims=True) + jnp.sum(e2, axis=-1, keepdims=True)
        o = _dot_nt(e1.astype(BF16), v_past) + _dot(e2.astype(BF16),
                                                    knew[:, HEAD_DIM:].astype(BF16))
        return o / l

    def window(t, c):
        dist = wb + t - j_io
        maskw = (dist >= 0) & (dist < WINDOW)
        distn = t - tnew
        maskn = (distn >= 0) & (distn < WINDOW)
        buckets = _bucket_onehot(dist, thr_ref)
        bucketsn = _bucket_onehot(distn, thr_ref)
        for g in range(N_KV_HEADS):
            ow_ref[g, t] = attend(q_ref[g, t].astype(BF16), tbl_ref[g], wk[g], wv[g], buckets,
                                  maskw, kwnew_ref[g], bucketsn, maskn)
        return c
    lax.fori_loop(0, seq, window, 0)

    for_all_tiles(lambda cp: cp.wait())

    lane = lax.broadcasted_iota(jnp.int32, (1, n_keys), 1)
    tile_of_lane = lane // page
    row_in_page = lane % page

    def selected(t, c):
        distn = t - tnew
        maskn = distn >= 0
        bucketsn = _bucket_onehot(distn, thr_ref)
        for g in range(N_KV_HEADS):
            kvt = buf_ref[g * seq + t]
            page_base = jnp.zeros((1, n_keys), jnp.int32)
            half = jnp.zeros((1, n_keys), jnp.int32)
            for j in range(n_top):
                blk = block_of(g, t, j)
                page_base = jnp.where(tile_of_lane == j, (blk // blocks_per_page) * page, page_base)
                half = jnp.where(tile_of_lane == j, blk % blocks_per_page, half)
            mask = (row_in_page // SEL_BLOCK) == half
            dist = past_len + t - (page_base + row_in_page)
            os_ref[g, t] = attend(q_ref[g, t].astype(BF16), tbl_ref[g],
                                  _norm_k_cols(kvt[0:HEAD_DIM], kgs_ref[...]),
                                  kvt[HEAD_DIM:GROUP_COLS].astype(BF16),
                                  _bucket_onehot(dist, thr_ref), mask,
                                  ksnew_ref[g], bucketsn, maskn)
        return c
    lax.fori_loop(0, seq, selected, 0)


def _sample_sel_win(idx_flat, pt_flat, thr, pool_t, q_tr, ksnew, kwnew, win_t, tbl_r8,
                    kgs_col, kgw_col, b, past_len, seq):
    ns = past_len // SEL_BLOCK
    n_top = min(N_SELECTED - 1, ns)
    page = pool_t.shape[3]
    wb = win_t.shape[3]
    o_shape = jax.ShapeDtypeStruct((b, N_KV_HEADS, seq, SUBLANES, HEAD_DIM), F32)
    o_spec = pl.BlockSpec((None, N_KV_HEADS, seq, SUBLANES, HEAD_DIM),
                          lambda i, *_: (i, 0, 0, 0, 0))
    new_spec = pl.BlockSpec((None, N_KV_HEADS, SUBLANES, GROUP_COLS), lambda i, *_: (i, 0, 0, 0))
    grid_spec = pltpu.PrefetchScalarGridSpec(
        num_scalar_prefetch=3,
        grid=(b,),
        in_specs=[
            pl.BlockSpec(memory_space=pl.ANY),
            pl.BlockSpec((None, N_KV_HEADS, seq, SUBLANES, HEAD_DIM), lambda i, *_: (i, 0, 0, 0, 0)),
            new_spec, new_spec,
            pl.BlockSpec((None, N_KV_HEADS, GROUP_COLS, wb), lambda i, *_: (i, 0, 0, 0)),
            pl.BlockSpec(tbl_r8.shape, lambda i, *_: (0, 0, 0)),
            pl.BlockSpec(kgs_col.shape, lambda i, *_: (0, 0)),
            pl.BlockSpec(kgw_col.shape, lambda i, *_: (0, 0)),
        ],
        out_specs=[o_spec, o_spec],
        scratch_shapes=[pltpu.VMEM((N_KV_HEADS * seq, GROUP_COLS, n_top * page), F32),
                        pltpu.SemaphoreType.DMA((1,))],
    )
    return pl.pallas_call(
        functools.partial(_sample_sel_win_kernel, past_len=past_len, seq=seq, n_top=n_top,
                          page=page),
        grid_spec=grid_spec,
        out_shape=[o_shape, o_shape],
        compiler_params=_cparams(("arbitrary",)),
        name="sample_sel_win",
    )(idx_flat, pt_flat, thr, pool_t, q_tr, ksnew, kwnew, win_t, tbl_r8, kgs_col, kgw_col)


def _route(f, rhi_ref, rlo_ref):
    f_hi = f.astype(BF16)
    f_lo = (f - f_hi.astype(F32)).astype(BF16)
    logits = (_dot(f_hi, rhi_ref[...]) + _dot(f_lo, rhi_ref[...])) + _dot(f_hi, rlo_ref[...])
    lane = lax.broadcasted_iota(jnp.int32, logits.shape, 1).astype(F32)
    logits = jnp.where(lane < N_EXPERTS, logits, -jnp.inf)
    m1 = jnp.max(logits, axis=-1, keepdims=True)
    i1 = jnp.min(jnp.where(logits == m1, lane, float(LANES)), axis=-1, keepdims=True)
    rest = jnp.where(lane == i1, -jnp.inf, logits)
    m2 = jnp.max(rest, axis=-1, keepdims=True)
    i2 = jnp.min(jnp.where(rest == m2, lane, float(LANES)), axis=-1, keepdims=True)
    e2 = jnp.exp(m2 - m1)
    denom = 1.0 + e2
    return jnp.where(lane == i1, 1.0 / denom, 0.0) + jnp.where(lane == i2, e2 / denom, 0.0)


def _attn_out_kernel(h_ref, o_ref, wout_ref, gf_ref, rhi_ref, rlo_ref, h2_ref, f_ref, gate_ref):
    h2 = h_ref[...] + _dot(o_ref[...], wout_ref[...])
    h2_ref[...] = h2
    f = _rmsnorm(h2, gf_ref[...])
    f_ref[...] = f.astype(f_ref.dtype)
    gate_ref[...] = _route(f, rhi_ref, rlo_ref)


def _attn_out_merge_kernel(h_ref, oc_ref, os_ref, ow_ref, gt_ref, wout_ref, gf_ref, rhi_ref,
                           rlo_ref, h2_ref, f_ref, gate_ref):
    gates = gt_ref[...]
    n_gate = gates.shape[1]
    hpg = HEADS_PER_GROUP
    idx = lax.broadcasted_iota(jnp.int32, (n_gate, D_MODEL), 0)
    head = lax.broadcasted_iota(jnp.int32, (n_gate, D_MODEL), 1) // HEAD_DIM
    o = None
    for c, ref in enumerate((oc_ref, os_ref, ow_ref)):
        lane_of_gate = (head // hpg) * LANES + c * hpg + head % hpg
        gate = _exact_dot(gates, (idx == lane_of_gate).astype(BF16))
        term = gate * ref[...]
        o = term if o is None else o + term
    h2 = h_ref[...] + _dot(o.astype(BF16), wout_ref[...])
    h2_ref[...] = h2
    f = _rmsnorm(h2, gf_ref[...])
    f_ref[...] = f.astype(f_ref.dtype)
    gate_ref[...] = _route(f, rhi_ref, rlo_ref)


def _attn_out(h2d, o_list, gates, wout, gf, rhi, rlo):
    n, d = h2d.shape
    tm = min(TOKEN_TILE, n)
    row = lambda w: pl.BlockSpec((tm, w), lambda i: (i, 0))
    merged = len(o_list) == 1
    body = _attn_out_kernel if merged else _attn_out_merge_kernel
    in_specs = [row(d)] + [row(d)] * len(o_list) + ([] if merged else [row(gates.shape[1])])
    in_specs += [_const_spec(wout.shape), _const_spec((1, d)), _const_spec(rhi.shape),
                 _const_spec(rlo.shape)]
    args = [h2d] + list(o_list) + ([] if merged else [gates]) + [wout, gf, rhi, rlo]
    return pl.pallas_call(
        body,
        grid=(n // tm,),
        in_specs=in_specs,
        out_specs=[row(d), row(d), row(LANES)],
        out_shape=[jax.ShapeDtypeStruct((n, d), F32), jax.ShapeDtypeStruct((n, d), BF16),
                   jax.ShapeDtypeStruct((n, LANES), F32)],
        compiler_params=_cparams(("arbitrary",)),
        name="attn_out_route",
    )(*args)


def _moe_kernel(f_ref, h2_ref, gate_ref, p_ref, wg_ref, wu_ref, wd_ref, gp_ref, wpg_ref, wpp_ref,
                o_ref, acc_ref):
    e = pl.program_id(1)

    @pl.when(e == 0)
    def _():
        acc_ref[...] = jnp.zeros_like(acc_ref)

    f = f_ref[...]
    gate = gate_ref[...]
    lane = lax.broadcasted_iota(jnp.int32, gate.shape, 1)
    ge = jnp.sum(jnp.where(lane == e, gate, 0.0), axis=-1, keepdims=True)
    hid = _silu(_dot(f, wg_ref[...])) * _dot(f, wu_ref[...])
    acc_ref[...] += ge * _dot(hid.astype(BF16), wd_ref[...])

    @pl.when(e == pl.num_programs(1) - 1)
    def _():
        o_ref[...] = _ple(h2_ref[...] + acc_ref[...], p_ref[...], gp_ref[...], wpg_ref, wpp_ref)


def _moe(f2d, h2, gate, p2d, wg, wu, wd, gp, wpg, wpp):
    n, d = h2.shape
    tm = min(TOKEN_TILE, n)
    n_e, _, dff = wg.shape
    row = lambda w: pl.BlockSpec((tm, w), lambda i, e: (i, 0))
    return pl.pallas_call(
        _moe_kernel,
        grid=(n // tm, n_e),
        in_specs=[
            row(d), row(d), row(LANES), row(p2d.shape[1]),
            pl.BlockSpec((None, d, dff), lambda i, e: (e, 0, 0)),
            pl.BlockSpec((None, d, dff), lambda i, e: (e, 0, 0)),
            pl.BlockSpec((None, dff, d), lambda i, e: (e, 0, 0)),
            _const_spec((1, d)), _const_spec(wpg.shape), _const_spec(wpp.shape),
        ],
        out_specs=row(d),
        out_shape=jax.ShapeDtypeStruct((n, d), F32),
        scratch_shapes=[pltpu.VMEM((tm, d), F32)],
        compiler_params=_cparams(("arbitrary", "arbitrary")),
        name="moe_ple",
    )(f2d, h2, gate, p2d, wg, wu, wd, gp, wpg, wpp)


def _bucket_thresholds():
    n = jnp.arange(MAX_DISTANCE + 1)
    max_exact = N_BUCKETS // 2
    nf = jnp.maximum(n, 1).astype(F32)
    large = max_exact + (jnp.log(nf / max_exact) / math.log(MAX_DISTANCE / max_exact)
                         * (N_BUCKETS - max_exact)).astype(jnp.int32)
    bucket = jnp.where(n < max_exact, n, jnp.minimum(large, N_BUCKETS - 1))
    return jnp.sum(bucket[None, :] < jnp.arange(N_BUCKETS)[:, None], axis=1).astype(jnp.int32)


def _row(v):
    return v.reshape(1, -1).astype(F32)


def _key_gain_row(k_gain, cols):
    g = jnp.concatenate([k_gain, jnp.ones_like(k_gain)])
    return jnp.tile(g, cols // GROUP_COLS).reshape(1, cols).astype(F32)


def _rows_on_lanes(x):
    lead = x.shape[:-4]
    rows = x.shape[-4]
    nl = len(lead)
    perm = tuple(range(nl)) + (nl + 1, nl + 2, nl + 3, nl)
    return jnp.transpose(x, perm).reshape(*lead, N_KV_HEADS, GROUP_COLS, rows)


def _rows_on_lanes_inverse(xt):
    b, _, rows = xt.shape
    x = xt.reshape(b, N_KV_HEADS, 2, HEAD_DIM, rows)
    return jnp.transpose(x, (0, 4, 1, 2, 3))[None]


def kernel(x_prompt, x_sample, state_conv, cache_cmp, cache_sel, state_win, page_table,
           p_prompt, p_sample, norm_mix, norm_ffn, norm_ple, conv_w_in, conv_w, conv_w_out,
           nsa_w_in, nsa_w_cmp, nsa_q_norm, nsa_k_norm, nsa_w_out, rel_bias,
           ffn_w_gate, ffn_w_up, ffn_w_down, moe_router, moe_w_gate, moe_w_up, moe_w_down,
           ple_w_proj, ple_w_gate):
    bp, sp, d = x_prompt.shape
    bs, ss, _ = x_sample.shape
    n_p = bp * sp
    n_s = bs * ss
    page = cache_cmp.shape[2]
    past_len = page_table.shape[1] * page
    hpg = HEADS_PER_GROUP
    bf = lambda w: w.astype(BF16)

    cw = jnp.zeros((SUBLANES, d), F32).at[:CONV_WIDTH].set(conv_w[0])
    w_in0, w_out0 = bf(conv_w_in[0]), bf(conv_w_out[0])
    g_mix0 = _row(norm_mix[0])
    h_p, tail_p = _mix0_prompt(x_prompt, g_mix0, w_in0, cw, w_out0)
    conv_prompt = tail_p[:, SUBLANES - (CONV_WIDTH - 1):][None]

    st = state_conv[0]
    zeros = jnp.zeros((bs, ss, d), F32)
    s1 = zeros.at[:, 0].set(st[:, 1]).reshape(n_s, d)
    s2 = zeros.at[:, 0].set(st[:, 0]).at[:, 1].set(st[:, 1]).reshape(n_s, d)
    h_s, u_s = _mix0_sample(x_sample.reshape(n_s, d), g_mix0, w_in0, cw, w_out0, s1, s2, ss)
    conv_sample = u_s.reshape(bs, ss, d)[:, ss - (CONV_WIDTH - 1):][None]

    ffn0_w = (_row(norm_ffn[0]), bf(ffn_w_gate[0]), bf(ffn_w_up[0]), bf(ffn_w_down[0]),
              _row(norm_ple[0]), bf(ple_w_gate[0]), bf(ple_w_proj[0]))
    h_p = _ffn0(h_p.reshape(n_p, d), p_prompt[0].reshape(n_p, -1), *ffn0_w)
    h_s = _ffn0(h_s, p_sample[0].reshape(n_s, -1), *ffn0_w)

    q_cols = N_HEADS * HEAD_DIM
    kv_cols = N_BRANCHES * KV_COLS
    w_in1 = nsa_w_in[0]
    wq = bf(w_in1[:, :q_cols])
    wkv = w_in1[:, q_cols:q_cols + kv_cols]
    wkc = bf(wkv[:, :KV_COLS])
    wkvt = bf(wkv.T)
    wv_src = wkv[:, KV_COLS:].reshape(d, 2 * N_KV_HEADS, 2, HEAD_DIM)[:, :, 1]
    wv = bf(jnp.zeros((d, 2 * N_KV_HEADS, LANES), F32).at[:, :, :HEAD_DIM].set(wv_src).reshape(
        d, 2 * KV_COLS))
    wg_src = w_in1[:, q_cols + kv_cols:].reshape(d, N_BRANCHES, N_KV_HEADS, hpg)
    wgt = jnp.zeros((d, N_KV_HEADS, LANES), F32).at[:, :, :N_BRANCHES * hpg].set(
        wg_src.transpose(0, 2, 1, 3).reshape(d, N_KV_HEADS, N_BRANCHES * hpg))
    wgt = bf(wgt.reshape(d, N_KV_HEADS * LANES))
    gi = jnp.arange(MXU_DIM) // HEAD_DIM
    gmat = (gi[:, None] == gi[None, :]).astype(BF16)
    qg = jnp.tile(nsa_q_norm[0], N_HEADS).reshape(1, q_cols).astype(F32)
    kgc = _key_gain_row(nsa_k_norm[0, 0], KV_COLS)
    kgs_col = nsa_k_norm[0, 1].reshape(HEAD_DIM, 1).astype(F32)
    kgw_col = nsa_k_norm[0, 2].reshape(HEAD_DIM, 1).astype(F32)
    proj_w = (_row(norm_mix[1]), wq, wkc, wkvt, wv, wgt, gmat, qg, kgs_col, kgw_col)

    per_tile = MXU_DIM // HEAD_DIM
    wc = nsa_w_cmp[0].reshape(CMP_BLOCK, 2, per_tile, HEAD_DIM, HEAD_DIM)
    w2 = bf(jnp.einsum('lhade,ab->lhadbe', wc, jnp.eye(per_tile, dtype=F32)).reshape(
        CMP_BLOCK, 2, MXU_DIM, MXU_DIM))

    thr = _bucket_thresholds()
    tbl = rel_bias.astype(F32)

    q_p, kvc_p, kvct_p, kvst_p, kvwt_p, ksn_p, kwn_p, vr_p, gt_p = _nsa_proj(h_p, bp, *proj_w)
    kcn_p = _compress_prompt(kvc_p, w2, gmat, kgc)
    o_p = _nsa_prompt_attn(tbl, thr, q_p, kcn_p, ksn_p, kwn_p, vr_p, gt_p, bp, sp)

    q_s, _, kvct_s, kvst_s, kvwt_s, ksn_s, kwn_s, _, gt_s = _nsa_proj(h_s, 1, *proj_w)
    pt_flat = page_table.reshape(-1).astype(jnp.int32)
    kcn_s = _compress_paged(pt_flat, _rows_on_lanes(cache_cmp[0]), w2, gmat, kgc, PAGES_PER_STEP)
    q5 = q_s.astype(F32).reshape(bs, ss, N_KV_HEADS, hpg, HEAD_DIM)
    q_rt = q5.transpose(0, 2, 3, 1, 4).reshape(bs, N_KV_HEADS, hpg * ss, HEAD_DIM)
    q_tr = jnp.zeros((bs, N_KV_HEADS, ss, SUBLANES, HEAD_DIM), F32).at[:, :, :, :hpg].set(
        q5.transpose(0, 2, 1, 3, 4))
    tbl_gr = tbl.T.reshape(N_KV_HEADS, hpg, N_BUCKETS)
    tbl_rt = jnp.repeat(tbl_gr, ss, axis=1)
    tbl_r8 = jnp.zeros((N_KV_HEADS, SUBLANES, N_BUCKETS), F32).at[:, :hpg].set(tbl_gr)
    oc_s, idx = _sample_cmp(thr, q_rt, kcn_s, tbl_rt, bs, past_len, ss)
    n_top_s = min(N_SELECTED - 1, past_len // SEL_BLOCK)
    idx_flat = idx[:, :, :n_top_s + 1].reshape(-1)

    def new_rows(kn_t):
        r = kn_t[0].astype(F32).reshape(N_KV_HEADS, GROUP_COLS, bs, ss).transpose(2, 0, 3, 1)
        return jnp.zeros((bs, N_KV_HEADS, SUBLANES, GROUP_COLS), F32).at[:, :, :ss].set(r)

    win_t = _rows_on_lanes(state_win[0])
    os_s, ow_s = _sample_sel_win(idx_flat, pt_flat, thr, _rows_on_lanes(cache_sel[0]), q_tr,
                                 new_rows(ksn_s), new_rows(kwn_s), win_t, tbl_r8,
                                 kgs_col, kgw_col, bs, past_len, ss)
    oc_s2 = oc_s.reshape(bs, N_KV_HEADS, hpg, ss, HEAD_DIM).transpose(0, 3, 1, 2, 4).reshape(n_s, d)
    to2d = lambda o: o[:, :, :, :hpg].transpose(0, 2, 1, 3, 4).reshape(n_s, d)

    w_out1 = bf(nsa_w_out[0])
    router = jnp.zeros((d, LANES), F32).at[:, :N_EXPERTS].set(moe_router[0])
    r_hi = bf(router)
    r_lo = bf(router - r_hi.astype(F32))
    gf1 = _row(norm_ffn[1])
    moe_w = (bf(moe_w_gate[0]), bf(moe_w_up[0]), bf(moe_w_down[0]), _row(norm_ple[1]),
             bf(ple_w_gate[1]), bf(ple_w_proj[1]))
    h2_p, f_p, gate_p = _attn_out(h_p, [o_p], None, w_out1, gf1, r_hi, r_lo)
    y_p = _moe(f_p, h2_p, gate_p, p_prompt[1].reshape(n_p, -1), *moe_w)
    h2_s, f_s, gate_s = _attn_out(h_s, [oc_s2, to2d(os_s), to2d(ow_s)], gt_s, w_out1, gf1,
                                  r_hi, r_lo)
    y_s = _moe(f_s, h2_s, gate_s, p_sample[1].reshape(n_s, -1), *moe_w)

    sample_rows = lambda xt: _rows_on_lanes_inverse(
        xt[0].reshape(KV_COLS, bs, ss).transpose(1, 0, 2))
    wbp = min(WINDOW, sp)
    win_s_t = jnp.concatenate(
        [win_t, kvwt_s[0].reshape(N_KV_HEADS, GROUP_COLS, bs, ss).transpose(2, 0, 1, 3)],
        axis=3)[..., ss:]
    return (y_p.reshape(bp, sp, d), y_s.reshape(bs, ss, d), conv_prompt, conv_sample,
            _rows_on_lanes_inverse(kvct_p), sample_rows(kvct_s),
            _rows_on_lanes_inverse(kvst_p), sample_rows(kvst_s),
            _rows_on_lanes_inverse(kvwt_p[:, :, sp - wbp:]),
            _rows_on_lanes_inverse(win_s_t.reshape(bs, KV_COLS, -1)))
```

```python
import functools
import math

import jax
import jax.numpy as jnp
from jax import lax
from jax.experimental import pallas as pl
from jax.experimental.pallas import tpu as pltpu

F32 = jnp.float32
BF16 = jnp.bfloat16

D_MODEL = 1024
N_HEADS = 16
HEAD_DIM = 64
N_KV_HEADS = 4
HEADS_PER_GROUP = 4
N_BRANCHES = 3
CMP_BLOCK = 32
SEL_BLOCK = 64
CMP_PER_SEL = SEL_BLOCK // CMP_BLOCK
N_SELECTED = 16
WINDOW = 512
N_BUCKETS = 32
MAX_DISTANCE = 128
N_EXPERTS = 8
CONV_WIDTH = 3
EPS = 1e-6
FORCED_SCORE = 1e4
NEG = -1e30
KV_COLS = N_KV_HEADS * 2 * HEAD_DIM
GROUP_COLS = 2 * HEAD_DIM
Q_GROUP_COLS = HEADS_PER_GROUP * HEAD_DIM

LANES = 128
SUBLANES = 8
MXU_DIM = 256
VMEM_LIMIT = 56 * 1024 * 1024

TOKEN_TILE = 512
ATTN_TILE = 256
PAGES_PER_STEP = 32


def _cparams(sem, vmem=VMEM_LIMIT):
    return pltpu.CompilerParams(dimension_semantics=sem, vmem_limit_bytes=vmem)


def _const_spec(shape):
    nd = len(shape)
    return pl.BlockSpec(shape, lambda *_: (0,) * nd, pipeline_mode=pl.Buffered(1))


def _smem_spec():
    return pl.BlockSpec(memory_space=pltpu.SMEM)


def _dot(a, b):
    return jnp.dot(a, b, preferred_element_type=F32)


def _dot_nt(a, b):
    return lax.dot_general(a, b, (((1,), (1,)), ((), ())), preferred_element_type=F32)


def _split3(x):
    hi = x.astype(BF16)
    r1 = x - hi.astype(F32)
    mid = r1.astype(BF16)
    lo = (r1 - mid.astype(F32)).astype(BF16)
    return hi, mid, lo


def _exact_dot(x, m01):
    hi, mid, lo = _split3(x)
    return (_dot(hi, m01) + _dot(mid, m01)) + _dot(lo, m01)


def _rmsnorm(x, g):
    ms = jnp.mean(x * x, axis=-1, keepdims=True)
    return x * lax.rsqrt(ms + EPS) * g


def _group_mean_sq(x, gmat):
    n = x.shape[1]
    w = gmat.shape[0]
    outs = []
    for c in range(n // w):
        blk = x[:, c * w:(c + 1) * w]
        sq = blk * blk
        hi = sq.astype(BF16)
        lo = (sq - hi.astype(F32)).astype(BF16)
        outs.append(_dot(hi, gmat) + _dot(lo, gmat))
    out = outs[0] if len(outs) == 1 else jnp.concatenate(outs, axis=1)
    return out * (1.0 / HEAD_DIM)


def _norm_keys(kv, gain_row, gmat):
    ms = _group_mean_sq(kv, gmat)
    lane = lax.broadcasted_iota(jnp.int32, kv.shape, 1)
    is_k = (lane % GROUP_COLS) < HEAD_DIM
    return jnp.where(is_k, kv * lax.rsqrt(ms + EPS) * gain_row, kv)


def _bucket_bias(dist, thr_ref, value_of_bucket):
    val = value_of_bucket(0)
    val = jnp.broadcast_to(val, dist.shape).astype(F32)
    for k in range(1, N_BUCKETS):
        val = jnp.where(dist >= thr_ref[k], value_of_bucket(k), val)
    return val


def _silu(x):
    return x * jax.nn.sigmoid(x)


def _ple(h, p, gain, wpg_ref, wpp_ref):
    r = _rmsnorm(h, gain).astype(BF16)
    g = jax.nn.sigmoid(_dot(r, wpg_ref[...]))
    return h + g * _dot(p.astype(BF16), wpp_ref[...])


def _conv_mix_tail(x, gb, u, um1, um2, cw_ref, wout_ref):
    conv = cw_ref[0:1, :] * um2 + cw_ref[1:2, :] * um1 + cw_ref[2:3, :] * u
    y = _dot((gb * conv).astype(BF16), wout_ref[...])
    return x + y


def _mix0_prompt_kernel(x_ref, g_ref, win_ref, cw_ref, wout_ref, h_ref, st_ref, carry_ref):
    j = pl.program_id(1)
    tm = x_ref.shape[0]

    @pl.when(j == 0)
    def _():
        carry_ref[...] = jnp.zeros_like(carry_ref)

    x = x_ref[...]
    a = _rmsnorm(x, g_ref[...]).astype(BF16)
    proj = _dot(a, win_ref[...])
    gb = proj[:, :D_MODEL]
    u = proj[:, D_MODEL:2 * D_MODEL] * proj[:, 2 * D_MODEL:]
    c0 = carry_ref[SUBLANES - 2:SUBLANES - 1, :]
    c1 = carry_ref[SUBLANES - 1:SUBLANES, :]
    row = lax.broadcasted_iota(jnp.int32, u.shape, 0)
    um1 = jnp.where(row == 0, c1, pltpu.roll(u, 1, 0))
    um2 = jnp.where(row == 0, c0, jnp.where(row == 1, c1, pltpu.roll(u, 2, 0)))
    h_ref[...] = _conv_mix_tail(x, gb, u, um1, um2, cw_ref, wout_ref)
    tail = u[tm - SUBLANES:, :]
    carry_ref[...] = tail
    st_ref[...] = tail


def _mix0_sample_kernel(x_ref, g_ref, win_ref, cw_ref, wout_ref, s1_ref, s2_ref, h_ref, u_ref,
                        *, seq):
    x = x_ref[...]
    a = _rmsnorm(x, g_ref[...]).astype(BF16)
    proj = _dot(a, win_ref[...])
    gb = proj[:, :D_MODEL]
    u = proj[:, D_MODEL:2 * D_MODEL] * proj[:, 2 * D_MODEL:]
    t = lax.broadcasted_iota(jnp.int32, u.shape, 0) % seq
    um1 = jnp.where(t >= 1, pltpu.roll(u, 1, 0), s1_ref[...])
    um2 = jnp.where(t >= 2, pltpu.roll(u, 2, 0), s2_ref[...])
    h_ref[...] = _conv_mix_tail(x, gb, u, um1, um2, cw_ref, wout_ref)
    u_ref[...] = u


def _mix0_prompt(x, gain, w_in, cw, w_out):
    b, s, d = x.shape
    tm = min(TOKEN_TILE, s)
    grid = (b, s // tm)
    return pl.pallas_call(
        _mix0_prompt_kernel,
        grid=grid,
        in_specs=[
            pl.BlockSpec((None, tm, d), lambda i, j: (i, j, 0)),
            _const_spec((1, d)),
            _const_spec(w_in.shape),
            _const_spec(cw.shape),
            _const_spec(w_out.shape),
        ],
        out_specs=[
            pl.BlockSpec((None, tm, d), lambda i, j: (i, j, 0)),
            pl.BlockSpec((None, SUBLANES, d), lambda i, j: (i, 0, 0)),
        ],
        out_shape=[
            jax.ShapeDtypeStruct((b, s, d), F32),
            jax.ShapeDtypeStruct((b, SUBLANES, d), F32),
        ],
        scratch_shapes=[pltpu.VMEM((SUBLANES, d), F32)],
        compiler_params=_cparams(("arbitrary", "arbitrary")),
        name="mix0_prompt",
    )(x, gain, w_in, cw, w_out)


def _mix0_sample(x2d, gain, w_in, cw, w_out, s1, s2, seq):
    n, d = x2d.shape
    return pl.pallas_call(
        functools.partial(_mix0_sample_kernel, seq=seq),
        out_shape=[jax.ShapeDtypeStruct((n, d), F32), jax.ShapeDtypeStruct((n, d), F32)],
        compiler_params=_cparams(None),
        name="mix0_sample",
    )(x2d, gain, w_in, cw, w_out, s1, s2)


def _ffn0_kernel(h_ref, p_ref, gf_ref, wg_ref, wu_ref, wd_ref, gp_ref, wpg_ref, wpp_ref, o_ref,
                 *, chunk):
    h = h_ref[...]
    f = _rmsnorm(h, gf_ref[...]).astype(BF16)
    d_ff = wg_ref.shape[1]
    acc = None
    for c in range(d_ff // chunk):
        sl = slice(c * chunk, (c + 1) * chunk)
        hid = _silu(_dot(f, wg_ref[:, sl])) * _dot(f, wu_ref[:, sl])
        part = _dot(hid.astype(BF16), wd_ref[sl, :])
        acc = part if acc is None else acc + part
    o_ref[...] = _ple(h + acc, p_ref[...], gp_ref[...], wpg_ref, wpp_ref)


def _ffn0(h2d, p2d, gf, wg, wu, wd, gp, wpg, wpp):
    n, d = h2d.shape
    tm = min(TOKEN_TILE, n)
    d_ff = wg.shape[1]
    chunk = d_ff // 2
    assert chunk % LANES == 0
    return pl.pallas_call(
        functools.partial(_ffn0_kernel, chunk=chunk),
        grid=(n // tm,),
        in_specs=[
            pl.BlockSpec((tm, d), lambda i: (i, 0)),
            pl.BlockSpec((tm, p2d.shape[1]), lambda i: (i, 0)),
            _const_spec((1, d)),
            _const_spec(wg.shape), _const_spec(wu.shape), _const_spec(wd.shape),
            _const_spec((1, d)),
            _const_spec(wpg.shape), _const_spec(wpp.shape),
        ],
        out_specs=pl.BlockSpec((tm, d), lambda i: (i, 0)),
        out_shape=jax.ShapeDtypeStruct((n, d), F32),
        compiler_params=_cparams(("arbitrary",)),
        name="ffn0_ple",
    )(h2d, p2d, gf, wg, wu, wd, gp, wpg, wpp)


def _norm_keys_t(kvt, gain_col):
    parts = []
    for gi in range(KV_COLS // HEAD_DIM):
        x = kvt[gi * HEAD_DIM:(gi + 1) * HEAD_DIM, :]
        if gi % 2 == 0:
            ms = jnp.mean(x * x, axis=0, keepdims=True)
            x = x * lax.rsqrt(ms + EPS) * gain_col
        parts.append(x)
    return jnp.concatenate(parts, axis=0)


def _nsa_proj_kernel(h_ref, g_ref, wq_ref, wkc_ref, wkvt_ref, wv_ref, wgt_ref, gm_ref, qg_ref,
                     kgs_ref, kgw_ref, q_ref, kvc_ref, kvct_ref, kvst_ref, kvwt_ref, ksn_ref,
                     kwn_ref, vr_ref, gt_ref):
    a = _rmsnorm(h_ref[...], g_ref[...]).astype(BF16)
    vr = _dot(a, wv_ref[...])
    lane = lax.broadcasted_iota(jnp.int32, vr.shape, 1)
    vr_ref[...] = jnp.where(lane % LANES < HEAD_DIM, vr, 1.0).astype(vr_ref.dtype)
    q = _dot(a, wq_ref[...])
    ms = _group_mean_sq(q, gm_ref[...])
    q_ref[...] = ((q * lax.rsqrt(ms + EPS) * qg_ref[...]) * (HEAD_DIM ** -0.5)).astype(q_ref.dtype)
    kvc_ref[...] = _dot(a, wkc_ref[...])
    kvt = _dot_nt(wkvt_ref[...], a)
    kvst = kvt[KV_COLS:2 * KV_COLS]
    kvwt = kvt[2 * KV_COLS:]
    kvct_ref[...] = kvt[:KV_COLS]
    kvst_ref[...] = kvst
    kvwt_ref[...] = kvwt
    ksn_ref[...] = _norm_keys_t(kvst, kgs_ref[...]).astype(ksn_ref.dtype)
    kwn_ref[...] = _norm_keys_t(kvwt, kgw_ref[...]).astype(kwn_ref.dtype)
    gt_ref[...] = jax.nn.sigmoid(_dot(a, wgt_ref[...]))


def _nsa_proj(h2d, n_seq, gain, wq, wkc, wkvt, wv, wgt, gmat, qg, kgs_col, kgw_col):
    n, d = h2d.shape
    s = n // n_seq
    tm = min(TOKEN_TILE, s)
    tps = s // tm
    row = lambda w: pl.BlockSpec((tm, w), lambda i: (i, 0))
    col = pl.BlockSpec((None, KV_COLS, tm), lambda i: (i // tps, 0, i % tps))
    tshape = lambda dt: jax.ShapeDtypeStruct((n_seq, KV_COLS, s), dt)
    n_gate_cols = wgt.shape[1]
    return pl.pallas_call(
        _nsa_proj_kernel,
        grid=(n // tm,),
        in_specs=[
            row(d), _const_spec((1, d)),
            _const_spec(wq.shape), _const_spec(wkc.shape), _const_spec(wkvt.shape),
            _const_spec(wv.shape), _const_spec(wgt.shape), _const_spec(gmat.shape),
            _const_spec(qg.shape), _const_spec(kgs_col.shape), _const_spec(kgw_col.shape),
        ],
        out_specs=[row(d), row(KV_COLS), col, col, col, col, col, row(wv.shape[1]),
                   row(n_gate_cols)],
        out_shape=[
            jax.ShapeDtypeStruct((n, d), BF16),
            jax.ShapeDtypeStruct((n, KV_COLS), F32),
            tshape(F32), tshape(F32), tshape(F32), tshape(BF16), tshape(BF16),
            jax.ShapeDtypeStruct((n, wv.shape[1]), BF16),
            jax.ShapeDtypeStruct((n, n_gate_cols), F32),
        ],
        compiler_params=_cparams(("arbitrary",)),
        name="nsa_proj",
    )(h2d, gain, wq, wkc, wkvt, wv, wgt, gmat, qg, kgs_col, kgw_col)


SLABS = KV_COLS // LANES


def _compress_rows(read_slab, n_blocks, w_ref, gmat, kg_row):
    half = KV_COLS // 2
    per_half = SLABS // 2
    acc = [jnp.zeros((n_blocks, half), F32) for _ in range(2)]
    for l in range(CMP_BLOCK):
        for hf in range(2):
            xl = jnp.concatenate([read_slab(l, hf * per_half + c) for c in range(per_half)], axis=1)
            acc[hf] = acc[hf] + _dot(xl.astype(BF16), w_ref[l, hf])
    kv = jnp.concatenate(acc, axis=1)
    return _norm_keys(kv, kg_row, gmat).astype(BF16)


def _compress_kernel(x_ref, w_ref, gm_ref, kg_ref, o_ref):
    nb = o_ref.shape[0]
    read = lambda l, g: x_ref[pl.ds(l * SLABS + g, nb, stride=CMP_BLOCK * SLABS), :]
    o_ref[...] = _compress_rows(read, nb, w_ref, gm_ref[...], kg_ref[...])


def _compress_prompt(rows2d, w2, gmat, kg):
    n = rows2d.shape[0]
    tr = min(4096, n)
    nb = tr // CMP_BLOCK
    return pl.pallas_call(
        _compress_kernel,
        grid=(n // tr,),
        in_specs=[pl.BlockSpec((tr * SLABS, LANES), lambda i: (i, 0)),
                  _const_spec(w2.shape), _const_spec(gmat.shape), _const_spec(kg.shape)],
        out_specs=pl.BlockSpec((nb, KV_COLS), lambda i: (i, 0)),
        out_shape=jax.ShapeDtypeStruct((n // CMP_BLOCK, KV_COLS), BF16),
        compiler_params=_cparams(("arbitrary",)),
        name="compress_prompt",
    )(rows2d.reshape(n * SLABS, LANES), w2, gmat, kg)


def _compress_paged_kernel(pt_ref, pool_ref, w_ref, gm_ref, kg_ref, o_ref, buf_ref, st_ref, sem_ref,
                           *, pages_per_step, page):
    i = pl.program_id(0)
    n = pl.num_programs(0)

    def page_copy(step, p, slot):
        phys = pt_ref[step * pages_per_step + p]
        return pltpu.make_async_copy(pool_ref.at[phys], buf_ref.at[slot, p], sem_ref.at[slot])

    def start(step, slot):
        def issue(p, c):
            page_copy(step, p, slot).start()
            return c
        lax.fori_loop(0, pages_per_step, issue, 0)

    @pl.when(i == 0)
    def _():
        start(0, 0)

    @pl.when(i + 1 < n)
    def _():
        start(i + 1, (i + 1) % 2)

    slot = i % 2

    def wait(p, c):
        page_copy(i, p, slot).wait()
        return c
    lax.fori_loop(0, pages_per_step, wait, 0)

    blocks_per_page = page // CMP_BLOCK
    group = 2 * blocks_per_page
    assert group == SUBLANES
    ri = lax.broadcasted_iota(jnp.int32, (2 * page, 2 * page), 0)
    ci = lax.broadcasted_iota(jnp.int32, (2 * page, 2 * page), 1)
    src_lane = ((ri % group) // blocks_per_page) * page + (ri % blocks_per_page) * CMP_BLOCK + ri // group
    perm = (ci == src_lane).astype(BF16)

    def to_rows(pp, c):
        for g in range(N_KV_HEADS):
            pair = jnp.concatenate([buf_ref[slot, 2 * pp, g], buf_ref[slot, 2 * pp + 1, g]],
                                   axis=1).astype(BF16)
            rows = _dot_nt(perm, pair)
            for l in range(CMP_BLOCK):
                st_ref[g, l, pl.ds(pl.multiple_of(pp * group, group), group), :] = (
                    rows[l * group:(l + 1) * group])
        return c
    lax.fori_loop(0, pages_per_step // 2, to_rows, 0)

    nb = pages_per_step * page // CMP_BLOCK
    read = lambda l, g: st_ref[g, l]
    o_ref[...] = _compress_rows(read, nb, w_ref, gm_ref[...], kg_ref[...])


def _compress_paged(page_table_flat, pool_t, w2, gmat, kg, pages_per_step):
    n_pages = page_table_flat.shape[0]
    page = pool_t.shape[3]
    assert n_pages % pages_per_step == 0 and page == LANES and pool_t.shape[2] == GROUP_COLS
    steps = n_pages // pages_per_step
    nb = pages_per_step * page // CMP_BLOCK
    grid_spec = pltpu.PrefetchScalarGridSpec(
        num_scalar_prefetch=1,
        grid=(steps,),
        in_specs=[pl.BlockSpec(memory_space=pl.ANY),
                  pl.BlockSpec(w2.shape, lambda i, pt: (0, 0, 0, 0), pipeline_mode=pl.Buffered(1)),
                  pl.BlockSpec(gmat.shape, lambda i, pt: (0, 0)),
                  pl.BlockSpec(kg.shape, lambda i, pt: (0, 0))],
        out_specs=pl.BlockSpec((nb, KV_COLS), lambda i, pt: (i, 0)),
        scratch_shapes=[pltpu.VMEM((2, pages_per_step, N_KV_HEADS, GROUP_COLS, page), F32),
                        pltpu.VMEM((N_KV_HEADS, CMP_BLOCK, nb, GROUP_COLS), F32),
                        pltpu.SemaphoreType.DMA((2,))],
    )
    return pl.pallas_call(
        functools.partial(_compress_paged_kernel, pages_per_step=pages_per_step, page=page),
        grid_spec=grid_spec,
        out_shape=jax.ShapeDtypeStruct((steps * nb, KV_COLS), BF16),
        compiler_params=_cparams(("arbitrary",)),
        name="compress_paged",
    )(page_table_flat, pool_t, w2, gmat, kg)


FLASH_ROWS = 128


def _flash_tile(qh_ref, kt_ref, v_ref, kt, tk, state, terms):
    m_ref, acc_ref = state
    start = pl.multiple_of(kt * tk, tk)
    k = kt_ref[:, pl.ds(start, tk)]
    v = v_ref[pl.ds(start, tk), :]
    tq = qh_ref.shape[1]
    blocks = [(r, pl.ds(rb * FLASH_ROWS, FLASH_ROWS))
              for r in range(HEADS_PER_GROUP) for rb in range(tq // FLASH_ROWS)]
    old = [(m_ref[r, rows, :], acc_ref[r, rows, :]) for r, rows in blocks]
    new = []
    for (r, rows), (m_old, acc_old) in zip(blocks, old):
        s = _dot(qh_ref[r, rows, :], k)
        for term in terms(r, rows):
            s = s + term
        m_new = jnp.maximum(m_old, jnp.max(s, axis=-1, keepdims=True))
        e = jnp.exp(s - jnp.concatenate([m_new] * (tk // LANES), axis=1))
        new.append((m_new, jnp.exp(m_old - m_new) * acc_old + _dot(e.astype(BF16), v)))
    for (r, rows), (m_new, acc_new) in zip(blocks, new):
        m_ref[r, rows, :] = m_new
        acc_ref[r, rows, :] = acc_new


def _nsa_prompt_kernel(tbl_ref, thr_ref, q_ref, kc_ref, ks_ref, vs_ref, kw_ref, vw_ref, gt_ref, o_ref,
                       bias_ref, edge_ref, cbias_ref, amask_ref, qh_ref, m_ref, acc_ref, *, n_top):
    g = pl.program_id(0)
    b = pl.program_id(1)
    qi = pl.program_id(2)
    tq = q_ref.shape[0]
    tk = tq
    hpg = HEADS_PER_GROUP
    nc = kc_ref.shape[0]
    ns = nc // CMP_PER_SEL
    blocks_per_tile = tk // SEL_BLOCK
    far = N_BUCKETS - 1

    qpos = qi * tq + lax.broadcasted_iota(jnp.int32, (tq, 1), 0)
    n_io = lax.broadcasted_iota(jnp.int32, (tq, nc), 1)
    relc = qpos - ((n_io + 1) * CMP_BLOCK - 1)

    n_back = WINDOW // tk

    @pl.when((b == 0) & (qi == 0))
    def _():
        diag = (lax.broadcasted_iota(jnp.int32, (tq, tk), 0)
                - lax.broadcasted_iota(jnp.int32, (tq, tk), 1))
        for r in range(hpg):
            h = g * hpg + r
            shifted = lambda k: tbl_ref[k, h] - tbl_ref[far, h]
            bias_ref[r, 0] = jnp.where(diag >= 0, _bucket_bias(diag, thr_ref, shifted), NEG)
            bias_ref[r, 1] = _bucket_bias(diag + tk, thr_ref, shifted)
        edge_ref[...] = jnp.where(diag + n_back * tk < WINDOW, 0.0, NEG)

    @pl.when(b == 0)
    def _():
        for r in range(hpg):
            h = g * hpg + r
            cbias_ref[r, qi] = _bucket_bias(relc, thr_ref, lambda k: tbl_ref[k, h])

    q = q_ref[...]
    for r in range(hpg):
        qh_ref[r] = q[:, r * HEAD_DIM:(r + 1) * HEAD_DIM]

    kc = kc_ref[...]
    kck = kc[:, :HEAD_DIM]
    kcv = kc[:, HEAD_DIM:]
    maskc = relc >= 0
    imp = None
    oc = []
    for r in range(hpg):
        s = _dot_nt(q[:, r * HEAD_DIM:(r + 1) * HEAD_DIM], kck) + cbias_ref[r, qi]
        sm = jnp.where(maskc, s, NEG)
        e = jnp.exp(sm - jnp.max(sm, axis=-1, keepdims=True))
        p = jnp.where(maskc, e / jnp.sum(e, axis=-1, keepdims=True), 0.0)
        imp = p if imp is None else imp + p
        oc.append(_dot(p.astype(BF16), kcv))

    pair_t = (lax.broadcasted_iota(jnp.int32, (ns, nc), 0)
              == lax.broadcasted_iota(jnp.int32, (ns, nc), 1) // CMP_PER_SEL).astype(BF16)
    hi, mid, lo = _split3(imp)
    imp_sel = (_dot_nt(pair_t, hi) + _dot_nt(pair_t, mid)) + _dot_nt(pair_t, lo)
    blk = lax.broadcasted_iota(jnp.int32, (ns, tq), 0)
    cur = (qi * tq + lax.broadcasted_iota(jnp.int32, (1, tq), 1)) // SEL_BLOCK
    forced = (blk == 0) | (blk == cur) | (blk == cur - 1)
    score = jnp.where(forced, FORCED_SCORE, jnp.where(blk <= cur, imp_sel, -1.0))
    rank = jnp.zeros((ns, tq), F32)
    for i in range(ns):
        si = score[i:i + 1, :]
        beats = (si > score) | ((si == score) & (blk > i))
        rank = rank + jnp.where(beats, 1.0, 0.0)
    sel_t = jnp.where(rank < n_top, 1.0, 0.0).astype(BF16)
    eye = (lax.broadcasted_iota(jnp.int32, (tq, tq), 0)
           == lax.broadcasted_iota(jnp.int32, (tq, tq), 1)).astype(BF16)
    sel = _dot_nt(eye, sel_t).astype(BF16)

    sel_state = (m_ref.at[0], acc_ref.at[0])
    win_state = (m_ref.at[1], acc_ref.at[1])
    m_ref[...] = jnp.full(m_ref.shape, NEG, F32)
    acc_ref[...] = jnp.zeros(acc_ref.shape, F32)

    def finish(branch):
        out = []
        for r in range(hpg):
            acc = acc_ref[branch, r]
            out.append((acc / pltpu.roll(acc, HEAD_DIM, 1))[:, :HEAD_DIM])
        return out

    def win_tile(kt, terms):
        _flash_tile(qh_ref, kw_ref, vw_ref, kt, tk, win_state, terms)

    def sel_tile(kt, tiles_back):
        expand = (lax.broadcasted_iota(jnp.int32, (ns, tk), 0)
                  == kt * blocks_per_tile + lax.broadcasted_iota(jnp.int32, (ns, tk), 1) // SEL_BLOCK)
        amask_ref[...] = jnp.where(_dot(sel, expand.astype(BF16)) > 0.5, 0.0, NEG)
        if tiles_back is None:
            terms = lambda r, rows: (amask_ref[rows, :],)
        else:
            terms = lambda r, rows: (bias_ref[r, tiles_back, rows, :], amask_ref[rows, :])
        _flash_tile(qh_ref, ks_ref, vs_ref, kt, tk, sel_state, terms)

    def far_tile(kt, c):
        sel_tile(kt, None)
        return c
    lax.fori_loop(0, jnp.maximum(qi - 1, 0), far_tile, 0)

    win_far = lambda: win_tile(qi - n_back, lambda r, rows: (edge_ref[rows, :],))
    win_prev = lambda: win_tile(qi - 1, lambda r, rows: (bias_ref[r, 1, rows, :],))

    @pl.when(qi >= n_back)
    def _():
        sel_tile(qi - 1, 1)
        win_far()

    @pl.when((qi >= 1) & (qi < n_back))
    def _():
        sel_tile(qi - 1, 1)

    @pl.when(qi >= 1)
    def _():
        sel_tile(qi, 0)
        win_prev()

    @pl.when(qi == 0)
    def _():
        sel_tile(qi, 0)

    o_sel = finish(0)
    win_tile(qi, lambda r, rows: (bias_ref[r, 0, rows, :],))
    o_win = finish(1)

    width = hpg * HEAD_DIM
    idx = lax.broadcasted_iota(jnp.int32, (LANES, N_BRANCHES * width), 0)
    col = lax.broadcasted_iota(jnp.int32, (LANES, N_BRANCHES * width), 1)
    spread = (idx == (col // width) * hpg + (col % width) // HEAD_DIM).astype(BF16)
    gates = _exact_dot(gt_ref[...], spread)
    out = None
    for c, branch in enumerate((oc, o_sel, o_win)):
        term = gates[:, c * width:(c + 1) * width] * jnp.concatenate(branch, axis=1)
        out = term if out is None else out + term
    o_ref[...] = out.astype(o_ref.dtype)


def _nsa_prompt_attn(tbl, thr, q, kcn, ksn_t, kwn_t, v_rows, gates, b, s):
    tq = min(ATTN_TILE, s)
    assert WINDOW == 2 * tq and tq + 1 >= MAX_DISTANCE and tq % FLASH_ROWS == 0
    nq = s // tq
    nc = s // CMP_BLOCK
    ns = s // SEL_BLOCK
    n_top = min(N_SELECTED, ns)
    hpg = HEADS_PER_GROUP
    k_spec = pl.BlockSpec((None, HEAD_DIM, s), lambda g, i, j: (i, 2 * g, 0))
    v_spec = lambda branch: pl.BlockSpec((s, LANES), lambda g, i, j: (i, branch * N_KV_HEADS + g))
    return pl.pallas_call(
        functools.partial(_nsa_prompt_kernel, n_top=n_top),
        grid=(N_KV_HEADS, b, nq),
        in_specs=[
            _smem_spec(), _smem_spec(),
            pl.BlockSpec((tq, Q_GROUP_COLS), lambda g, i, j: (i * nq + j, g)),
            pl.BlockSpec((nc, GROUP_COLS), lambda g, i, j: (i, g)),
            k_spec, v_spec(0), k_spec, v_spec(1),
            pl.BlockSpec((tq, LANES), lambda g, i, j: (i * nq + j, g)),
        ],
        out_specs=pl.BlockSpec((tq, Q_GROUP_COLS), lambda g, i, j: (i * nq + j, g)),
        out_shape=jax.ShapeDtypeStruct((b * s, D_MODEL), BF16),
        scratch_shapes=[pltpu.VMEM((hpg, 2, tq, tq), F32),
                        pltpu.VMEM((tq, tq), F32),
                        pltpu.VMEM((hpg, nq, tq, nc), F32),
                        pltpu.VMEM((tq, tq), F32),
                        pltpu.VMEM((hpg, tq, HEAD_DIM), BF16),
                        pltpu.VMEM((2, hpg, tq, LANES), F32),
                        pltpu.VMEM((2, hpg, tq, LANES), F32)],
        compiler_params=_cparams(("arbitrary", "arbitrary", "arbitrary")),
        name="nsa_prompt_attn",
    )(tbl, thr, q, kcn, ksn_t, v_rows, kwn_t, v_rows, gates)


def _bucket_onehot(dist_row, thr_ref):
    n = dist_row.shape[1]
    bucket = jnp.zeros(dist_row.shape, jnp.int32)
    for k in range(1, N_BUCKETS):
        bucket = bucket + jnp.where(dist_row >= thr_ref[k], 1, 0)
    return (lax.broadcasted_iota(jnp.int32, (N_BUCKETS, n), 0) == bucket).astype(BF16)


def _sample_cmp_kernel(thr_ref, q_ref, kc_ref, tbl_ref, oc_ref, idx_ref, *, past_len, seq, n_top):
    nc = kc_ref.shape[0]
    ns = past_len // SEL_BLOCK
    hpg = HEADS_PER_GROUP
    rows = hpg * seq
    kc = kc_ref[...]
    pair = (lax.broadcasted_iota(jnp.int32, (nc, ns), 0) // CMP_PER_SEL
            == lax.broadcasted_iota(jnp.int32, (nc, ns), 1)).astype(BF16)
    eye = (lax.broadcasted_iota(jnp.int32, (ns, ns), 0)
           == lax.broadcasted_iota(jnp.int32, (ns, ns), 1))
    eye_bf = eye.astype(BF16)
    ii = lax.broadcasted_iota(jnp.int32, (ns, ns), 0)
    jj = lax.broadcasted_iota(jnp.int32, (ns, ns), 1)
    t_row = lax.broadcasted_iota(jnp.int32, (rows, nc), 0) % seq
    n_io = lax.broadcasted_iota(jnp.int32, (rows, nc), 1)
    rel = past_len + t_row - ((n_io + 1) * CMP_BLOCK - 1)
    mask = rel >= 0
    blk = lax.broadcasted_iota(jnp.int32, (SUBLANES, ns), 1)
    forced = (blk == 0) | (blk == ns - 1)
    rank_lane = lax.broadcasted_iota(jnp.int32, (ns, LANES), 1).astype(F32)
    blk_col = lax.broadcasted_iota(jnp.int32, (ns, LANES), 0).astype(F32)
    for g in range(N_KV_HEADS):
        kck = kc[:, g * GROUP_COLS:g * GROUP_COLS + HEAD_DIM]
        kcv = kc[:, g * GROUP_COLS + HEAD_DIM:(g + 1) * GROUP_COLS]
        qg = q_ref[g].astype(BF16)
        tbl_g = tbl_ref[g]
        s = _dot_nt(qg, kck)
        s = s + _bucket_bias(rel, thr_ref, lambda k: tbl_g[:, k:k + 1])
        sm = jnp.where(mask, s, NEG)
        e = jnp.exp(sm - jnp.max(sm, axis=-1, keepdims=True))
        p = jnp.where(mask, e / jnp.sum(e, axis=-1, keepdims=True), 0.0)
        oc_ref[g] = _dot(p.astype(BF16), kcv)
        imp = p[0:seq]
        for r in range(1, hpg):
            imp = imp + p[r * seq:(r + 1) * seq]
        imp = jnp.concatenate([imp, jnp.zeros((SUBLANES - seq, nc), F32)], axis=0)
        score = jnp.where(forced, FORCED_SCORE, _exact_dot(imp, pair))
        hi, mid, lo = _split3(score)
        score_t = (_dot_nt(eye_bf, hi) + _dot_nt(eye_bf, mid)) + _dot_nt(eye_bf, lo)
        for t in range(seq):
            s_row = score[t:t + 1, :]
            s_col = score_t[:, t:t + 1]
            beats = (ii != jj) & ((s_row > s_col) | ((s_row == s_col) & (jj < ii)))
            rank_col = jnp.sum(jnp.where(beats, 1.0, 0.0), axis=1, keepdims=True)
            onehot = rank_col == rank_lane
            idx_row = jnp.sum(jnp.where(onehot, blk_col, 0.0), axis=0, keepdims=True)
            idx_ref[g * seq + t:g * seq + t + 1, :] = idx_row.astype(jnp.int32)


def _sample_cmp(thr, q_rt, kcn, tbl_rt, b, past_len, seq):
    nc = kcn.shape[0] // b
    ns = past_len // SEL_BLOCK
    n_top = min(N_SELECTED - 1, ns)
    rows = HEADS_PER_GROUP * seq
    return pl.pallas_call(
        functools.partial(_sample_cmp_kernel, past_len=past_len, seq=seq, n_top=n_top),
        grid=(b,),
        in_specs=[
            _smem_spec(),
            pl.BlockSpec((None, N_KV_HEADS, rows, HEAD_DIM), lambda i: (i, 0, 0, 0)),
            pl.BlockSpec((nc, KV_COLS), lambda i: (i, 0)),
            _const_spec(tbl_rt.shape),
        ],
        out_specs=[
            pl.BlockSpec((None, N_KV_HEADS, rows, HEAD_DIM), lambda i: (i, 0, 0, 0)),
            pl.BlockSpec((None, N_KV_HEADS * seq, LANES), lambda i: (i, 0, 0)),
        ],
        out_shape=[
            jax.ShapeDtypeStruct((b, N_KV_HEADS, rows, HEAD_DIM), F32),
            jax.ShapeDtypeStruct((b, N_KV_HEADS * seq, LANES), jnp.int32),
        ],
        compiler_params=_cparams(("arbitrary",)),
        name="sample_cmp",
    )(thr, q_rt, kcn, tbl_rt)


def _norm_k_cols(kt, gain_col):
    ms = jnp.mean(kt * kt, axis=0, keepdims=True)
    return (kt * lax.rsqrt(ms + EPS) * gain_col).astype(BF16)


def _sample_sel_win_kernel(idx_ref, pt_ref, thr_ref, pool_ref, q_ref, ksnew_ref, kwnew_ref,
                           win_ref, tbl_ref, kgs_ref, kgw_ref, os_ref, ow_ref,
                           buf_ref, sem_ref, *, past_len, seq, n_top, page):
    i = pl.program_id(0)
    n_pages = past_len // page
    blocks_per_page = page // SEL_BLOCK
    n_keys = n_top * page
    stride = n_top + 1

    def block_of(g, t, j):
        return idx_ref[(i * N_KV_HEADS * seq + g * seq + t) * stride + j]

    def tile_copy(g, t, j):
        phys = pt_ref[i * n_pages + block_of(g, t, j) // blocks_per_page]
        return pltpu.make_async_copy(
            pool_ref.at[phys, g],
            buf_ref.at[g * seq + t, :, pl.ds(pl.multiple_of(j * page, page), page)],
            sem_ref.at[0])

    def for_all_tiles(fn):
        for g in range(N_KV_HEADS):
            for t in range(seq):
                def per_tile(j, c):
                    fn(tile_copy(g, t, j))
                    return c
                lax.fori_loop(0, n_top, per_tile, 0)

    for_all_tiles(lambda cp: cp.start())

    wb = win_ref.shape[2]
    wk = [_norm_k_cols(win_ref[g, 0:HEAD_DIM, :], kgw_ref[...]) for g in range(N_KV_HEADS)]
    wv = [win_ref[g, HEAD_DIM:GROUP_COLS, :].astype(BF16) for g in range(N_KV_HEADS)]
    j_io = lax.broadcasted_iota(jnp.int32, (1, wb), 1)
    tnew = lax.broadcasted_iota(jnp.int32, (1, SUBLANES), 1)

    def attend(q, tbl_g, k_past, v_past, buckets, mask, knew, bucketsn, maskn):
        s1 = jnp.where(mask, _dot(q, k_past) + _exact_dot(tbl_g, buckets), NEG)
        s2 = jnp.where(maskn, _dot_nt(q, knew[:, :HEAD_DIM].astype(BF16))
                       + _exact_dot(tbl_g, bucketsn), NEG)
        m = jnp.maximum(jnp.max(s1, axis=-1, keepdims=True), jnp.max(s2, axis=-1, keepdims=True))
        e1 = jnp.where(mask, jnp.exp(s1 - m), 0.0)
        e2 = jnp.where(maskn, jnp.exp(s2 - m), 0.0)
        l = jnp.sum(e1, axis=-1, keepdims=True) + jnp.sum(e2, axis=-1, keepdims=True)
        o = _dot_nt(e1.astype(BF16), v_past) + _dot(e2.astype(BF16),
                                                    knew[:, HEAD_DIM:].astype(BF16))
        return o / l

    def window(t, c):
        dist = wb + t - j_io
        maskw = (dist >= 0) & (dist < WINDOW)
        distn = t - tnew
        maskn = (distn >= 0) & (distn < WINDOW)
        buckets = _bucket_onehot(dist, thr_ref)
        bucketsn = _bucket_onehot(distn, thr_ref)
        for g in range(N_KV_HEADS):
            ow_ref[g, t] = attend(q_ref[g, t].astype(BF16), tbl_ref[g], wk[g], wv[g], buckets,
                                  maskw, kwnew_ref[g], bucketsn, maskn)
        return c
    lax.fori_loop(0, seq, window, 0)

    for_all_tiles(lambda cp: cp.wait())

    lane = lax.broadcasted_iota(jnp.int32, (1, n_keys), 1)
    tile_of_lane = lane // page
    row_in_page = lane % page

    def selected(t, c):
        distn = t - tnew
        maskn = distn >= 0
        bucketsn = _bucket_onehot(distn, thr_ref)
        for g in range(N_KV_HEADS):
            kvt = buf_ref[g * seq + t]
            page_base = jnp.zeros((1, n_keys), jnp.int32)
            half = jnp.zeros((1, n_keys), jnp.int32)
            for j in range(n_top):
                blk = block_of(g, t, j)
                page_base = jnp.where(tile_of_lane == j, (blk // blocks_per_page) * page, page_base)
                half = jnp.where(tile_of_lane == j, blk % blocks_per_page, half)
            mask = (row_in_page // SEL_BLOCK) == half
            dist = past_len + t - (page_base + row_in_page)
            os_ref[g, t] = attend(q_ref[g, t].astype(BF16), tbl_ref[g],
                                  _norm_k_cols(kvt[0:HEAD_DIM], kgs_ref[...]),
                                  kvt[HEAD_DIM:GROUP_COLS].astype(BF16),
                                  _bucket_onehot(dist, thr_ref), mask,
                                  ksnew_ref[g], bucketsn, maskn)
        return c
    lax.fori_loop(0, seq, selected, 0)


def _sample_sel_win(idx_flat, pt_flat, thr, pool_t, q_tr, ksnew, kwnew, win_t, tbl_r8,
                    kgs_col, kgw_col, b, past_len, seq):
    ns = past_len // SEL_BLOCK
    n_top = min(N_SELECTED - 1, ns)
    page = pool_t.shape[3]
    wb = win_t.shape[3]
    o_shape = jax.ShapeDtypeStruct((b, N_KV_HEADS, seq, SUBLANES, HEAD_DIM), F32)
    o_spec = pl.BlockSpec((None, N_KV_HEADS, seq, SUBLANES, HEAD_DIM),
                          lambda i, *_: (i, 0, 0, 0, 0))
    new_spec = pl.BlockSpec((None, N_KV_HEADS, SUBLANES, GROUP_COLS), lambda i, *_: (i, 0, 0, 0))
    grid_spec = pltpu.PrefetchScalarGridSpec(
        num_scalar_prefetch=3,
        grid=(b,),
        in_specs=[
            pl.BlockSpec(memory_space=pl.ANY),
            pl.BlockSpec((None, N_KV_HEADS, seq, SUBLANES, HEAD_DIM), lambda i, *_: (i, 0, 0, 0, 0)),
            new_spec, new_spec,
            pl.BlockSpec((None, N_KV_HEADS, GROUP_COLS, wb), lambda i, *_: (i, 0, 0, 0)),
            pl.BlockSpec(tbl_r8.shape, lambda i, *_: (0, 0, 0)),
            pl.BlockSpec(kgs_col.shape, lambda i, *_: (0, 0)),
            pl.BlockSpec(kgw_col.shape, lambda i, *_: (0, 0)),
        ],
        out_specs=[o_spec, o_spec],
        scratch_shapes=[pltpu.VMEM((N_KV_HEADS * seq, GROUP_COLS, n_top * page), F32),
                        pltpu.SemaphoreType.DMA((1,))],
    )
    return pl.pallas_call(
        functools.partial(_sample_sel_win_kernel, past_len=past_len, seq=seq, n_top=n_top,
                          page=page),
        grid_spec=grid_spec,
        out_shape=[o_shape, o_shape],
        compiler_params=_cparams(("arbitrary",)),
        name="sample_sel_win",
    )(idx_flat, pt_flat, thr, pool_t, q_tr, ksnew, kwnew, win_t, tbl_r8, kgs_col, kgw_col)


def _route(f, rhi_ref, rlo_ref):
    f_hi = f.astype(BF16)
    f_lo = (f - f_hi.astype(F32)).astype(BF16)
    logits = (_dot(f_hi, rhi_ref[...]) + _dot(f_lo, rhi_ref[...])) + _dot(f_hi, rlo_ref[...])
    lane = lax.broadcasted_iota(jnp.int32, logits.shape, 1).astype(F32)
    logits = jnp.where(lane < N_EXPERTS, logits, -jnp.inf)
    m1 = jnp.max(logits, axis=-1, keepdims=True)
    i1 = jnp.min(jnp.where(logits == m1, lane, float(LANES)), axis=-1, keepdims=True)
    rest = jnp.where(lane == i1, -jnp.inf, logits)
    m2 = jnp.max(rest, axis=-1, keepdims=True)
    i2 = jnp.min(jnp.where(rest == m2, lane, float(LANES)), axis=-1, keepdims=True)
    e2 = jnp.exp(m2 - m1)
    denom = 1.0 + e2
    return jnp.where(lane == i1, 1.0 / denom, 0.0) + jnp.where(lane == i2, e2 / denom, 0.0)


def _attn_out_kernel(h_ref, o_ref, wout_ref, gf_ref, rhi_ref, rlo_ref, h2_ref, f_ref, gate_ref):
    h2 = h_ref[...] + _dot(o_ref[...], wout_ref[...])
    h2_ref[...] = h2
    f = _rmsnorm(h2, gf_ref[...])
    f_ref[...] = f.astype(f_ref.dtype)
    gate_ref[...] = _route(f, rhi_ref, rlo_ref)


def _attn_out_merge_kernel(h_ref, oc_ref, os_ref, ow_ref, gt_ref, wout_ref, gf_ref, rhi_ref,
                           rlo_ref, h2_ref, f_ref, gate_ref):
    gates = gt_ref[...]
    n_gate = gates.shape[1]
    hpg = HEADS_PER_GROUP
    idx = lax.broadcasted_iota(jnp.int32, (n_gate, D_MODEL), 0)
    head = lax.broadcasted_iota(jnp.int32, (n_gate, D_MODEL), 1) // HEAD_DIM
    o = None
    for c, ref in enumerate((oc_ref, os_ref, ow_ref)):
        lane_of_gate = (head // hpg) * LANES + c * hpg + head % hpg
        gate = _exact_dot(gates, (idx == lane_of_gate).astype(BF16))
        term = gate * ref[...]
        o = term if o is None else o + term
    h2 = h_ref[...] + _dot(o.astype(BF16), wout_ref[...])
    h2_ref[...] = h2
    f = _rmsnorm(h2, gf_ref[...])
    f_ref[...] = f.astype(f_ref.dtype)
    gate_ref[...] = _route(f, rhi_ref, rlo_ref)


def _attn_out(h2d, o_list, gates, wout, gf, rhi, rlo):
    n, d = h2d.shape
    tm = min(TOKEN_TILE, n)
    row = lambda w: pl.BlockSpec((tm, w), lambda i: (i, 0))
    merged = len(o_list) == 1
    body = _attn_out_kernel if merged else _attn_out_merge_kernel
    in_specs = [row(d)] + [row(d)] * len(o_list) + ([] if merged else [row(gates.shape[1])])
    in_specs += [_const_spec(wout.shape), _const_spec((1, d)), _const_spec(rhi.shape),
                 _const_spec(rlo.shape)]
    args = [h2d] + list(o_list) + ([] if merged else [gates]) + [wout, gf, rhi, rlo]
    return pl.pallas_call(
        body,
        grid=(n // tm,),
        in_specs=in_specs,
        out_specs=[row(d), row(d), row(LANES)],
        out_shape=[jax.ShapeDtypeStruct((n, d), F32), jax.ShapeDtypeStruct((n, d), BF16),
                   jax.ShapeDtypeStruct((n, LANES), F32)],
        compiler_params=_cparams(("arbitrary",)),
        name="attn_out_route",
    )(*args)


def _moe_kernel(f_ref, h2_ref, gate_ref, p_ref, wg_ref, wu_ref, wd_ref, gp_ref, wpg_ref, wpp_ref,
                o_ref, acc_ref):
    e = pl.program_id(1)

    @pl.when(e == 0)
    def _():
        acc_ref[...] = jnp.zeros_like(acc_ref)

    f = f_ref[...]
    gate = gate_ref[...]
    lane = lax.broadcasted_iota(jnp.int32, gate.shape, 1)
    ge = jnp.sum(jnp.where(lane == e, gate, 0.0), axis=-1, keepdims=True)
    hid = _silu(_dot(f, wg_ref[...])) * _dot(f, wu_ref[...])
    acc_ref[...] += ge * _dot(hid.astype(BF16), wd_ref[...])

    @pl.when(e == pl.num_programs(1) - 1)
    def _():
        o_ref[...] = _ple(h2_ref[...] + acc_ref[...], p_ref[...], gp_ref[...], wpg_ref, wpp_ref)


def _moe(f2d, h2, gate, p2d, wg, wu, wd, gp, wpg, wpp):
    n, d = h2.shape
    tm = min(TOKEN_TILE, n)
    n_e, _, dff = wg.shape
    row = lambda w: pl.BlockSpec((tm, w), lambda i, e: (i, 0))
    return pl.pallas_call(
        _moe_kernel,
        grid=(n // tm, n_e),
        in_specs=[
            row(d), row(d), row(LANES), row(p2d.shape[1]),
            pl.BlockSpec((None, d, dff), lambda i, e: (e, 0, 0)),
            pl.BlockSpec((None, d, dff), lambda i, e: (e, 0, 0)),
            pl.BlockSpec((None, dff, d), lambda i, e: (e, 0, 0)),
            _const_spec((1, d)), _const_spec(wpg.shape), _const_spec(wpp.shape),
        ],
        out_specs=row(d),
        out_shape=jax.ShapeDtypeStruct((n, d), F32),
        scratch_shapes=[pltpu.VMEM((tm, d), F32)],
        compiler_params=_cparams(("arbitrary", "arbitrary")),
        name="moe_ple",
    )(f2d, h2, gate, p2d, wg, wu, wd, gp, wpg, wpp)


def _bucket_thresholds():
    n = jnp.arange(MAX_DISTANCE + 1)
    max_exact = N_BUCKETS // 2
    nf = jnp.maximum(n, 1).astype(F32)
    large = max_exact + (jnp.log(nf / max_exact) / math.log(MAX_DISTANCE / max_exact)
                         * (N_BUCKETS - max_exact)).astype(jnp.int32)
    bucket = jnp.where(n < max_exact, n, jnp.minimum(large, N_BUCKETS - 1))
    return jnp.sum(bucket[None, :] < jnp.arange(N_BUCKETS)[:, None], axis=1).astype(jnp.int32)


def _row(v):
    return v.reshape(1, -1).astype(F32)


def _key_gain_row(k_gain, cols):
    g = jnp.concatenate([k_gain, jnp.ones_like(k_gain)])
    return jnp.tile(g, cols // GROUP_COLS).reshape(1, cols).astype(F32)


def _rows_on_lanes(x):
    lead = x.shape[:-4]
    rows = x.shape[-4]
    nl = len(lead)
    perm = tuple(range(nl)) + (nl + 1, nl + 2, nl + 3, nl)
    return jnp.transpose(x, perm).reshape(*lead, N_KV_HEADS, GROUP_COLS, rows)


def _rows_on_lanes_inverse(xt):
    b, _, rows = xt.shape
    x = xt.reshape(b, N_KV_HEADS, 2, HEAD_DIM, rows)
    return jnp.transpose(x, (0, 4, 1, 2, 3))[None]


def kernel(x_prompt, x_sample, state_conv, cache_cmp, cache_sel, state_win, page_table,
           p_prompt, p_sample, norm_mix, norm_ffn, norm_ple, conv_w_in, conv_w, conv_w_out,
           nsa_w_in, nsa_w_cmp, nsa_q_norm, nsa_k_norm, nsa_w_out, rel_bias,
           ffn_w_gate, ffn_w_up, ffn_w_down, moe_router, moe_w_gate, moe_w_up, moe_w_down,
           ple_w_proj, ple_w_gate):
    bp, sp, d = x_prompt.shape
    bs, ss, _ = x_sample.shape
    n_p = bp * sp
    n_s = bs * ss
    page = cache_cmp.shape[2]
    past_len = page_table.shape[1] * page
    hpg = HEADS_PER_GROUP
    bf = lambda w: w.astype(BF16)

    cw = jnp.zeros((SUBLANES, d), F32).at[:CONV_WIDTH].set(conv_w[0])
    w_in0, w_out0 = bf(conv_w_in[0]), bf(conv_w_out[0])
    g_mix0 = _row(norm_mix[0])
    h_p, tail_p = _mix0_prompt(x_prompt, g_mix0, w_in0, cw, w_out0)
    conv_prompt = tail_p[:, SUBLANES - (CONV_WIDTH - 1):][None]

    st = state_conv[0]
    zeros = jnp.zeros((bs, ss, d), F32)
    s1 = zeros.at[:, 0].set(st[:, 1]).reshape(n_s, d)
    s2 = zeros.at[:, 0].set(st[:, 0]).at[:, 1].set(st[:, 1]).reshape(n_s, d)
    h_s, u_s = _mix0_sample(x_sample.reshape(n_s, d), g_mix0, w_in0, cw, w_out0, s1, s2, ss)
    conv_sample = u_s.reshape(bs, ss, d)[:, ss - (CONV_WIDTH - 1):][None]

    ffn0_w = (_row(norm_ffn[0]), bf(ffn_w_gate[0]), bf(ffn_w_up[0]), bf(ffn_w_down[0]),
              _row(norm_ple[0]), bf(ple_w_gate[0]), bf(ple_w_proj[0]))
    h_p = _ffn0(h_p.reshape(n_p, d), p_prompt[0].reshape(n_p, -1), *ffn0_w)
    h_s = _ffn0(h_s, p_sample[0].reshape(n_s, -1), *ffn0_w)

    q_cols = N_HEADS * HEAD_DIM
    kv_cols = N_BRANCHES * KV_COLS
    w_in1 = nsa_w_in[0]
    wq = bf(w_in1[:, :q_cols])
    wkv = w_in1[:, q_cols:q_cols + kv_cols]
    wkc = bf(wkv[:, :KV_COLS])
    wkvt = bf(wkv.T)
    wv_src = wkv[:, KV_COLS:].reshape(d, 2 * N_KV_HEADS, 2, HEAD_DIM)[:, :, 1]
    wv = bf(jnp.zeros((d, 2 * N_KV_HEADS, LANES), F32).at[:, :, :HEAD_DIM].set(wv_src).reshape(
        d, 2 * KV_COLS))
    wg_src = w_in1[:, q_cols + kv_cols:].reshape(d, N_BRANCHES, N_KV_HEADS, hpg)
    wgt = jnp.zeros((d, N_KV_HEADS, LANES), F32).at[:, :, :N_BRANCHES * hpg].set(
        wg_src.transpose(0, 2, 1, 3).reshape(d, N_KV_HEADS, N_BRANCHES * hpg))
    wgt = bf(wgt.reshape(d, N_KV_HEADS * LANES))
    gi = jnp.arange(MXU_DIM) // HEAD_DIM
    gmat = (gi[:, None] == gi[None, :]).astype(BF16)
    qg = jnp.tile(nsa_q_norm[0], N_HEADS).reshape(1, q_cols).astype(F32)
    kgc = _key_gain_row(nsa_k_norm[0, 0], KV_COLS)
    kgs_col = nsa_k_norm[0, 1].reshape(HEAD_DIM, 1).astype(F32)
    kgw_col = nsa_k_norm[0, 2].reshape(HEAD_DIM, 1).astype(F32)
    proj_w = (_row(norm_mix[1]), wq, wkc, wkvt, wv, wgt, gmat, qg, kgs_col, kgw_col)

    per_tile = MXU_DIM // HEAD_DIM
    wc = nsa_w_cmp[0].reshape(CMP_BLOCK, 2, per_tile, HEAD_DIM, HEAD_DIM)
    w2 = bf(jnp.einsum('lhade,ab->lhadbe', wc, jnp.eye(per_tile, dtype=F32)).reshape(
        CMP_BLOCK, 2, MXU_DIM, MXU_DIM))

    thr = _bucket_thresholds()
    tbl = rel_bias.astype(F32)

    q_p, kvc_p, kvct_p, kvst_p, kvwt_p, ksn_p, kwn_p, vr_p, gt_p = _nsa_proj(h_p, bp, *proj_w)
    kcn_p = _compress_prompt(kvc_p, w2, gmat, kgc)
    o_p = _nsa_prompt_attn(tbl, thr, q_p, kcn_p, ksn_p, kwn_p, vr_p, gt_p, bp, sp)

    q_s, _, kvct_s, kvst_s, kvwt_s, ksn_s, kwn_s, _, gt_s = _nsa_proj(h_s, 1, *proj_w)
    pt_flat = page_table.reshape(-1).astype(jnp.int32)
    kcn_s = _compress_paged(pt_flat, _rows_on_lanes(cache_cmp[0]), w2, gmat, kgc, PAGES_PER_STEP)
    q5 = q_s.astype(F32).reshape(bs, ss, N_KV_HEADS, hpg, HEAD_DIM)
    q_rt = q5.transpose(0, 2, 3, 1, 4).reshape(bs, N_KV_HEADS, hpg * ss, HEAD_DIM)
    q_tr = jnp.zeros((bs, N_KV_HEADS, ss, SUBLANES, HEAD_DIM), F32).at[:, :, :, :hpg].set(
        q5.transpose(0, 2, 1, 3, 4))
    tbl_gr = tbl.T.reshape(N_KV_HEADS, hpg, N_BUCKETS)
    tbl_rt = jnp.repeat(tbl_gr, ss, axis=1)
    tbl_r8 = jnp.zeros((N_KV_HEADS, SUBLANES, N_BUCKETS), F32).at[:, :hpg].set(tbl_gr)
    oc_s, idx = _sample_cmp(thr, q_rt, kcn_s, tbl_rt, bs, past_len, ss)
    n_top_s = min(N_SELECTED - 1, past_len // SEL_BLOCK)
    idx_flat = idx[:, :, :n_top_s + 1].reshape(-1)

    def new_rows(kn_t):
        r = kn_t[0].astype(F32).reshape(N_KV_HEADS, GROUP_COLS, bs, ss).transpose(2, 0, 3, 1)
        return jnp.zeros((bs, N_KV_HEADS, SUBLANES, GROUP_COLS), F32).at[:, :, :ss].set(r)

    win_t = _rows_on_lanes(state_win[0])
    os_s, ow_s = _sample_sel_win(idx_flat, pt_flat, thr, _rows_on_lanes(cache_sel[0]), q_tr,
                                 new_rows(ksn_s), new_rows(kwn_s), win_t, tbl_r8,
                                 kgs_col, kgw_col, bs, past_len, ss)
    oc_s2 = oc_s.reshape(bs, N_KV_HEADS, hpg, ss, HEAD_DIM).transpose(0, 3, 1, 2, 4).reshape(n_s, d)
    to2d = lambda o: o[:, :, :, :hpg].transpose(0, 2, 1, 3, 4).reshape(n_s, d)

    w_out1 = bf(nsa_w_out[0])
    router = jnp.zeros((d, LANES), F32).at[:, :N_EXPERTS].set(moe_router[0])
    r_hi = bf(router)
    r_lo = bf(router - r_hi.astype(F32))
    gf1 = _row(norm_ffn[1])
    moe_w = (bf(moe_w_gate[0]), bf(moe_w_up[0]), bf(moe_w_down[0]), _row(norm_ple[1]),
             bf(ple_w_gate[1]), bf(ple_w_proj[1]))
    h2_p, f_p, gate_p = _attn_out(h_p, [o_p], None, w_out1, gf1, r_hi, r_lo)
    y_p = _moe(f_p, h2_p, gate_p, p_prompt[1].reshape(n_p, -1), *moe_w)
    h2_s, f_s, gate_s = _attn_out(h_s, [oc_s2, to2d(os_s), to2d(ow_s)], gt_s, w_out1, gf1,
                                  r_hi, r_lo)
    y_s = _moe(f_s, h2_s, gate_s, p_sample[1].reshape(n_s, -1), *moe_w)

    sample_rows = lambda xt: _rows_on_lanes_inverse(
        xt[0].reshape(KV_COLS, bs, ss).transpose(1, 0, 2))
    wbp = min(WINDOW, sp)
    win_s_t = jnp.concatenate(
        [win_t, kvwt_s[0].reshape(N_KV_HEADS, GROUP_COLS, bs, ss).transpose(2, 0, 1, 3)],
        axis=3)[..., ss:]
    return (y_p.reshape(bp, sp, d), y_s.reshape(bs, ss, d), conv_prompt, conv_sample,
            _rows_on_lanes_inverse(kvct_p), sample_rows(kvct_s),
            _rows_on_lanes_inverse(kvst_p), sample_rows(kvst_s),
            _rows_on_lanes_inverse(kvwt_p[:, :, sp - wbp:]),
            _rows_on_lanes_inverse(win_s_t.reshape(bs, KV_COLS, -1)))
```

```python
import functools
import math

import jax
import jax.numpy as jnp
from jax import lax
from jax.experimental import pallas as pl
from jax.experimental.pallas import tpu as pltpu

F32 = jnp.float32
BF16 = jnp.bfloat16

D_MODEL = 1024
N_HEADS = 16
HEAD_DIM = 64
N_KV_HEADS = 4
HEADS_PER_GROUP = 4
N_BRANCHES = 3
CMP_BLOCK = 32
SEL_BLOCK = 64
CMP_PER_SEL = SEL_BLOCK // CMP_BLOCK
N_SELECTED = 16
WINDOW = 512
N_BUCKETS = 32
MAX_DISTANCE = 128
N_EXPERTS = 8
CONV_WIDTH = 3
EPS = 1e-6
FORCED_SCORE = 1e4
NEG = -1e30
KV_COLS = N_KV_HEADS * 2 * HEAD_DIM
GROUP_COLS = 2 * HEAD_DIM
Q_GROUP_COLS = HEADS_PER_GROUP * HEAD_DIM

LANES = 128
SUBLANES = 8
MXU_DIM = 256
VMEM_LIMIT = 56 * 1024 * 1024

TOKEN_TILE = 512
ATTN_TILE = 256
PAGES_PER_STEP = 32


def _cparams(sem, vmem=VMEM_LIMIT):
    return pltpu.CompilerParams(dimension_semantics=sem, vmem_limit_bytes=vmem)


def _const_spec(shape):
    nd = len(shape)
    return pl.BlockSpec(shape, lambda *_: (0,) * nd, pipeline_mode=pl.Buffered(1))


def _smem_spec():
    return pl.BlockSpec(memory_space=pltpu.SMEM)


def _dot(a, b):
    return jnp.dot(a, b, preferred_element_type=F32)


def _dot_nt(a, b):
    return lax.dot_general(a, b, (((1,), (1,)), ((), ())), preferred_element_type=F32)


def _split3(x):
    hi = x.astype(BF16)
    r1 = x - hi.astype(F32)
    mid = r1.astype(BF16)
    lo = (r1 - mid.astype(F32)).astype(BF16)
    return hi, mid, lo


def _exact_dot(x, m01):
    hi, mid, lo = _split3(x)
    return (_dot(hi, m01) + _dot(mid, m01)) + _dot(lo, m01)


def _rmsnorm(x, g):
    ms = jnp.mean(x * x, axis=-1, keepdims=True)
    return x * lax.rsqrt(ms + EPS) * g


def _group_mean_sq(x, gmat):
    n = x.shape[1]
    w = gmat.shape[0]
    outs = []
    for c in range(n // w):
        blk = x[:, c * w:(c + 1) * w]
        sq = blk * blk
        hi = sq.astype(BF16)
        lo = (sq - hi.astype(F32)).astype(BF16)
        outs.append(_dot(hi, gmat) + _dot(lo, gmat))
    out = outs[0] if len(outs) == 1 else jnp.concatenate(outs, axis=1)
    return out * (1.0 / HEAD_DIM)


def _norm_keys(kv, gain_row, gmat):
    ms = _group_mean_sq(kv, gmat)
    lane = lax.broadcasted_iota(jnp.int32, kv.shape, 1)
    is_k = (lane % GROUP_COLS) < HEAD_DIM
    return jnp.where(is_k, kv * lax.rsqrt(ms + EPS) * gain_row, kv)


def _bucket_bias(dist, thr_ref, value_of_bucket):
    val = value_of_bucket(0)
    val = jnp.broadcast_to(val, dist.shape).astype(F32)
    for k in range(1, N_BUCKETS):
        val = jnp.where(dist >= thr_ref[k], value_of_bucket(k), val)
    return val


def _silu(x):
    return x * jax.nn.sigmoid(x)


def _ple(h, p, gain, wpg_ref, wpp_ref):
    r = _rmsnorm(h, gain).astype(BF16)
    g = jax.nn.sigmoid(_dot(r, wpg_ref[...]))
    return h + g * _dot(p.astype(BF16), wpp_ref[...])


def _conv_mix_tail(x, gb, u, um1, um2, cw_ref, wout_ref):
    conv = cw_ref[0:1, :] * um2 + cw_ref[1:2, :] * um1 + cw_ref[2:3, :] * u
    y = _dot((gb * conv).astype(BF16), wout_ref[...])
    return x + y


def _mix0_prompt_kernel(x_ref, g_ref, win_ref, cw_ref, wout_ref, h_ref, st_ref, carry_ref):
    j = pl.program_id(1)
    tm = x_ref.shape[0]

    @pl.when(j == 0)
    def _():
        carry_ref[...] = jnp.zeros_like(carry_ref)

    x = x_ref[...]
    a = _rmsnorm(x, g_ref[...]).astype(BF16)
    proj = _dot(a, win_ref[...])
    gb = proj[:, :D_MODEL]
    u = proj[:, D_MODEL:2 * D_MODEL] * proj[:, 2 * D_MODEL:]
    c0 = carry_ref[SUBLANES - 2:SUBLANES - 1, :]
    c1 = carry_ref[SUBLANES - 1:SUBLANES, :]
    row = lax.broadcasted_iota(jnp.int32, u.shape, 0)
    um1 = jnp.where(row == 0, c1, pltpu.roll(u, 1, 0))
    um2 = jnp.where(row == 0, c0, jnp.where(row == 1, c1, pltpu.roll(u, 2, 0)))
    h_ref[...] = _conv_mix_tail(x, gb, u, um1, um2, cw_ref, wout_ref)
    tail = u[tm - SUBLANES:, :]
    carry_ref[...] = tail
    st_ref[...] = tail


def _mix0_sample_kernel(x_ref, g_ref, win_ref, cw_ref, wout_ref, s1_ref, s2_ref, h_ref, u_ref,
                        *, seq):
    x = x_ref[...]
    a = _rmsnorm(x, g_ref[...]).astype(BF16)
    proj = _dot(a, win_ref[...])
    gb = proj[:, :D_MODEL]
    u = proj[:, D_MODEL:2 * D_MODEL] * proj[:, 2 * D_MODEL:]
    t = lax.broadcasted_iota(jnp.int32, u.shape, 0) % seq
    um1 = jnp.where(t >= 1, pltpu.roll(u, 1, 0), s1_ref[...])
    um2 = jnp.where(t >= 2, pltpu.roll(u, 2, 0), s2_ref[...])
    h_ref[...] = _conv_mix_tail(x, gb, u, um1, um2, cw_ref, wout_ref)
    u_ref[...] = u


def _mix0_prompt(x, gain, w_in, cw, w_out):
    b, s, d = x.shape
    tm = min(TOKEN_TILE, s)
    grid = (b, s // tm)
    return pl.pallas_call(
        _mix0_prompt_kernel,
        grid=grid,
        in_specs=[
            pl.BlockSpec((None, tm, d), lambda i, j: (i, j, 0)),
            _const_spec((1, d)),
            _const_spec(w_in.shape),
            _const_spec(cw.shape),
            _const_spec(w_out.shape),
        ],
        out_specs=[
            pl.BlockSpec((None, tm, d), lambda i, j: (i, j, 0)),
            pl.BlockSpec((None, SUBLANES, d), lambda i, j: (i, 0, 0)),
        ],
        out_shape=[
            jax.ShapeDtypeStruct((b, s, d), F32),
            jax.ShapeDtypeStruct((b, SUBLANES, d), F32),
        ],
        scratch_shapes=[pltpu.VMEM((SUBLANES, d), F32)],
        compiler_params=_cparams(("arbitrary", "arbitrary")),
        name="mix0_prompt",
    )(x, gain, w_in, cw, w_out)


def _mix0_sample(x2d, gain, w_in, cw, w_out, s1, s2, seq):
    n, d = x2d.shape
    return pl.pallas_call(
        functools.partial(_mix0_sample_kernel, seq=seq),
        out_shape=[jax.ShapeDtypeStruct((n, d), F32), jax.ShapeDtypeStruct((n, d), F32)],
        compiler_params=_cparams(None),
        name="mix0_sample",
    )(x2d, gain, w_in, cw, w_out, s1, s2)


def _ffn0_kernel(h_ref, p_ref, gf_ref, wg_ref, wu_ref, wd_ref, gp_ref, wpg_ref, wpp_ref, o_ref,
                 *, chunk):
    h = h_ref[...]
    f = _rmsnorm(h, gf_ref[...]).astype(BF16)
    d_ff = wg_ref.shape[1]
    acc = None
    for c in range(d_ff // chunk):
        sl = slice(c * chunk, (c + 1) * chunk)
        hid = _silu(_dot(f, wg_ref[:, sl])) * _dot(f, wu_ref[:, sl])
        part = _dot(hid.astype(BF16), wd_ref[sl, :])
        acc = part if acc is None else acc + part
    o_ref[...] = _ple(h + acc, p_ref[...], gp_ref[...], wpg_ref, wpp_ref)


def _ffn0(h2d, p2d, gf, wg, wu, wd, gp, wpg, wpp):
    n, d = h2d.shape
    tm = min(TOKEN_TILE, n)
    d_ff = wg.shape[1]
    chunk = d_ff // 2
    assert chunk % LANES == 0
    return pl.pallas_call(
        functools.partial(_ffn0_kernel, chunk=chunk),
        grid=(n // tm,),
        in_specs=[
            pl.BlockSpec((tm, d), lambda i: (i, 0)),
            pl.BlockSpec((tm, p2d.shape[1]), lambda i: (i, 0)),
            _const_spec((1, d)),
            _const_spec(wg.shape), _const_spec(wu.shape), _const_spec(wd.shape),
            _const_spec((1, d)),
            _const_spec(wpg.shape), _const_spec(wpp.shape),
        ],
        out_specs=pl.BlockSpec((tm, d), lambda i: (i, 0)),
        out_shape=jax.ShapeDtypeStruct((n, d), F32),
        compiler_params=_cparams(("arbitrary",)),
        name="ffn0_ple",
    )(h2d, p2d, gf, wg, wu, wd, gp, wpg, wpp)


def _norm_keys_t(kvt, gain_col):
    parts = []
    for gi in range(KV_COLS // HEAD_DIM):
        x = kvt[gi * HEAD_DIM:(gi + 1) * HEAD_DIM, :]
        if gi % 2 == 0:
            ms = jnp.mean(x * x, axis=0, keepdims=True)
            x = x * lax.rsqrt(ms + EPS) * gain_col
        parts.append(x)
    return jnp.concatenate(parts, axis=0)


def _nsa_proj_kernel(h_ref, g_ref, wq_ref, wkc_ref, wkvt_ref, wv_ref, wgt_ref, gm_ref, qg_ref,
                     kgs_ref, kgw_ref, q_ref, kvc_ref, kvct_ref, kvst_ref, kvwt_ref, ksn_ref,
                     kwn_ref, vr_ref, gt_ref):
    a = _rmsnorm(h_ref[...], g_ref[...]).astype(BF16)
    vr = _dot(a, wv_ref[...])
    lane = lax.broadcasted_iota(jnp.int32, vr.shape, 1)
    vr_ref[...] = jnp.where(lane % LANES < HEAD_DIM, vr, 1.0).astype(vr_ref.dtype)
    q = _dot(a, wq_ref[...])
    ms = _group_mean_sq(q, gm_ref[...])
    q_ref[...] = ((q * lax.rsqrt(ms + EPS) * qg_ref[...]) * (HEAD_DIM ** -0.5)).astype(q_ref.dtype)
    kvc_ref[...] = _dot(a, wkc_ref[...])
    kvt = _dot_nt(wkvt_ref[...], a)
    kvst = kvt[KV_COLS:2 * KV_COLS]
    kvwt = kvt[2 * KV_COLS:]
    kvct_ref[...] = kvt[:KV_COLS]
    kvst_ref[...] = kvst
    kvwt_ref[...] = kvwt
    ksn_ref[...] = _norm_keys_t(kvst, kgs_ref[...]).astype(ksn_ref.dtype)
    kwn_ref[...] = _norm_keys_t(kvwt, kgw_ref[...]).astype(kwn_ref.dtype)
    gt_ref[...] = jax.nn.sigmoid(_dot(a, wgt_ref[...]))


def _nsa_proj(h2d, n_seq, gain, wq, wkc, wkvt, wv, wgt, gmat, qg, kgs_col, kgw_col):
    n, d = h2d.shape
    s = n // n_seq
    tm = min(TOKEN_TILE, s)
    tps = s // tm
    row = lambda w: pl.BlockSpec((tm, w), lambda i: (i, 0))
    col = pl.BlockSpec((None, KV_COLS, tm), lambda i: (i // tps, 0, i % tps))
    tshape = lambda dt: jax.ShapeDtypeStruct((n_seq, KV_COLS, s), dt)
    n_gate_cols = wgt.shape[1]
    return pl.pallas_call(
        _nsa_proj_kernel,
        grid=(n // tm,),
        in_specs=[
            row(d), _const_spec((1, d)),
            _const_spec(wq.shape), _const_spec(wkc.shape), _const_spec(wkvt.shape),
            _const_spec(wv.shape), _const_spec(wgt.shape), _const_spec(gmat.shape),
            _const_spec(qg.shape), _const_spec(kgs_col.shape), _const_spec(kgw_col.shape),
        ],
        out_specs=[row(d), row(KV_COLS), col, col, col, col, col, row(wv.shape[1]),
                   row(n_gate_cols)],
        out_shape=[
            jax.ShapeDtypeStruct((n, d), BF16),
            jax.ShapeDtypeStruct((n, KV_COLS), F32),
            tshape(F32), tshape(F32), tshape(F32), tshape(BF16), tshape(BF16),
            jax.ShapeDtypeStruct((n, wv.shape[1]), BF16),
            jax.ShapeDtypeStruct((n, n_gate_cols), F32),
        ],
        compiler_params=_cparams(("arbitrary",)),
        name="nsa_proj",
    )(h2d, gain, wq, wkc, wkvt, wv, wgt, gmat, qg, kgs_col, kgw_col)


SLABS = KV_COLS // LANES


def _compress_rows(read_slab, n_blocks, w_ref, gmat, kg_row):
    half = KV_COLS // 2
    per_half = SLABS // 2
    acc = [jnp.zeros((n_blocks, half), F32) for _ in range(2)]
    for l in range(CMP_BLOCK):
        for hf in range(2):
            xl = jnp.concatenate([read_slab(l, hf * per_half + c) for c in range(per_half)], axis=1)
            acc[hf] = acc[hf] + _dot(xl.astype(BF16), w_ref[l, hf])
    kv = jnp.concatenate(acc, axis=1)
    return _norm_keys(kv, kg_row, gmat).astype(BF16)


def _compress_kernel(x_ref, w_ref, gm_ref, kg_ref, o_ref):
    nb = o_ref.shape[0]
    read = lambda l, g: x_ref[pl.ds(l * SLABS + g, nb, stride=CMP_BLOCK * SLABS), :]
    o_ref[...] = _compress_rows(read, nb, w_ref, gm_ref[...], kg_ref[...])


def _compress_prompt(rows2d, w2, gmat, kg):
    n = rows2d.shape[0]
    tr = min(4096, n)
    nb = tr // CMP_BLOCK
    return pl.pallas_call(
        _compress_kernel,
        grid=(n // tr,),
        in_specs=[pl.BlockSpec((tr * SLABS, LANES), lambda i: (i, 0)),
                  _const_spec(w2.shape), _const_spec(gmat.shape), _const_spec(kg.shape)],
        out_specs=pl.BlockSpec((nb, KV_COLS), lambda i: (i, 0)),
        out_shape=jax.ShapeDtypeStruct((n // CMP_BLOCK, KV_COLS), BF16),
        compiler_params=_cparams(("arbitrary",)),
        name="compress_prompt",
    )(rows2d.reshape(n * SLABS, LANES), w2, gmat, kg)


def _compress_paged_kernel(pt_ref, pool_ref, w_ref, gm_ref, kg_ref, o_ref, buf_ref, st_ref, sem_ref,
                           *, pages_per_step, page):
    i = pl.program_id(0)
    n = pl.num_programs(0)

    def page_copy(step, p, slot):
        phys = pt_ref[step * pages_per_step + p]
        return pltpu.make_async_copy(pool_ref.at[phys], buf_ref.at[slot, p], sem_ref.at[slot])

    def start(step, slot):
        def issue(p, c):
            page_copy(step, p, slot).start()
            return c
        lax.fori_loop(0, pages_per_step, issue, 0)

    @pl.when(i == 0)
    def _():
        start(0, 0)

    @pl.when(i + 1 < n)
    def _():
        start(i + 1, (i + 1) % 2)

    slot = i % 2

    def wait(p, c):
        page_copy(i, p, slot).wait()
        return c
    lax.fori_loop(0, pages_per_step, wait, 0)

    blocks_per_page = page // CMP_BLOCK
    group = 2 * blocks_per_page
    assert group == SUBLANES
    ri = lax.broadcasted_iota(jnp.int32, (2 * page, 2 * page), 0)
    ci = lax.broadcasted_iota(jnp.int32, (2 * page, 2 * page), 1)
    src_lane = ((ri % group) // blocks_per_page) * page + (ri % blocks_per_page) * CMP_BLOCK + ri // group
    perm = (ci == src_lane).astype(BF16)

    def to_rows(pp, c):
        for g0 in range(0, N_KV_HEADS, 2):
            pair = jnp.concatenate([buf_ref[slot, 2 * pp, g0:g0 + 2].reshape(2 * GROUP_COLS, page),
                                    buf_ref[slot, 2 * pp + 1, g0:g0 + 2].reshape(2 * GROUP_COLS, page)],
                                   axis=1).astype(BF16)
            rows = _dot_nt(perm, pair)
            for l in range(CMP_BLOCK):
                dst = pl.ds(pl.multiple_of(pp * group, group), group)
                blk = rows[l * group:(l + 1) * group]
                st_ref[g0, l, dst, :] = blk[:, :GROUP_COLS]
                st_ref[g0 + 1, l, dst, :] = blk[:, GROUP_COLS:]
        return c
    lax.fori_loop(0, pages_per_step // 2, to_rows, 0)

    nb = pages_per_step * page // CMP_BLOCK
    read = lambda l, g: st_ref[g, l]
    o_ref[...] = _compress_rows(read, nb, w_ref, gm_ref[...], kg_ref[...])


def _compress_paged(page_table_flat, pool_t, w2, gmat, kg, pages_per_step):
    n_pages = page_table_flat.shape[0]
    page = pool_t.shape[3]
    assert n_pages % pages_per_step == 0 and page == LANES and pool_t.shape[2] == GROUP_COLS
    steps = n_pages // pages_per_step
    nb = pages_per_step * page // CMP_BLOCK
    grid_spec = pltpu.PrefetchScalarGridSpec(
        num_scalar_prefetch=1,
        grid=(steps,),
        in_specs=[pl.BlockSpec(memory_space=pl.ANY),
                  pl.BlockSpec(w2.shape, lambda i, pt: (0, 0, 0, 0), pipeline_mode=pl.Buffered(1)),
                  pl.BlockSpec(gmat.shape, lambda i, pt: (0, 0)),
                  pl.BlockSpec(kg.shape, lambda i, pt: (0, 0))],
        out_specs=pl.BlockSpec((nb, KV_COLS), lambda i, pt: (i, 0)),
        scratch_shapes=[pltpu.VMEM((2, pages_per_step, N_KV_HEADS, GROUP_COLS, page), F32),
                        pltpu.VMEM((N_KV_HEADS, CMP_BLOCK, nb, GROUP_COLS), F32),
                        pltpu.SemaphoreType.DMA((2,))],
    )
    return pl.pallas_call(
        functools.partial(_compress_paged_kernel, pages_per_step=pages_per_step, page=page),
        grid_spec=grid_spec,
        out_shape=jax.ShapeDtypeStruct((steps * nb, KV_COLS), BF16),
        compiler_params=_cparams(("arbitrary",)),
        name="compress_paged",
    )(page_table_flat, pool_t, w2, gmat, kg)


FLASH_ROWS = 128


def _flash_tile(qh_ref, kt_ref, v_ref, kt, tk, state, terms):
    m_ref, acc_ref = state
    start = pl.multiple_of(kt * tk, tk)
    k = kt_ref[:, pl.ds(start, tk)]
    v = v_ref[pl.ds(start, tk), :]
    tq = qh_ref.shape[1]
    blocks = [(r, pl.ds(rb * FLASH_ROWS, FLASH_ROWS))
              for r in range(HEADS_PER_GROUP) for rb in range(tq // FLASH_ROWS)]
    old = [(m_ref[r, rows, :], acc_ref[r, rows, :]) for r, rows in blocks]
    new = []
    for (r, rows), (m_old, acc_old) in zip(blocks, old):
        s = _dot(qh_ref[r, rows, :], k)
        for term in terms(r, rows):
            s = s + term
        m_new = jnp.maximum(m_old, jnp.max(s, axis=-1, keepdims=True))
        e = jnp.exp(s - jnp.concatenate([m_new] * (tk // LANES), axis=1))
        new.append((m_new, jnp.exp(m_old - m_new) * acc_old + _dot(e.astype(BF16), v)))
    for (r, rows), (m_new, acc_new) in zip(blocks, new):
        m_ref[r, rows, :] = m_new
        acc_ref[r, rows, :] = acc_new


def _nsa_prompt_kernel(tbl_ref, thr_ref, q_ref, kc_ref, ks_ref, vs_ref, kw_ref, vw_ref, gt_ref, o_ref,
                       bias_ref, edge_ref, cbias_ref, amask_ref, qh_ref, m_ref, acc_ref, *, n_top):
    g = pl.program_id(0)
    b = pl.program_id(1)
    qi = pl.program_id(2)
    tq = q_ref.shape[0]
    tk = tq
    hpg = HEADS_PER_GROUP
    nc = kc_ref.shape[0]
    ns = nc // CMP_PER_SEL
    blocks_per_tile = tk // SEL_BLOCK
    far = N_BUCKETS - 1

    qpos = qi * tq + lax.broadcasted_iota(jnp.int32, (tq, 1), 0)
    n_io = lax.broadcasted_iota(jnp.int32, (tq, nc), 1)
    relc = qpos - ((n_io + 1) * CMP_BLOCK - 1)

    n_back = WINDOW // tk

    @pl.when((b == 0) & (qi == 0))
    def _():
        diag = (lax.broadcasted_iota(jnp.int32, (tq, tk), 0)
                - lax.broadcasted_iota(jnp.int32, (tq, tk), 1))
        for r in range(hpg):
            h = g * hpg + r
            shifted = lambda k: tbl_ref[k, h] - tbl_ref[far, h]
            bias_ref[r, 0] = jnp.where(diag >= 0, _bucket_bias(diag, thr_ref, shifted), NEG)
            bias_ref[r, 1] = _bucket_bias(diag + tk, thr_ref, shifted)
        edge_ref[...] = jnp.where(diag + n_back * tk < WINDOW, 0.0, NEG)

    @pl.when(b == 0)
    def _():
        for r in range(hpg):
            h = g * hpg + r
            cbias_ref[r, qi] = _bucket_bias(relc, thr_ref, lambda k: tbl_ref[k, h])

    q = q_ref[...]
    for r in range(hpg):
        qh_ref[r] = q[:, r * HEAD_DIM:(r + 1) * HEAD_DIM]

    sel_state = (m_ref.at[0], acc_ref.at[0])
    win_state = (m_ref.at[1], acc_ref.at[1])
    m_ref[...] = jnp.full(m_ref.shape, NEG, F32)
    acc_ref[...] = jnp.zeros(acc_ref.shape, F32)

    def finish(branch):
        out = []
        for r in range(hpg):
            acc = acc_ref[branch, r]
            out.append((acc / pltpu.roll(acc, HEAD_DIM, 1))[:, :HEAD_DIM])
        return out

    def win_tile(kt, terms):
        _flash_tile(qh_ref, kw_ref, vw_ref, kt, tk, win_state, terms)

    win_tile(qi, lambda r, rows: (bias_ref[r, 0, rows, :],))
    no_prev = jnp.where(qi >= 1, 0.0, NEG)
    win_tile(jnp.maximum(qi - 1, 0), lambda r, rows: (bias_ref[r, 1, rows, :], no_prev))

    kc = kc_ref[...]
    kck = kc[:, :HEAD_DIM]
    kcv = kc[:, HEAD_DIM:]
    maskc = relc >= 0
    imp = None
    oc = []
    for r in range(hpg):
        s = _dot_nt(q[:, r * HEAD_DIM:(r + 1) * HEAD_DIM], kck) + cbias_ref[r, qi]
        sm = jnp.where(maskc, s, NEG)
        e = jnp.exp(sm - jnp.max(sm, axis=-1, keepdims=True))
        p = jnp.where(maskc, e / jnp.sum(e, axis=-1, keepdims=True), 0.0)
        imp = p if imp is None else imp + p
        oc.append(_dot(p.astype(BF16), kcv))

    pair_t = (lax.broadcasted_iota(jnp.int32, (ns, nc), 0)
              == lax.broadcasted_iota(jnp.int32, (ns, nc), 1) // CMP_PER_SEL).astype(BF16)
    hi, mid, lo = _split3(imp)
    imp_sel = (_dot_nt(pair_t, hi) + _dot_nt(pair_t, mid)) + _dot_nt(pair_t, lo)
    blk = lax.broadcasted_iota(jnp.int32, (ns, tq), 0)
    cur = (qi * tq + lax.broadcasted_iota(jnp.int32, (1, tq), 1)) // SEL_BLOCK
    forced = (blk == 0) | (blk == cur) | (blk == cur - 1)
    score = jnp.where(forced, FORCED_SCORE, jnp.where(blk <= cur, imp_sel, -1.0))
    rank = jnp.zeros((ns, tq), F32)
    for i in range(ns):
        si = score[i:i + 1, :]
        beats = (si > score) | ((si == score) & (blk > i))
        rank = rank + jnp.where(beats, 1.0, 0.0)
    sel_t = jnp.where(rank < n_top, 1.0, 0.0).astype(BF16)
    eye = (lax.broadcasted_iota(jnp.int32, (tq, tq), 0)
           == lax.broadcasted_iota(jnp.int32, (tq, tq), 1)).astype(BF16)
    sel = _dot_nt(eye, sel_t).astype(BF16)

    def sel_tile(kt, tiles_back):
        expand = (lax.broadcasted_iota(jnp.int32, (ns, tk), 0)
                  == kt * blocks_per_tile + lax.broadcasted_iota(jnp.int32, (ns, tk), 1) // SEL_BLOCK)
        amask_ref[...] = jnp.where(_dot(sel, expand.astype(BF16)) > 0.5, 0.0, NEG)
        if tiles_back is None:
            terms = lambda r, rows: (amask_ref[rows, :],)
        else:
            terms = lambda r, rows: (bias_ref[r, tiles_back, rows, :], amask_ref[rows, :])
        _flash_tile(qh_ref, ks_ref, vs_ref, kt, tk, sel_state, terms)

    def far_tile(kt, c):
        sel_tile(kt, None)
        return c
    lax.fori_loop(0, jnp.maximum(qi - 1, 0), far_tile, 0)

    @pl.when(qi >= n_back)
    def _():
        sel_tile(qi - 1, 1)
        win_tile(qi - n_back, lambda r, rows: (edge_ref[rows, :],))

    @pl.when((qi >= 1) & (qi < n_back))
    def _():
        sel_tile(qi - 1, 1)

    sel_tile(qi, 0)
    o_sel = finish(0)
    o_win = finish(1)

    width = hpg * HEAD_DIM
    idx = lax.broadcasted_iota(jnp.int32, (LANES, N_BRANCHES * width), 0)
    col = lax.broadcasted_iota(jnp.int32, (LANES, N_BRANCHES * width), 1)
    spread = (idx == (col // width) * hpg + (col % width) // HEAD_DIM).astype(BF16)
    gates = _exact_dot(gt_ref[...], spread)
    out = None
    for c, branch in enumerate((oc, o_sel, o_win)):
        term = gates[:, c * width:(c + 1) * width] * jnp.concatenate(branch, axis=1)
        out = term if out is None else out + term
    o_ref[...] = out.astype(o_ref.dtype)


def _nsa_prompt_attn(tbl, thr, q, kcn, ksn_t, kwn_t, v_rows, gates, b, s):
    tq = min(ATTN_TILE, s)
    assert WINDOW == 2 * tq and tq + 1 >= MAX_DISTANCE and tq % FLASH_ROWS == 0
    nq = s // tq
    nc = s // CMP_BLOCK
    ns = s // SEL_BLOCK
    n_top = min(N_SELECTED, ns)
    hpg = HEADS_PER_GROUP
    k_spec = pl.BlockSpec((None, HEAD_DIM, s), lambda g, i, j: (i, 2 * g, 0))
    v_spec = lambda branch: pl.BlockSpec((s, LANES), lambda g, i, j: (i, branch * N_KV_HEADS + g))
    return pl.pallas_call(
        functools.partial(_nsa_prompt_kernel, n_top=n_top),
        grid=(N_KV_HEADS, b, nq),
        in_specs=[
            _smem_spec(), _smem_spec(),
            pl.BlockSpec((tq, Q_GROUP_COLS), lambda g, i, j: (i * nq + j, g)),
            pl.BlockSpec((nc, GROUP_COLS), lambda g, i, j: (i, g)),
            k_spec, v_spec(0), k_spec, v_spec(1),
            pl.BlockSpec((tq, LANES), lambda g, i, j: (i * nq + j, g)),
        ],
        out_specs=pl.BlockSpec((tq, Q_GROUP_COLS), lambda g, i, j: (i * nq + j, g)),
        out_shape=jax.ShapeDtypeStruct((b * s, D_MODEL), BF16),
        scratch_shapes=[pltpu.VMEM((hpg, 2, tq, tq), F32),
                        pltpu.VMEM((tq, tq), F32),
                        pltpu.VMEM((hpg, nq, tq, nc), F32),
                        pltpu.VMEM((tq, tq), F32),
                        pltpu.VMEM((hpg, tq, HEAD_DIM), BF16),
                        pltpu.VMEM((2, hpg, tq, LANES), F32),
                        pltpu.VMEM((2, hpg, tq, LANES), F32)],
        compiler_params=_cparams(("arbitrary", "arbitrary", "arbitrary")),
        name="nsa_prompt_attn",
    )(tbl, thr, q, kcn, ksn_t, v_rows, kwn_t, v_rows, gates)


def _bucket_onehot(dist_row, thr_ref):
    n = dist_row.shape[1]
    bucket = jnp.zeros(dist_row.shape, jnp.int32)
    for k in range(1, N_BUCKETS):
        bucket = bucket + jnp.where(dist_row >= thr_ref[k], 1, 0)
    return (lax.broadcasted_iota(jnp.int32, (N_BUCKETS, n), 0) == bucket).astype(BF16)


def _sample_cmp_kernel(thr_ref, q_ref, kc_ref, tbl_ref, oc_ref, idx_ref, *, past_len, seq, n_top):
    nc = kc_ref.shape[0]
    ns = past_len // SEL_BLOCK
    hpg = HEADS_PER_GROUP
    rows = hpg * seq
    kc = kc_ref[...]
    pair = (lax.broadcasted_iota(jnp.int32, (nc, ns), 0) // CMP_PER_SEL
            == lax.broadcasted_iota(jnp.int32, (nc, ns), 1)).astype(BF16)
    eye = (lax.broadcasted_iota(jnp.int32, (ns, ns), 0)
           == lax.broadcasted_iota(jnp.int32, (ns, ns), 1))
    eye_bf = eye.astype(BF16)
    ii = lax.broadcasted_iota(jnp.int32, (ns, ns), 0)
    jj = lax.broadcasted_iota(jnp.int32, (ns, ns), 1)
    t_row = lax.broadcasted_iota(jnp.int32, (rows, nc), 0) % seq
    n_io = lax.broadcasted_iota(jnp.int32, (rows, nc), 1)
    rel = past_len + t_row - ((n_io + 1) * CMP_BLOCK - 1)
    mask = rel >= 0
    blk = lax.broadcasted_iota(jnp.int32, (SUBLANES, ns), 1)
    forced = (blk == 0) | (blk == ns - 1)
    rank_lane = lax.broadcasted_iota(jnp.int32, (ns, LANES), 1).astype(F32)
    blk_col = lax.broadcasted_iota(jnp.int32, (ns, LANES), 0).astype(F32)
    for g in range(N_KV_HEADS):
        kck = kc[:, g * GROUP_COLS:g * GROUP_COLS + HEAD_DIM]
        kcv = kc[:, g * GROUP_COLS + HEAD_DIM:(g + 1) * GROUP_COLS]
        qg = q_ref[g].astype(BF16)
        tbl_g = tbl_ref[g]
        s = _dot_nt(qg, kck)
        s = s + _bucket_bias(rel, thr_ref, lambda k: tbl_g[:, k:k + 1])
        sm = jnp.where(mask, s, NEG)
        e = jnp.exp(sm - jnp.max(sm, axis=-1, keepdims=True))
        p = jnp.where(mask, e / jnp.sum(e, axis=-1, keepdims=True), 0.0)
        oc_ref[g] = _dot(p.astype(BF16), kcv)
        imp = p[0:seq]
        for r in range(1, hpg):
            imp = imp + p[r * seq:(r + 1) * seq]
        imp = jnp.concatenate([imp, jnp.zeros((SUBLANES - seq, nc), F32)], axis=0)
        score = jnp.where(forced, FORCED_SCORE, _exact_dot(imp, pair))
        hi, mid, lo = _split3(score)
        score_t = (_dot_nt(eye_bf, hi) + _dot_nt(eye_bf, mid)) + _dot_nt(eye_bf, lo)
        for t in range(seq):
            s_row = score[t:t + 1, :]
            s_col = score_t[:, t:t + 1]
            beats = (ii != jj) & ((s_row > s_col) | ((s_row == s_col) & (jj < ii)))
            rank_col = jnp.sum(jnp.where(beats, 1.0, 0.0), axis=1, keepdims=True)
            onehot = rank_col == rank_lane
            idx_row = jnp.sum(jnp.where(onehot, blk_col, 0.0), axis=0, keepdims=True)
            idx_ref[g * seq + t:g * seq + t + 1, :] = idx_row.astype(jnp.int32)


def _sample_cmp(thr, q_rt, kcn, tbl_rt, b, past_len, seq):
    nc = kcn.shape[0] // b
    ns = past_len // SEL_BLOCK
    n_top = min(N_SELECTED - 1, ns)
    rows = HEADS_PER_GROUP * seq
    return pl.pallas_call(
        functools.partial(_sample_cmp_kernel, past_len=past_len, seq=seq, n_top=n_top),
        grid=(b,),
        in_specs=[
            _smem_spec(),
            pl.BlockSpec((None, N_KV_HEADS, rows, HEAD_DIM), lambda i: (i, 0, 0, 0)),
            pl.BlockSpec((nc, KV_COLS), lambda i: (i, 0)),
            _const_spec(tbl_rt.shape),
        ],
        out_specs=[
            pl.BlockSpec((None, N_KV_HEADS, rows, HEAD_DIM), lambda i: (i, 0, 0, 0)),
            pl.BlockSpec((None, N_KV_HEADS * seq, LANES), lambda i: (i, 0, 0)),
        ],
        out_shape=[
            jax.ShapeDtypeStruct((b, N_KV_HEADS, rows, HEAD_DIM), F32),
            jax.ShapeDtypeStruct((b, N_KV_HEADS * seq, LANES), jnp.int32),
        ],
        compiler_params=_cparams(("arbitrary",)),
        name="sample_cmp",
    )(thr, q_rt, kcn, tbl_rt)


def _norm_k_cols(kt, gain_col):
    ms = jnp.mean(kt * kt, axis=0, keepdims=True)
    return (kt * lax.rsqrt(ms + EPS) * gain_col).astype(BF16)


def _sample_sel_win_kernel(idx_ref, pt_ref, thr_ref, pool_ref, q_ref, ksnew_ref, kwnew_ref,
                           win_ref, tbl_ref, kgs_ref, kgw_ref, os_ref, ow_ref,
                           buf_ref, sem_ref, *, past_len, seq, n_top, page):
    i = pl.program_id(0)
    n_pages = past_len // page
    blocks_per_page = page // SEL_BLOCK
    n_keys = n_top * page
    stride = n_top + 1

    def block_of(g, t, j):
        return idx_ref[(i * N_KV_HEADS * seq + g * seq + t) * stride + j]

    def tile_copy(g, t, j):
        phys = pt_ref[i * n_pages + block_of(g, t, j) // blocks_per_page]
        return pltpu.make_async_copy(
            pool_ref.at[phys, g],
            buf_ref.at[g * seq + t, :, pl.ds(pl.multiple_of(j * page, page), page)],
            sem_ref.at[0])

    def for_all_tiles(fn):
        for g in range(N_KV_HEADS):
            for t in range(seq):
                def per_tile(j, c):
                    fn(tile_copy(g, t, j))
                    return c
                lax.fori_loop(0, n_top, per_tile, 0)

    for_all_tiles(lambda cp: cp.start())

    wb = win_ref.shape[2]
    wk = [_norm_k_cols(win_ref[g, 0:HEAD_DIM, :], kgw_ref[...]) for g in range(N_KV_HEADS)]
    wv = [win_ref[g, HEAD_DIM:GROUP_COLS, :].astype(BF16) for g in range(N_KV_HEADS)]
    j_io = lax.broadcasted_iota(jnp.int32, (1, wb), 1)
    tnew = lax.broadcasted_iota(jnp.int32, (1, SUBLANES), 1)

    def attend(q, tbl_g, k_past, v_past, buckets, mask, knew, bucketsn, maskn):
        s1 = jnp.where(mask, _dot(q, k_past) + _exact_dot(tbl_g, buckets), NEG)
        s2 = jnp.where(maskn, _dot_nt(q, knew[:, :HEAD_DIM].astype(BF16))
                       + _exact_dot(tbl_g, bucketsn), NEG)
        m = jnp.maximum(jnp.max(s1, axis=-1, keepdims=True), jnp.max(s2, axis=-1, keepdims=True))
        e1 = jnp.where(mask, jnp.exp(s1 - m), 0.0)
        e2 = jnp.where(maskn, jnp.exp(s2 - m), 0.0)
        l = jnp.sum(e1, axis=-1, keepdims=True) + jnp.sum(e2, axis=-1, keepdims=True)
        o = _dot_nt(e1.astype(BF16), v_past) + _dot(e2.astype(BF16),
                                                    knew[:, HEAD_DIM:].astype(BF16))
        return o / l

    def window(t, c):
        dist = wb + t - j_io
        maskw = (dist >= 0) & (dist < WINDOW)
        distn = t - tnew
        maskn = (distn >= 0) & (distn < WINDOW)
        buckets = _bucket_onehot(dist, thr_ref)
        bucketsn = _bucket_onehot(distn, thr_ref)
        for g in range(N_KV_HEADS):
            ow_ref[g, t] = attend(q_ref[g, t].astype(BF16), tbl_ref[g], wk[g], wv[g], buckets,
                                  maskw, kwnew_ref[g], bucketsn, maskn)
        return c
    lax.fori_loop(0, seq, window, 0)

    for_all_tiles(lambda cp: cp.wait())

    lane = lax.broadcasted_iota(jnp.int32, (1, n_keys), 1)
    tile_of_lane = lane // page
    row_in_page = lane % page

    def selected(t, c):
        distn = t - tnew
        maskn = distn >= 0
        bucketsn = _bucket_onehot(distn, thr_ref)
        for g in range(N_KV_HEADS):
            kvt = buf_ref[g * seq + t]
            page_base = jnp.zeros((1, n_keys), jnp.int32)
            half = jnp.zeros((1, n_keys), jnp.int32)
            for j in range(n_top):
                blk = block_of(g, t, j)
                page_base = jnp.where(tile_of_lane == j, (blk // blocks_per_page) * page, page_base)
                half = jnp.where(tile_of_lane == j, blk % blocks_per_page, half)
            mask = (row_in_page // SEL_BLOCK) == half
            dist = past_len + t - (page_base + row_in_page)
            os_ref[g, t] = attend(q_ref[g, t].astype(BF16), tbl_ref[g],
                                  _norm_k_cols(kvt[0:HEAD_DIM], kgs_ref[...]),
                                  kvt[HEAD_DIM:GROUP_COLS].astype(BF16),
                                  _bucket_onehot(dist, thr_ref), mask,
                                  ksnew_ref[g], bucketsn, maskn)
        return c
    lax.fori_loop(0, seq, selected, 0)


def _sample_sel_win(idx_flat, pt_flat, thr, pool_t, q_tr, ksnew, kwnew, win_t, tbl_r8,
                    kgs_col, kgw_col, b, past_len, seq):
    ns = past_len // SEL_BLOCK
    n_top = min(N_SELECTED - 1, ns)
    page = pool_t.shape[3]
    wb = win_t.shape[3]
    o_shape = jax.ShapeDtypeStruct((b, N_KV_HEADS, seq, SUBLANES, HEAD_DIM), F32)
    o_spec = pl.BlockSpec((None, N_KV_HEADS, seq, SUBLANES, HEAD_DIM),
                          lambda i, *_: (i, 0, 0, 0, 0))
    new_spec = pl.BlockSpec((None, N_KV_HEADS, SUBLANES, GROUP_COLS), lambda i, *_: (i, 0, 0, 0))
    grid_spec = pltpu.PrefetchScalarGridSpec(
        num_scalar_prefetch=3,
        grid=(b,),
        in_specs=[
            pl.BlockSpec(memory_space=pl.ANY),
            pl.BlockSpec((None, N_KV_HEADS, seq, SUBLANES, HEAD_DIM), lambda i, *_: (i, 0, 0, 0, 0)),
            new_spec, new_spec,
            pl.BlockSpec((None, N_KV_HEADS, GROUP_COLS, wb), lambda i, *_: (i, 0, 0, 0)),
            pl.BlockSpec(tbl_r8.shape, lambda i, *_: (0, 0, 0)),
            pl.BlockSpec(kgs_col.shape, lambda i, *_: (0, 0)),
            pl.BlockSpec(kgw_col.shape, lambda i, *_: (0, 0)),
        ],
        out_specs=[o_spec, o_spec],
        scratch_shapes=[pltpu.VMEM((N_KV_HEADS * seq, GROUP_COLS, n_top * page), F32),
                        pltpu.SemaphoreType.DMA((1,))],
    )
    return pl.pallas_call(
        functools.partial(_sample_sel_win_kernel, past_len=past_len, seq=seq, n_top=n_top,
                          page=page),
        grid_spec=grid_spec,
        out_shape=[o_shape, o_shape],
        compiler_params=_cparams(("arbitrary",)),
        name="sample_sel_win",
    )(idx_flat, pt_flat, thr, pool_t, q_tr, ksnew, kwnew, win_t, tbl_r8, kgs_col, kgw_col)


def _route(f, rhi_ref, rlo_ref):
    f_hi = f.astype(BF16)
    f_lo = (f - f_hi.astype(F32)).astype(BF16)
    logits = (_dot(f_hi, rhi_ref[...]) + _dot(f_lo, rhi_ref[...])) + _dot(f_hi, rlo_ref[...])
    lane = lax.broadcasted_iota(jnp.int32, logits.shape, 1).astype(F32)
    logits = jnp.where(lane < N_EXPERTS, logits, -jnp.inf)
    m1 = jnp.max(logits, axis=-1, keepdims=True)
    i1 = jnp.min(jnp.where(logits == m1, lane, float(LANES)), axis=-1, keepdims=True)
    rest = jnp.where(lane == i1, -jnp.inf, logits)
    m2 = jnp.max(rest, axis=-1, keepdims=True)
    i2 = jnp.min(jnp.where(rest == m2, lane, float(LANES)), axis=-1, keepdims=True)
    e2 = jnp.exp(m2 - m1)
    denom = 1.0 + e2
    return jnp.where(lane == i1, 1.0 / denom, 0.0) + jnp.where(lane == i2, e2 / denom, 0.0)


def _attn_out_kernel(h_ref, o_ref, wout_ref, gf_ref, rhi_ref, rlo_ref, h2_ref, f_ref, gate_ref):
    h2 = h_ref[...] + _dot(o_ref[...], wout_ref[...])
    h2_ref[...] = h2
    f = _rmsnorm(h2, gf_ref[...])
    f_ref[...] = f.astype(f_ref.dtype)
    gate_ref[...] = _route(f, rhi_ref, rlo_ref)


def _attn_out_merge_kernel(h_ref, oc_ref, os_ref, ow_ref, gt_ref, wout_ref, gf_ref, rhi_ref,
                           rlo_ref, h2_ref, f_ref, gate_ref):
    gates = gt_ref[...]
    n_gate = gates.shape[1]
    hpg = HEADS_PER_GROUP
    idx = lax.broadcasted_iota(jnp.int32, (n_gate, D_MODEL), 0)
    head = lax.broadcasted_iota(jnp.int32, (n_gate, D_MODEL), 1) // HEAD_DIM
    o = None
    for c, ref in enumerate((oc_ref, os_ref, ow_ref)):
        lane_of_gate = (head // hpg) * LANES + c * hpg + head % hpg
        gate = _exact_dot(gates, (idx == lane_of_gate).astype(BF16))
        term = gate * ref[...]
        o = term if o is None else o + term
    h2 = h_ref[...] + _dot(o.astype(BF16), wout_ref[...])
    h2_ref[...] = h2
    f = _rmsnorm(h2, gf_ref[...])
    f_ref[...] = f.astype(f_ref.dtype)
    gate_ref[...] = _route(f, rhi_ref, rlo_ref)


def _attn_out(h2d, o_list, gates, wout, gf, rhi, rlo):
    n, d = h2d.shape
    tm = min(TOKEN_TILE, n)
    row = lambda w: pl.BlockSpec((tm, w), lambda i: (i, 0))
    merged = len(o_list) == 1
    body = _attn_out_kernel if merged else _attn_out_merge_kernel
    in_specs = [row(d)] + [row(d)] * len(o_list) + ([] if merged else [row(gates.shape[1])])
    in_specs += [_const_spec(wout.shape), _const_spec((1, d)), _const_spec(rhi.shape),
                 _const_spec(rlo.shape)]
    args = [h2d] + list(o_list) + ([] if merged else [gates]) + [wout, gf, rhi, rlo]
    return pl.pallas_call(
        body,
        grid=(n // tm,),
        in_specs=in_specs,
        out_specs=[row(d), row(d), row(LANES)],
        out_shape=[jax.ShapeDtypeStruct((n, d), F32), jax.ShapeDtypeStruct((n, d), BF16),
                   jax.ShapeDtypeStruct((n, LANES), F32)],
        compiler_params=_cparams(("arbitrary",)),
        name="attn_out_route",
    )(*args)


def _moe_kernel(f_ref, h2_ref, gate_ref, p_ref, wg_ref, wu_ref, wd_ref, gp_ref, wpg_ref, wpp_ref,
                o_ref, acc_ref):
    e = pl.program_id(1)

    @pl.when(e == 0)
    def _():
        acc_ref[...] = jnp.zeros_like(acc_ref)

    f = f_ref[...]
    gate = gate_ref[...]
    lane = lax.broadcasted_iota(jnp.int32, gate.shape, 1)
    ge = jnp.sum(jnp.where(lane == e, gate, 0.0), axis=-1, keepdims=True)
    hid = _silu(_dot(f, wg_ref[...])) * _dot(f, wu_ref[...])
    acc_ref[...] += ge * _dot(hid.astype(BF16), wd_ref[...])

    @pl.when(e == pl.num_programs(1) - 1)
    def _():
        o_ref[...] = _ple(h2_ref[...] + acc_ref[...], p_ref[...], gp_ref[...], wpg_ref, wpp_ref)


def _moe(f2d, h2, gate, p2d, wg, wu, wd, gp, wpg, wpp):
    n, d = h2.shape
    tm = min(TOKEN_TILE, n)
    n_e, _, dff = wg.shape
    row = lambda w: pl.BlockSpec((tm, w), lambda i, e: (i, 0))
    return pl.pallas_call(
        _moe_kernel,
        grid=(n // tm, n_e),
        in_specs=[
            row(d), row(d), row(LANES), row(p2d.shape[1]),
            pl.BlockSpec((None, d, dff), lambda i, e: (e, 0, 0)),
            pl.BlockSpec((None, d, dff), lambda i, e: (e, 0, 0)),
            pl.BlockSpec((None, dff, d), lambda i, e: (e, 0, 0)),
            _const_spec((1, d)), _const_spec(wpg.shape), _const_spec(wpp.shape),
        ],
        out_specs=row(d),
        out_shape=jax.ShapeDtypeStruct((n, d), F32),
        scratch_shapes=[pltpu.VMEM((tm, d), F32)],
        compiler_params=_cparams(("arbitrary", "arbitrary")),
        name="moe_ple",
    )(f2d, h2, gate, p2d, wg, wu, wd, gp, wpg, wpp)


def _bucket_thresholds():
    n = jnp.arange(MAX_DISTANCE + 1)
    max_exact = N_BUCKETS // 2
    nf = jnp.maximum(n, 1).astype(F32)
    large = max_exact + (jnp.log(nf / max_exact) / math.log(MAX_DISTANCE / max_exact)
                         * (N_BUCKETS - max_exact)).astype(jnp.int32)
    bucket = jnp.where(n < max_exact, n, jnp.minimum(large, N_BUCKETS - 1))
    return jnp.sum(bucket[None, :] < jnp.arange(N_BUCKETS)[:, None], axis=1).astype(jnp.int32)


def _row(v):
    return v.reshape(1, -1).astype(F32)


def _key_gain_row(k_gain, cols):
    g = jnp.concatenate([k_gain, jnp.ones_like(k_gain)])
    return jnp.tile(g, cols // GROUP_COLS).reshape(1, cols).astype(F32)


def _rows_on_lanes(x):
    lead = x.shape[:-4]
    rows = x.shape[-4]
    nl = len(lead)
    perm = tuple(range(nl)) + (nl + 1, nl + 2, nl + 3, nl)
    return jnp.transpose(x, perm).reshape(*lead, N_KV_HEADS, GROUP_COLS, rows)


def _rows_on_lanes_inverse(xt):
    b, _, rows = xt.shape
    x = xt.reshape(b, N_KV_HEADS, 2, HEAD_DIM, rows)
    return jnp.transpose(x, (0, 4, 1, 2, 3))[None]


def kernel(x_prompt, x_sample, state_conv, cache_cmp, cache_sel, state_win, page_table,
           p_prompt, p_sample, norm_mix, norm_ffn, norm_ple, conv_w_in, conv_w, conv_w_out,
           nsa_w_in, nsa_w_cmp, nsa_q_norm, nsa_k_norm, nsa_w_out, rel_bias,
           ffn_w_gate, ffn_w_up, ffn_w_down, moe_router, moe_w_gate, moe_w_up, moe_w_down,
           ple_w_proj, ple_w_gate):
    bp, sp, d = x_prompt.shape
    bs, ss, _ = x_sample.shape
    n_p = bp * sp
    n_s = bs * ss
    page = cache_cmp.shape[2]
    past_len = page_table.shape[1] * page
    hpg = HEADS_PER_GROUP
    bf = lambda w: w.astype(BF16)

    cw = jnp.zeros((SUBLANES, d), F32).at[:CONV_WIDTH].set(conv_w[0])
    w_in0, w_out0 = bf(conv_w_in[0]), bf(conv_w_out[0])
    g_mix0 = _row(norm_mix[0])
    h_p, tail_p = _mix0_prompt(x_prompt, g_mix0, w_in0, cw, w_out0)
    conv_prompt = tail_p[:, SUBLANES - (CONV_WIDTH - 1):][None]

    st = state_conv[0]
    zeros = jnp.zeros((bs, ss, d), F32)
    s1 = zeros.at[:, 0].set(st[:, 1]).reshape(n_s, d)
    s2 = zeros.at[:, 0].set(st[:, 0]).at[:, 1].set(st[:, 1]).reshape(n_s, d)
    h_s, u_s = _mix0_sample(x_sample.reshape(n_s, d), g_mix0, w_in0, cw, w_out0, s1, s2, ss)
    conv_sample = u_s.reshape(bs, ss, d)[:, ss - (CONV_WIDTH - 1):][None]

    ffn0_w = (_row(norm_ffn[0]), bf(ffn_w_gate[0]), bf(ffn_w_up[0]), bf(ffn_w_down[0]),
              _row(norm_ple[0]), bf(ple_w_gate[0]), bf(ple_w_proj[0]))
    h_p = _ffn0(h_p.reshape(n_p, d), p_prompt[0].reshape(n_p, -1), *ffn0_w)
    h_s = _ffn0(h_s, p_sample[0].reshape(n_s, -1), *ffn0_w)

    q_cols = N_HEADS * HEAD_DIM
    kv_cols = N_BRANCHES * KV_COLS
    w_in1 = nsa_w_in[0]
    wq = bf(w_in1[:, :q_cols])
    wkv = w_in1[:, q_cols:q_cols + kv_cols]
    wkc = bf(wkv[:, :KV_COLS])
    wkvt = bf(wkv.T)
    wv_src = wkv[:, KV_COLS:].reshape(d, 2 * N_KV_HEADS, 2, HEAD_DIM)[:, :, 1]
    wv = bf(jnp.zeros((d, 2 * N_KV_HEADS, LANES), F32).at[:, :, :HEAD_DIM].set(wv_src).reshape(
        d, 2 * KV_COLS))
    wg_src = w_in1[:, q_cols + kv_cols:].reshape(d, N_BRANCHES, N_KV_HEADS, hpg)
    wgt = jnp.zeros((d, N_KV_HEADS, LANES), F32).at[:, :, :N_BRANCHES * hpg].set(
        wg_src.transpose(0, 2, 1, 3).reshape(d, N_KV_HEADS, N_BRANCHES * hpg))
    wgt = bf(wgt.reshape(d, N_KV_HEADS * LANES))
    gi = jnp.arange(MXU_DIM) // HEAD_DIM
    gmat = (gi[:, None] == gi[None, :]).astype(BF16)
    qg = jnp.tile(nsa_q_norm[0], N_HEADS).reshape(1, q_cols).astype(F32)
    kgc = _key_gain_row(nsa_k_norm[0, 0], KV_COLS)
    kgs_col = nsa_k_norm[0, 1].reshape(HEAD_DIM, 1).astype(F32)
    kgw_col = nsa_k_norm[0, 2].reshape(HEAD_DIM, 1).astype(F32)
    proj_w = (_row(norm_mix[1]), wq, wkc, wkvt, wv, wgt, gmat, qg, kgs_col, kgw_col)

    per_tile = MXU_DIM // HEAD_DIM
    wc = nsa_w_cmp[0].reshape(CMP_BLOCK, 2, per_tile, HEAD_DIM, HEAD_DIM)
    w2 = bf(jnp.einsum('lhade,ab->lhadbe', wc, jnp.eye(per_tile, dtype=F32)).reshape(
        CMP_BLOCK, 2, MXU_DIM, MXU_DIM))

    thr = _bucket_thresholds()
    tbl = rel_bias.astype(F32)

    q_p, kvc_p, kvct_p, kvst_p, kvwt_p, ksn_p, kwn_p, vr_p, gt_p = _nsa_proj(h_p, bp, *proj_w)
    kcn_p = _compress_prompt(kvc_p, w2, gmat, kgc)
    o_p = _nsa_prompt_attn(tbl, thr, q_p, kcn_p, ksn_p, kwn_p, vr_p, gt_p, bp, sp)

    q_s, _, kvct_s, kvst_s, kvwt_s, ksn_s, kwn_s, _, gt_s = _nsa_proj(h_s, 1, *proj_w)
    pt_flat = page_table.reshape(-1).astype(jnp.int32)
    kcn_s = _compress_paged(pt_flat, _rows_on_lanes(cache_cmp[0]), w2, gmat, kgc, PAGES_PER_STEP)
    q5 = q_s.astype(F32).reshape(bs, ss, N_KV_HEADS, hpg, HEAD_DIM)
    q_rt = q5.transpose(0, 2, 3, 1, 4).reshape(bs, N_KV_HEADS, hpg * ss, HEAD_DIM)
    q_tr = jnp.zeros((bs, N_KV_HEADS, ss, SUBLANES, HEAD_DIM), F32).at[:, :, :, :hpg].set(
        q5.transpose(0, 2, 1, 3, 4))
    tbl_gr = tbl.T.reshape(N_KV_HEADS, hpg, N_BUCKETS)
    tbl_rt = jnp.repeat(tbl_gr, ss, axis=1)
    tbl_r8 = jnp.zeros((N_KV_HEADS, SUBLANES, N_BUCKETS), F32).at[:, :hpg].set(tbl_gr)
    oc_s, idx = _sample_cmp(thr, q_rt, kcn_s, tbl_rt, bs, past_len, ss)
    n_top_s = min(N_SELECTED - 1, past_len // SEL_BLOCK)
    idx_flat = idx[:, :, :n_top_s + 1].reshape(-1)

    def new_rows(kn_t):
        r = kn_t[0].astype(F32).reshape(N_KV_HEADS, GROUP_COLS, bs, ss).transpose(2, 0, 3, 1)
        return jnp.zeros((bs, N_KV_HEADS, SUBLANES, GROUP_COLS), F32).at[:, :, :ss].set(r)

    win_t = _rows_on_lanes(state_win[0])
    os_s, ow_s = _sample_sel_win(idx_flat, pt_flat, thr, _rows_on_lanes(cache_sel[0]), q_tr,
                                 new_rows(ksn_s), new_rows(kwn_s), win_t, tbl_r8,
                                 kgs_col, kgw_col, bs, past_len, ss)
    oc_s2 = oc_s.reshape(bs, N_KV_HEADS, hpg, ss, HEAD_DIM).transpose(0, 3, 1, 2, 4).reshape(n_s, d)
    to2d = lambda o: o[:, :, :, :hpg].transpose(0, 2, 1, 3, 4).reshape(n_s, d)

    w_out1 = bf(nsa_w_out[0])
    router = jnp.zeros((d, LANES), F32).at[:, :N_EXPERTS].set(moe_router[0])
    r_hi = bf(router)
    r_lo = bf(router - r_hi.astype(F32))
    gf1 = _row(norm_ffn[1])
    moe_w = (bf(moe_w_gate[0]), bf(moe_w_up[0]), bf(moe_w_down[0]), _row(norm_ple[1]),
             bf(ple_w_gate[1]), bf(ple_w_proj[1]))
    h2_p, f_p, gate_p = _attn_out(h_p, [o_p], None, w_out1, gf1, r_hi, r_lo)
    y_p = _moe(f_p, h2_p, gate_p, p_prompt[1].reshape(n_p, -1), *moe_w)
    h2_s, f_s, gate_s = _attn_out(h_s, [oc_s2, to2d(os_s), to2d(ow_s)], gt_s, w_out1, gf1,
                                  r_hi, r_lo)
    y_s = _moe(f_s, h2_s, gate_s, p_sample[1].reshape(n_s, -1), *moe_w)

    sample_rows = lambda xt: _rows_on_lanes_inverse(
        xt[0].reshape(KV_COLS, bs, ss).transpose(1, 0, 2))
    wbp = min(WINDOW, sp)
    win_s_t = jnp.concatenate(
        [win_t, kvwt_s[0].reshape(N_KV_HEADS, GROUP_COLS, bs, ss).transpose(2, 0, 1, 3)],
        axis=3)[..., ss:]
    return (y_p.reshape(bp, sp, d), y_s.reshape(bs, ss, d), conv_prompt, conv_sample,
            _rows_on_lanes_inverse(kvct_p), sample_rows(kvct_s),
            _rows_on_lanes_inverse(kvst_p), sample_rows(kvst_s),
            _rows_on_lanes_inverse(kvwt_p[:, :, sp - wbp:]),
            _rows_on_lanes_inverse(win_s_t.reshape(bs, KV_COLS, -1)))
```

```python
import functools
import math

import jax
import jax.numpy as jnp
from jax import lax
from jax.experimental import pallas as pl
from jax.experimental.pallas import tpu as pltpu

F32 = jnp.float32
BF16 = jnp.bfloat16

D_MODEL = 1024
N_HEADS = 16
HEAD_DIM = 64
N_KV_HEADS = 4
HEADS_PER_GROUP = 4
N_BRANCHES = 3
CMP_BLOCK = 32
SEL_BLOCK = 64
CMP_PER_SEL = SEL_BLOCK // CMP_BLOCK
N_SELECTED = 16
WINDOW = 512
N_BUCKETS = 32
MAX_DISTANCE = 128
N_EXPERTS = 8
CONV_WIDTH = 3
EPS = 1e-6
FORCED_SCORE = 1e4
NEG = -1e30
KV_COLS = N_KV_HEADS * 2 * HEAD_DIM
GROUP_COLS = 2 * HEAD_DIM
Q_GROUP_COLS = HEADS_PER_GROUP * HEAD_DIM

LANES = 128
SUBLANES = 8
MXU_DIM = 256
VMEM_LIMIT = 56 * 1024 * 1024
SLABS = KV_COLS // LANES

TOKEN_TILE = 512
ATTN_TILE = 256
PAGES_PER_STEP = 32


def _cparams(sem, vmem=VMEM_LIMIT):
    return pltpu.CompilerParams(dimension_semantics=sem, vmem_limit_bytes=vmem)


def _const_spec(shape):
    nd = len(shape)
    return pl.BlockSpec(shape, lambda *_: (0,) * nd, pipeline_mode=pl.Buffered(1))


def _smem_spec():
    return pl.BlockSpec(memory_space=pltpu.SMEM)


def _dot(a, b):
    return jnp.dot(a, b, preferred_element_type=F32)


def _dot_nt(a, b):
    return lax.dot_general(a, b, (((1,), (1,)), ((), ())), preferred_element_type=F32)


def _split3(x):
    hi = x.astype(BF16)
    r1 = x - hi.astype(F32)
    mid = r1.astype(BF16)
    lo = (r1 - mid.astype(F32)).astype(BF16)
    return hi, mid, lo


def _exact_dot(x, m01):
    hi, mid, lo = _split3(x)
    return (_dot(hi, m01) + _dot(mid, m01)) + _dot(lo, m01)


def _rmsnorm(x, g):
    ms = jnp.mean(x * x, axis=-1, keepdims=True)
    return x * lax.rsqrt(ms + EPS) * g


def _group_mean_sq(x, gmat):
    n = x.shape[1]
    w = gmat.shape[0]
    outs = []
    for c in range(n // w):
        blk = x[:, c * w:(c + 1) * w]
        sq = blk * blk
        hi = sq.astype(BF16)
        lo = (sq - hi.astype(F32)).astype(BF16)
        outs.append(_dot(hi, gmat) + _dot(lo, gmat))
    out = outs[0] if len(outs) == 1 else jnp.concatenate(outs, axis=1)
    return out * (1.0 / HEAD_DIM)


def _norm_keys(kv, gain_row, gmat):
    ms = _group_mean_sq(kv, gmat)
    lane = lax.broadcasted_iota(jnp.int32, kv.shape, 1)
    is_k = (lane % GROUP_COLS) < HEAD_DIM
    return jnp.where(is_k, kv * lax.rsqrt(ms + EPS) * gain_row, kv)


def _bucket_bias(dist, thr_ref, value_of_bucket):
    val = value_of_bucket(0)
    val = jnp.broadcast_to(val, dist.shape).astype(F32)
    for k in range(1, N_BUCKETS):
        val = jnp.where(dist >= thr_ref[k], value_of_bucket(k), val)
    return val


def _silu(x):
    return x * jax.nn.sigmoid(x)


def _ple(h, p, gain, wpg_ref, wpp_ref):
    r = _rmsnorm(h, gain).astype(BF16)
    g = jax.nn.sigmoid(_dot(r, wpg_ref[...]))
    return h + g * _dot(p.astype(BF16), wpp_ref[...])


def _conv_mix_tail(x, gb, u, um1, um2, cw_ref, wout_ref):
    conv = cw_ref[0:1, :] * um2 + cw_ref[1:2, :] * um1 + cw_ref[2:3, :] * u
    y = _dot((gb * conv).astype(BF16), wout_ref[...])
    return x + y


def _mix0_prompt_kernel(x_ref, g_ref, win_ref, cw_ref, wout_ref, h_ref, st_ref, carry_ref):
    j = pl.program_id(1)
    tm = x_ref.shape[0]

    @pl.when(j == 0)
    def _():
        carry_ref[...] = jnp.zeros_like(carry_ref)

    x = x_ref[...]
    a = _rmsnorm(x, g_ref[...]).astype(BF16)
    proj = _dot(a, win_ref[...])
    gb = proj[:, :D_MODEL]
    u = proj[:, D_MODEL:2 * D_MODEL] * proj[:, 2 * D_MODEL:]
    c0 = carry_ref[SUBLANES - 2:SUBLANES - 1, :]
    c1 = carry_ref[SUBLANES - 1:SUBLANES, :]
    row = lax.broadcasted_iota(jnp.int32, u.shape, 0)
    um1 = jnp.where(row == 0, c1, pltpu.roll(u, 1, 0))
    um2 = jnp.where(row == 0, c0, jnp.where(row == 1, c1, pltpu.roll(u, 2, 0)))
    h_ref[...] = _conv_mix_tail(x, gb, u, um1, um2, cw_ref, wout_ref)
    tail = u[tm - SUBLANES:, :]
    carry_ref[...] = tail
    st_ref[...] = tail


def _mix0_sample_kernel(x_ref, g_ref, win_ref, cw_ref, wout_ref, s1_ref, s2_ref, h_ref, u_ref,
                        *, seq):
    x = x_ref[...]
    a = _rmsnorm(x, g_ref[...]).astype(BF16)
    proj = _dot(a, win_ref[...])
    gb = proj[:, :D_MODEL]
    u = proj[:, D_MODEL:2 * D_MODEL] * proj[:, 2 * D_MODEL:]
    t = lax.broadcasted_iota(jnp.int32, u.shape, 0) % seq
    um1 = jnp.where(t >= 1, pltpu.roll(u, 1, 0), s1_ref[...])
    um2 = jnp.where(t >= 2, pltpu.roll(u, 2, 0), s2_ref[...])
    h_ref[...] = _conv_mix_tail(x, gb, u, um1, um2, cw_ref, wout_ref)
    u_ref[...] = u


def _mix0_prompt(x, gain, w_in, cw, w_out):
    b, s, d = x.shape
    tm = min(TOKEN_TILE, s)
    grid = (b, s // tm)
    return pl.pallas_call(
        _mix0_prompt_kernel,
        grid=grid,
        in_specs=[
            pl.BlockSpec((None, tm, d), lambda i, j: (i, j, 0)),
            _const_spec((1, d)),
            _const_spec(w_in.shape),
            _const_spec(cw.shape),
            _const_spec(w_out.shape),
        ],
        out_specs=[
            pl.BlockSpec((None, tm, d), lambda i, j: (i, j, 0)),
            pl.BlockSpec((None, SUBLANES, d), lambda i, j: (i, 0, 0)),
        ],
        out_shape=[
            jax.ShapeDtypeStruct((b, s, d), F32),
            jax.ShapeDtypeStruct((b, SUBLANES, d), F32),
        ],
        scratch_shapes=[pltpu.VMEM((SUBLANES, d), F32)],
        compiler_params=_cparams(("arbitrary", "arbitrary")),
        name="mix0_prompt",
    )(x, gain, w_in, cw, w_out)


def _mix0_sample(x2d, gain, w_in, cw, w_out, s1, s2, seq):
    n, d = x2d.shape
    return pl.pallas_call(
        functools.partial(_mix0_sample_kernel, seq=seq),
        out_shape=[jax.ShapeDtypeStruct((n, d), F32), jax.ShapeDtypeStruct((n, d), F32)],
        compiler_params=_cparams(None),
        name="mix0_sample",
    )(x2d, gain, w_in, cw, w_out, s1, s2)


def _ffn0_kernel(h_ref, p_ref, gf_ref, wg_ref, wu_ref, wd_ref, gp_ref, wpg_ref, wpp_ref, o_ref,
                 *, chunk):
    h = h_ref[...]
    f = _rmsnorm(h, gf_ref[...]).astype(BF16)
    d_ff = wg_ref.shape[1]
    acc = None
    for c in range(d_ff // chunk):
        sl = slice(c * chunk, (c + 1) * chunk)
        hid = _silu(_dot(f, wg_ref[:, sl])) * _dot(f, wu_ref[:, sl])
        part = _dot(hid.astype(BF16), wd_ref[sl, :])
        acc = part if acc is None else acc + part
    o_ref[...] = _ple(h + acc, p_ref[...], gp_ref[...], wpg_ref, wpp_ref)


def _ffn0(h2d, p2d, gf, wg, wu, wd, gp, wpg, wpp):
    n, d = h2d.shape
    tm = min(TOKEN_TILE, n)
    d_ff = wg.shape[1]
    chunk = d_ff // 2
    assert chunk % LANES == 0
    return pl.pallas_call(
        functools.partial(_ffn0_kernel, chunk=chunk),
        grid=(n // tm,),
        in_specs=[
            pl.BlockSpec((tm, d), lambda i: (i, 0)),
            pl.BlockSpec((tm, p2d.shape[1]), lambda i: (i, 0)),
            _const_spec((1, d)),
            _const_spec(wg.shape), _const_spec(wu.shape), _const_spec(wd.shape),
            _const_spec((1, d)),
            _const_spec(wpg.shape), _const_spec(wpp.shape),
        ],
        out_specs=pl.BlockSpec((tm, d), lambda i: (i, 0)),
        out_shape=jax.ShapeDtypeStruct((n, d), F32),
        compiler_params=_cparams(("arbitrary",)),
        name="ffn0_ple",
    )(h2d, p2d, gf, wg, wu, wd, gp, wpg, wpp)


def _norm_keys_t(kvt, gain_col):
    parts = []
    for gi in range(KV_COLS // HEAD_DIM):
        x = kvt[gi * HEAD_DIM:(gi + 1) * HEAD_DIM, :]
        if gi % 2 == 0:
            ms = jnp.mean(x * x, axis=0, keepdims=True)
            x = x * lax.rsqrt(ms + EPS) * gain_col
        parts.append(x)
    return jnp.concatenate(parts, axis=0)


def _nsa_proj_kernel(h_ref, g_ref, wq_ref, wkc_ref, wkvt_ref, wv_ref, wgt_ref, gm_ref, qg_ref,
                     kgs_ref, kgw_ref, q_ref, kvc_ref, kvct_ref, kvst_ref, kvwt_ref, ksn_ref,
                     kwn_ref, vr_ref, gt_ref):
    a = _rmsnorm(h_ref[...], g_ref[...]).astype(BF16)
    vr = _dot(a, wv_ref[...])
    lane = lax.broadcasted_iota(jnp.int32, vr.shape, 1)
    vr_ref[...] = jnp.where(lane % LANES < HEAD_DIM, vr, 1.0).astype(vr_ref.dtype)
    q = _dot(a, wq_ref[...])
    ms = _group_mean_sq(q, gm_ref[...])
    q_ref[...] = ((q * lax.rsqrt(ms + EPS) * qg_ref[...]) * (HEAD_DIM ** -0.5)).astype(q_ref.dtype)
    kvc = _dot(a, wkc_ref[...])
    for g in range(SLABS):
        kvc_ref[g] = kvc[:, g * LANES:(g + 1) * LANES]
    kvt = _dot_nt(wkvt_ref[...], a)
    kvst = kvt[KV_COLS:2 * KV_COLS]
    kvwt = kvt[2 * KV_COLS:]
    kvct_ref[...] = kvt[:KV_COLS]
    kvst_ref[...] = kvst
    kvwt_ref[...] = kvwt
    ksn_ref[...] = _norm_keys_t(kvst, kgs_ref[...]).astype(ksn_ref.dtype)
    kwn_ref[...] = _norm_keys_t(kvwt, kgw_ref[...]).astype(kwn_ref.dtype)
    gt_ref[...] = jax.nn.sigmoid(_dot(a, wgt_ref[...]))


def _nsa_proj(h2d, n_seq, gain, wq, wkc, wkvt, wv, wgt, gmat, qg, kgs_col, kgw_col):
    n, d = h2d.shape
    s = n // n_seq
    tm = min(TOKEN_TILE, s)
    tps = s // tm
    row = lambda w: pl.BlockSpec((tm, w), lambda i: (i, 0))
    col = pl.BlockSpec((None, KV_COLS, tm), lambda i: (i // tps, 0, i % tps))
    tshape = lambda dt: jax.ShapeDtypeStruct((n_seq, KV_COLS, s), dt)
    n_gate_cols = wgt.shape[1]
    return pl.pallas_call(
        _nsa_proj_kernel,
        grid=(n // tm,),
        in_specs=[
            row(d), _const_spec((1, d)),
            _const_spec(wq.shape), _const_spec(wkc.shape), _const_spec(wkvt.shape),
            _const_spec(wv.shape), _const_spec(wgt.shape), _const_spec(gmat.shape),
            _const_spec(qg.shape), _const_spec(kgs_col.shape), _const_spec(kgw_col.shape),
        ],
        out_specs=[row(d), pl.BlockSpec((SLABS, tm, LANES), lambda i: (0, i, 0)),
                   col, col, col, col, col, row(wv.shape[1]), row(n_gate_cols)],
        out_shape=[
            jax.ShapeDtypeStruct((n, d), BF16),
            jax.ShapeDtypeStruct((SLABS, n, LANES), F32),
            tshape(F32), tshape(F32), tshape(F32), tshape(BF16), tshape(BF16),
            jax.ShapeDtypeStruct((n, wv.shape[1]), BF16),
            jax.ShapeDtypeStruct((n, n_gate_cols), F32),
        ],
        compiler_params=_cparams(("arbitrary",)),
        name="nsa_proj",
    )(h2d, gain, wq, wkc, wkvt, wv, wgt, gmat, qg, kgs_col, kgw_col)


def _compress_rows(read_slab, n_blocks, w_ref, gmat, kg_row):
    half = KV_COLS // 2
    per_half = SLABS // 2
    acc = [jnp.zeros((n_blocks, half), F32) for _ in range(2)]
    for l in range(CMP_BLOCK):
        for hf in range(2):
            xl = jnp.concatenate([read_slab(l, hf * per_half + c) for c in range(per_half)], axis=1)
            acc[hf] = acc[hf] + _dot(xl.astype(BF16), w_ref[l, hf])
    kv = jnp.concatenate(acc, axis=1)
    return _norm_keys(kv, kg_row, gmat).astype(BF16)


def _compress_kernel(x_ref, w_ref, gm_ref, kg_ref, o_ref):
    nb = o_ref.shape[0]
    read = lambda l, g: x_ref[g, pl.ds(l, nb, stride=CMP_BLOCK), :]
    o_ref[...] = _compress_rows(read, nb, w_ref, gm_ref[...], kg_ref[...])


def _compress_prompt(slabs, w2, gmat, kg):
    n = slabs.shape[1]
    tr = min(4096, n)
    nb = tr // CMP_BLOCK
    return pl.pallas_call(
        _compress_kernel,
        grid=(n // tr,),
        in_specs=[pl.BlockSpec((SLABS, tr, LANES), lambda i: (0, i, 0)),
                  _const_spec(w2.shape), _const_spec(gmat.shape), _const_spec(kg.shape)],
        out_specs=pl.BlockSpec((nb, KV_COLS), lambda i: (i, 0)),
        out_shape=jax.ShapeDtypeStruct((n // CMP_BLOCK, KV_COLS), BF16),
        compiler_params=_cparams(("arbitrary",)),
        name="compress_prompt",
    )(slabs, w2, gmat, kg)


def _compress_paged_kernel(pt_ref, pool_ref, w_ref, gm_ref, kg_ref, o_ref, buf_ref, st_ref, sem_ref,
                           *, pages_per_step, page):
    i = pl.program_id(0)
    n = pl.num_programs(0)

    def page_copy(step, p, slot):
        phys = pt_ref[step * pages_per_step + p]
        return pltpu.make_async_copy(pool_ref.at[phys], buf_ref.at[slot, p], sem_ref.at[slot])

    def start(step, slot):
        def issue(p, c):
            page_copy(step, p, slot).start()
            return c
        lax.fori_loop(0, pages_per_step, issue, 0)

    @pl.when(i == 0)
    def _():
        start(0, 0)

    @pl.when(i + 1 < n)
    def _():
        start(i + 1, (i + 1) % 2)

    slot = i % 2

    def wait(p, c):
        page_copy(i, p, slot).wait()
        return c
    lax.fori_loop(0, pages_per_step, wait, 0)

    blocks_per_page = page // CMP_BLOCK
    group = 2 * blocks_per_page
    assert group == SUBLANES
    ri = lax.broadcasted_iota(jnp.int32, (2 * page, 2 * page), 0)
    ci = lax.broadcasted_iota(jnp.int32, (2 * page, 2 * page), 1)
    src_lane = ((ri % group) // blocks_per_page) * page + (ri % blocks_per_page) * CMP_BLOCK + ri // group
    perm = (ci == src_lane).astype(BF16)

    def to_rows(pp, c):
        for g0 in range(0, N_KV_HEADS, 2):
            pair = jnp.concatenate([buf_ref[slot, 2 * pp, g0:g0 + 2].reshape(2 * GROUP_COLS, page),
                                    buf_ref[slot, 2 * pp + 1, g0:g0 + 2].reshape(2 * GROUP_COLS, page)],
                                   axis=1).astype(BF16)
            rows = _dot_nt(perm, pair)
            for l in range(CMP_BLOCK):
                dst = pl.ds(pl.multiple_of(pp * group, group), group)
                blk = rows[l * group:(l + 1) * group]
                st_ref[g0, l, dst, :] = blk[:, :GROUP_COLS]
                st_ref[g0 + 1, l, dst, :] = blk[:, GROUP_COLS:]
        return c
    lax.fori_loop(0, pages_per_step // 2, to_rows, 0)

    nb = pages_per_step * page // CMP_BLOCK
    read = lambda l, g: st_ref[g, l]
    o_ref[...] = _compress_rows(read, nb, w_ref, gm_ref[...], kg_ref[...])


def _compress_paged(page_table_flat, pool_t, w2, gmat, kg, pages_per_step):
    n_pages = page_table_flat.shape[0]
    page = pool_t.shape[3]
    assert n_pages % pages_per_step == 0 and page == LANES and pool_t.shape[2] == GROUP_COLS
    steps = n_pages // pages_per_step
    nb = pages_per_step * page // CMP_BLOCK
    grid_spec = pltpu.PrefetchScalarGridSpec(
        num_scalar_prefetch=1,
        grid=(steps,),
        in_specs=[pl.BlockSpec(memory_space=pl.ANY),
                  pl.BlockSpec(w2.shape, lambda i, pt: (0, 0, 0, 0), pipeline_mode=pl.Buffered(1)),
                  pl.BlockSpec(gmat.shape, lambda i, pt: (0, 0)),
                  pl.BlockSpec(kg.shape, lambda i, pt: (0, 0))],
        out_specs=pl.BlockSpec((nb, KV_COLS), lambda i, pt: (i, 0)),
        scratch_shapes=[pltpu.VMEM((2, pages_per_step, N_KV_HEADS, GROUP_COLS, page), F32),
                        pltpu.VMEM((N_KV_HEADS, CMP_BLOCK, nb, GROUP_COLS), F32),
                        pltpu.SemaphoreType.DMA((2,))],
    )
    return pl.pallas_call(
        functools.partial(_compress_paged_kernel, pages_per_step=pages_per_step, page=page),
        grid_spec=grid_spec,
        out_shape=jax.ShapeDtypeStruct((steps * nb, KV_COLS), BF16),
        compiler_params=_cparams(("arbitrary",)),
        name="compress_paged",
    )(page_table_flat, pool_t, w2, gmat, kg)


FLASH_ROWS = 128


def _flash_tile(qh_ref, kt_ref, v_ref, kt, tk, state, terms):
    m_ref, acc_ref = state
    start = pl.multiple_of(kt * tk, tk)
    k = kt_ref[:, pl.ds(start, tk)]
    v = v_ref[pl.ds(start, tk), :]
    tq = qh_ref.shape[1]
    blocks = [(r, pl.ds(rb * FLASH_ROWS, FLASH_ROWS))
              for r in range(HEADS_PER_GROUP) for rb in range(tq // FLASH_ROWS)]
    old = [(m_ref[r, rows, :], acc_ref[r, rows, :]) for r, rows in blocks]
    new = []
    for (r, rows), (m_old, acc_old) in zip(blocks, old):
        s = _dot(qh_ref[r, rows, :], k)
        for term in terms(r, rows):
            s = s + term
        m_new = jnp.maximum(m_old, jnp.max(s, axis=-1, keepdims=True))
        e = jnp.exp(s - jnp.concatenate([m_new] * (tk // LANES), axis=1))
        new.append((m_new, jnp.exp(m_old - m_new) * acc_old + _dot(e.astype(BF16), v)))
    for (r, rows), (m_new, acc_new) in zip(blocks, new):
        m_ref[r, rows, :] = m_new
        acc_ref[r, rows, :] = acc_new


def _nsa_prompt_kernel(tbl_ref, thr_ref, q_ref, kc_ref, ks_ref, vs_ref, kw_ref, vw_ref, gt_ref, o_ref,
                       bias_ref, edge_ref, cbias_ref, amask_ref, qh_ref, m_ref, acc_ref, *, n_top):
    g = pl.program_id(0)
    b = pl.program_id(1)
    qi = pl.program_id(2)
    tq = q_ref.shape[0]
    tk = tq
    hpg = HEADS_PER_GROUP
    nc = kc_ref.shape[0]
    ns = nc // CMP_PER_SEL
    blocks_per_tile = tk // SEL_BLOCK
    far = N_BUCKETS - 1

    qpos = qi * tq + lax.broadcasted_iota(jnp.int32, (tq, 1), 0)
    n_io = lax.broadcasted_iota(jnp.int32, (tq, nc), 1)
    relc = qpos - ((n_io + 1) * CMP_BLOCK - 1)

    n_back = WINDOW // tk

    @pl.when((b == 0) & (qi == 0))
    def _():
        diag = (lax.broadcasted_iota(jnp.int32, (tq, tk), 0)
                - lax.broadcasted_iota(jnp.int32, (tq, tk), 1))
        for r in range(hpg):
            h = g * hpg + r
            shifted = lambda k: tbl_ref[k, h] - tbl_ref[far, h]
            bias_ref[r, 0] = jnp.where(diag >= 0, _bucket_bias(diag, thr_ref, shifted), NEG)
            bias_ref[r, 1] = _bucket_bias(diag + tk, thr_ref, shifted)
        edge_ref[...] = jnp.where(diag + n_back * tk < WINDOW, 0.0, NEG)

    @pl.when(b == 0)
    def _():
        for r in range(hpg):
            h = g * hpg + r
            cbias_ref[r, qi] = _bucket_bias(relc, thr_ref, lambda k: tbl_ref[k, h])

    q = q_ref[...]
    for r in range(hpg):
        qh_ref[r] = q[:, r * HEAD_DIM:(r + 1) * HEAD_DIM]

    sel_state = (m_ref.at[0], acc_ref.at[0])
    win_state = (m_ref.at[1], acc_ref.at[1])
    m_ref[...] = jnp.full(m_ref.shape, NEG, F32)
    acc_ref[...] = jnp.zeros(acc_ref.shape, F32)

    def finish(branch):
        out = []
        for r in range(hpg):
            acc = acc_ref[branch, r]
            out.append((acc / pltpu.roll(acc, HEAD_DIM, 1))[:, :HEAD_DIM])
        return out

    def win_tile(kt, terms):
        _flash_tile(qh_ref, kw_ref, vw_ref, kt, tk, win_state, terms)

    win_tile(qi, lambda r, rows: (bias_ref[r, 0, rows, :],))
    no_prev = jnp.where(qi >= 1, 0.0, NEG)
    win_tile(jnp.maximum(qi - 1, 0), lambda r, rows: (bias_ref[r, 1, rows, :], no_prev))

    kc = kc_ref[...]
    kck = kc[:, :HEAD_DIM]
    kcv = kc[:, HEAD_DIM:]
    maskc = relc >= 0
    imp = None
    oc = []
    for r in range(hpg):
        s = _dot_nt(q[:, r * HEAD_DIM:(r + 1) * HEAD_DIM], kck) + cbias_ref[r, qi]
        sm = jnp.where(maskc, s, NEG)
        e = jnp.exp(sm - jnp.max(sm, axis=-1, keepdims=True))
        p = jnp.where(maskc, e / jnp.sum(e, axis=-1, keepdims=True), 0.0)
        imp = p if imp is None else imp + p
        oc.append(_dot(p.astype(BF16), kcv))

    pair_t = (lax.broadcasted_iota(jnp.int32, (ns, nc), 0)
              == lax.broadcasted_iota(jnp.int32, (ns, nc), 1) // CMP_PER_SEL).astype(BF16)
    hi, mid, lo = _split3(imp)
    imp_sel = (_dot_nt(pair_t, hi) + _dot_nt(pair_t, mid)) + _dot_nt(pair_t, lo)
    blk = lax.broadcasted_iota(jnp.int32, (ns, tq), 0)
    cur = (qi * tq + lax.broadcasted_iota(jnp.int32, (1, tq), 1)) // SEL_BLOCK
    forced = (blk == 0) | (blk == cur) | (blk == cur - 1)
    score = jnp.where(forced, FORCED_SCORE, jnp.where(blk <= cur, imp_sel, -1.0))
    rank = jnp.zeros((ns, tq), F32)
    for i in range(ns):
        si = score[i:i + 1, :]
        beats = (si > score) | ((si == score) & (blk > i))
        rank = rank + jnp.where(beats, 1.0, 0.0)
    sel_t = jnp.where(rank < n_top, 1.0, 0.0).astype(BF16)
    eye = (lax.broadcasted_iota(jnp.int32, (tq, tq), 0)
           == lax.broadcasted_iota(jnp.int32, (tq, tq), 1)).astype(BF16)
    sel = _dot_nt(eye, sel_t).astype(BF16)

    def sel_tile(kt, tiles_back):
        expand = (lax.broadcasted_iota(jnp.int32, (ns, tk), 0)
                  == kt * blocks_per_tile + lax.broadcasted_iota(jnp.int32, (ns, tk), 1) // SEL_BLOCK)
        amask_ref[...] = jnp.where(_dot(sel, expand.astype(BF16)) > 0.5, 0.0, NEG)
        if tiles_back is None:
            terms = lambda r, rows: (amask_ref[rows, :],)
        else:
            terms = lambda r, rows: (bias_ref[r, tiles_back, rows, :], amask_ref[rows, :])
        _flash_tile(qh_ref, ks_ref, vs_ref, kt, tk, sel_state, terms)

    def far_tile(kt, c):
        sel_tile(kt, None)
        return c
    lax.fori_loop(0, jnp.maximum(qi - 1, 0), far_tile, 0)

    @pl.when(qi >= n_back)
    def _():
        sel_tile(qi - 1, 1)
        win_tile(qi - n_back, lambda r, rows: (edge_ref[rows, :],))

    @pl.when((qi >= 1) & (qi < n_back))
    def _():
        sel_tile(qi - 1, 1)

    sel_tile(qi, 0)
    o_sel = finish(0)
    o_win = finish(1)

    width = hpg * HEAD_DIM
    idx = lax.broadcasted_iota(jnp.int32, (LANES, N_BRANCHES * width), 0)
    col = lax.broadcasted_iota(jnp.int32, (LANES, N_BRANCHES * width), 1)
    spread = (idx == (col // width) * hpg + (col % width) // HEAD_DIM).astype(BF16)
    gates = _exact_dot(gt_ref[...], spread)
    out = None
    for c, branch in enumerate((oc, o_sel, o_win)):
        term = gates[:, c * width:(c + 1) * width] * jnp.concatenate(branch, axis=1)
        out = term if out is None else out + term
    o_ref[...] = out.astype(o_ref.dtype)


def _nsa_prompt_attn(tbl, thr, q, kcn, ksn_t, kwn_t, v_rows, gates, b, s):
    tq = min(ATTN_TILE, s)
    assert WINDOW == 2 * tq and tq + 1 >= MAX_DISTANCE and tq % FLASH_ROWS == 0
    nq = s // tq
    nc = s // CMP_BLOCK
    ns = s // SEL_BLOCK
    n_top = min(N_SELECTED, ns)
    hpg = HEADS_PER_GROUP
    k_spec = pl.BlockSpec((None, HEAD_DIM, s), lambda g, i, j: (i, 2 * g, 0))
    v_spec = lambda branch: pl.BlockSpec((s, LANES), lambda g, i, j: (i, branch * N_KV_HEADS + g))
    return pl.pallas_call(
        functools.partial(_nsa_prompt_kernel, n_top=n_top),
        grid=(N_KV_HEADS, b, nq),
        in_specs=[
            _smem_spec(), _smem_spec(),
            pl.BlockSpec((tq, Q_GROUP_COLS), lambda g, i, j: (i * nq + j, g)),
            pl.BlockSpec((nc, GROUP_COLS), lambda g, i, j: (i, g)),
            k_spec, v_spec(0), k_spec, v_spec(1),
            pl.BlockSpec((tq, LANES), lambda g, i, j: (i * nq + j, g)),
        ],
        out_specs=pl.BlockSpec((tq, Q_GROUP_COLS), lambda g, i, j: (i * nq + j, g)),
        out_shape=jax.ShapeDtypeStruct((b * s, D_MODEL), BF16),
        scratch_shapes=[pltpu.VMEM((hpg, 2, tq, tq), F32),
                        pltpu.VMEM((tq, tq), F32),
                        pltpu.VMEM((hpg, nq, tq, nc), F32),
                        pltpu.VMEM((tq, tq), F32),
                        pltpu.VMEM((hpg, tq, HEAD_DIM), BF16),
                        pltpu.VMEM((2, hpg, tq, LANES), F32),
                        pltpu.VMEM((2, hpg, tq, LANES), F32)],
        compiler_params=_cparams(("arbitrary", "arbitrary", "arbitrary")),
        name="nsa_prompt_attn",
    )(tbl, thr, q, kcn, ksn_t, v_rows, kwn_t, v_rows, gates)


def _bucket_onehot(dist_row, thr_ref):
    n = dist_row.shape[1]
    bucket = jnp.zeros(dist_row.shape, jnp.int32)
    for k in range(1, N_BUCKETS):
        bucket = bucket + jnp.where(dist_row >= thr_ref[k], 1, 0)
    return (lax.broadcasted_iota(jnp.int32, (N_BUCKETS, n), 0) == bucket).astype(BF16)


def _sample_cmp_kernel(thr_ref, q_ref, kc_ref, tbl_ref, oc_ref, idx_ref, *, past_len, seq, n_top):
    nc = kc_ref.shape[0]
    ns = past_len // SEL_BLOCK
    hpg = HEADS_PER_GROUP
    rows = hpg * seq
    kc = kc_ref[...]
    pair = (lax.broadcasted_iota(jnp.int32, (nc, ns), 0) // CMP_PER_SEL
            == lax.broadcasted_iota(jnp.int32, (nc, ns), 1)).astype(BF16)
    eye = (lax.broadcasted_iota(jnp.int32, (ns, ns), 0)
           == lax.broadcasted_iota(jnp.int32, (ns, ns), 1))
    eye_bf = eye.astype(BF16)
    ii = lax.broadcasted_iota(jnp.int32, (ns, ns), 0)
    jj = lax.broadcasted_iota(jnp.int32, (ns, ns), 1)
    t_row = lax.broadcasted_iota(jnp.int32, (rows, nc), 0) % seq
    n_io = lax.broadcasted_iota(jnp.int32, (rows, nc), 1)
    rel = past_len + t_row - ((n_io + 1) * CMP_BLOCK - 1)
    mask = rel >= 0
    blk = lax.broadcasted_iota(jnp.int32, (SUBLANES, ns), 1)
    forced = (blk == 0) | (blk == ns - 1)
    rank_lane = lax.broadcasted_iota(jnp.int32, (ns, LANES), 1).astype(F32)
    blk_col = lax.broadcasted_iota(jnp.int32, (ns, LANES), 0).astype(F32)
    for g in range(N_KV_HEADS):
        kck = kc[:, g * GROUP_COLS:g * GROUP_COLS + HEAD_DIM]
        kcv = kc[:, g * GROUP_COLS + HEAD_DIM:(g + 1) * GROUP_COLS]
        qg = q_ref[g].astype(BF16)
        tbl_g = tbl_ref[g]
        s = _dot_nt(qg, kck)
        s = s + _bucket_bias(rel, thr_ref, lambda k: tbl_g[:, k:k + 1])
        sm = jnp.where(mask, s, NEG)
        e = jnp.exp(sm - jnp.max(sm, axis=-1, keepdims=True))
        p = jnp.where(mask, e / jnp.sum(e, axis=-1, keepdims=True), 0.0)
        oc_ref[g] = _dot(p.astype(BF16), kcv)
        imp = p[0:seq]
        for r in range(1, hpg):
            imp = imp + p[r * seq:(r + 1) * seq]
        imp = jnp.concatenate([imp, jnp.zeros((SUBLANES - seq, nc), F32)], axis=0)
        score = jnp.where(forced, FORCED_SCORE, _exact_dot(imp, pair))
        hi, mid, lo = _split3(score)
        score_t = (_dot_nt(eye_bf, hi) + _dot_nt(eye_bf, mid)) + _dot_nt(eye_bf, lo)
        for t in range(seq):
            s_row = score[t:t + 1, :]
            s_col = score_t[:, t:t + 1]
            beats = (ii != jj) & ((s_row > s_col) | ((s_row == s_col) & (jj < ii)))
            rank_col = jnp.sum(jnp.where(beats, 1.0, 0.0), axis=1, keepdims=True)
            onehot = rank_col == rank_lane
            idx_row = jnp.sum(jnp.where(onehot, blk_col, 0.0), axis=0, keepdims=True)
            idx_ref[g * seq + t:g * seq + t + 1, :] = idx_row.astype(jnp.int32)


def _sample_cmp(thr, q_rt, kcn, tbl_rt, b, past_len, seq):
    nc = kcn.shape[0] // b
    ns = past_len // SEL_BLOCK
    n_top = min(N_SELECTED - 1, ns)
    rows = HEADS_PER_GROUP * seq
    return pl.pallas_call(
        functools.partial(_sample_cmp_kernel, past_len=past_len, seq=seq, n_top=n_top),
        grid=(b,),
        in_specs=[
            _smem_spec(),
            pl.BlockSpec((None, N_KV_HEADS, rows, HEAD_DIM), lambda i: (i, 0, 0, 0)),
            pl.BlockSpec((nc, KV_COLS), lambda i: (i, 0)),
            _const_spec(tbl_rt.shape),
        ],
        out_specs=[
            pl.BlockSpec((None, N_KV_HEADS, rows, HEAD_DIM), lambda i: (i, 0, 0, 0)),
            pl.BlockSpec((None, N_KV_HEADS * seq, LANES), lambda i: (i, 0, 0)),
        ],
        out_shape=[
            jax.ShapeDtypeStruct((b, N_KV_HEADS, rows, HEAD_DIM), F32),
            jax.ShapeDtypeStruct((b, N_KV_HEADS * seq, LANES), jnp.int32),
        ],
        compiler_params=_cparams(("arbitrary",)),
        name="sample_cmp",
    )(thr, q_rt, kcn, tbl_rt)


def _norm_k_cols(kt, gain_col):
    ms = jnp.mean(kt * kt, axis=0, keepdims=True)
    return (kt * lax.rsqrt(ms + EPS) * gain_col).astype(BF16)


def _sample_sel_win_kernel(idx_ref, pt_ref, thr_ref, pool_ref, q_ref, ksnew_ref, kwnew_ref,
                           win_ref, tbl_ref, kgs_ref, kgw_ref, os_ref, ow_ref,
                           buf_ref, sem_ref, *, past_len, seq, n_top, page):
    i = pl.program_id(0)
    n_pages = past_len // page
    blocks_per_page = page // SEL_BLOCK
    n_keys = n_top * page
    stride = n_top + 1

    def block_of(g, t, j):
        return idx_ref[(i * N_KV_HEADS * seq + g * seq + t) * stride + j]

    def tile_copy(g, t, j):
        phys = pt_ref[i * n_pages + block_of(g, t, j) // blocks_per_page]
        return pltpu.make_async_copy(
            pool_ref.at[phys, g],
            buf_ref.at[g * seq + t, :, pl.ds(pl.multiple_of(j * page, page), page)],
            sem_ref.at[0])

    def for_all_tiles(fn):
        for g in range(N_KV_HEADS):
            for t in range(seq):
                def per_tile(j, c):
                    fn(tile_copy(g, t, j))
                    return c
                lax.fori_loop(0, n_top, per_tile, 0)

    for_all_tiles(lambda cp: cp.start())

    wb = win_ref.shape[2]
    wk = [_norm_k_cols(win_ref[g, 0:HEAD_DIM, :], kgw_ref[...]) for g in range(N_KV_HEADS)]
    wv = [win_ref[g, HEAD_DIM:GROUP_COLS, :].astype(BF16) for g in range(N_KV_HEADS)]
    j_io = lax.broadcasted_iota(jnp.int32, (1, wb), 1)
    tnew = lax.broadcasted_iota(jnp.int32, (1, SUBLANES), 1)

    def attend(q, tbl_g, k_past, v_past, buckets, mask, knew, bucketsn, maskn):
        s1 = jnp.where(mask, _dot(q, k_past) + _exact_dot(tbl_g, buckets), NEG)
        s2 = jnp.where(maskn, _dot_nt(q, knew[:, :HEAD_DIM].astype(BF16))
                       + _exact_dot(tbl_g, bucketsn), NEG)
        m = jnp.maximum(jnp.max(s1, axis=-1, keepdims=True), jnp.max(s2, axis=-1, keepdims=True))
        e1 = jnp.where(mask, jnp.exp(s1 - m), 0.0)
        e2 = jnp.where(maskn, jnp.exp(s2 - m), 0.0)
        l = jnp.sum(e1, axis=-1, keepdims=True) + jnp.sum(e2, axis=-1, keepdims=True)
        o = _dot_nt(e1.astype(BF16), v_past) + _dot(e2.astype(BF16),
                                                    knew[:, HEAD_DIM:].astype(BF16))
        return o / l

    def window(t, c):
        dist = wb + t - j_io
        maskw = (dist >= 0) & (dist < WINDOW)
        distn = t - tnew
        maskn = (distn >= 0) & (distn < WINDOW)
        buckets = _bucket_onehot(dist, thr_ref)
        bucketsn = _bucket_onehot(distn, thr_ref)
        for g in range(N_KV_HEADS):
            ow_ref[g, t] = attend(q_ref[g, t].astype(BF16), tbl_ref[g], wk[g], wv[g], buckets,
                                  maskw, kwnew_ref[g], bucketsn, maskn)
        return c
    lax.fori_loop(0, seq, window, 0)

    for_all_tiles(lambda cp: cp.wait())

    lane = lax.broadcasted_iota(jnp.int32, (1, n_keys), 1)
    tile_of_lane = lane // page
    row_in_page = lane % page

    def selected(t, c):
        distn = t - tnew
        maskn = distn >= 0
        bucketsn = _bucket_onehot(distn, thr_ref)
        for g in range(N_KV_HEADS):
            kvt = buf_ref[g * seq + t]
            page_base = jnp.zeros((1, n_keys), jnp.int32)
            half = jnp.zeros((1, n_keys), jnp.int32)
            for j in range(n_top):
                blk = block_of(g, t, j)
                page_base = jnp.where(tile_of_lane == j, (blk // blocks_per_page) * page, page_base)
                half = jnp.where(tile_of_lane == j, blk % blocks_per_page, half)
            mask = (row_in_page // SEL_BLOCK) == half
            dist = past_len + t - (page_base + row_in_page)
            os_ref[g, t] = attend(q_ref[g, t].astype(BF16), tbl_ref[g],
                                  _norm_k_cols(kvt[0:HEAD_DIM], kgs_ref[...]),
                                  kvt[HEAD_DIM:GROUP_COLS].astype(BF16),
                                  _bucket_onehot(dist, thr_ref), mask,
                                  ksnew_ref[g], bucketsn, maskn)
        return c
    lax.fori_loop(0, seq, selected, 0)


def _sample_sel_win(idx_flat, pt_flat, thr, pool_t, q_tr, ksnew, kwnew, win_t, tbl_r8,
                    kgs_col, kgw_col, b, past_len, seq):
    ns = past_len // SEL_BLOCK
    n_top = min(N_SELECTED - 1, ns)
    page = pool_t.shape[3]
    wb = win_t.shape[3]
    o_shape = jax.ShapeDtypeStruct((b, N_KV_HEADS, seq, SUBLANES, HEAD_DIM), F32)
    o_spec = pl.BlockSpec((None, N_KV_HEADS, seq, SUBLANES, HEAD_DIM),
                          lambda i, *_: (i, 0, 0, 0, 0))
    new_spec = pl.BlockSpec((None, N_KV_HEADS, SUBLANES, GROUP_COLS), lambda i, *_: (i, 0, 0, 0))
    grid_spec = pltpu.PrefetchScalarGridSpec(
        num_scalar_prefetch=3,
        grid=(b,),
        in_specs=[
            pl.BlockSpec(memory_space=pl.ANY),
            pl.BlockSpec((None, N_KV_HEADS, seq, SUBLANES, HEAD_DIM), lambda i, *_: (i, 0, 0, 0, 0)),
            new_spec, new_spec,
            pl.BlockSpec((None, N_KV_HEADS, GROUP_COLS, wb), lambda i, *_: (i, 0, 0, 0)),
            pl.BlockSpec(tbl_r8.shape, lambda i, *_: (0, 0, 0)),
            pl.BlockSpec(kgs_col.shape, lambda i, *_: (0, 0)),
            pl.BlockSpec(kgw_col.shape, lambda i, *_: (0, 0)),
        ],
        out_specs=[o_spec, o_spec],
        scratch_shapes=[pltpu.VMEM((N_KV_HEADS * seq, GROUP_COLS, n_top * page), F32),
                        pltpu.SemaphoreType.DMA((1,))],
    )
    return pl.pallas_call(
        functools.partial(_sample_sel_win_kernel, past_len=past_len, seq=seq, n_top=n_top,
                          page=page),
        grid_spec=grid_spec,
        out_shape=[o_shape, o_shape],
        compiler_params=_cparams(("arbitrary",)),
        name="sample_sel_win",
    )(idx_flat, pt_flat, thr, pool_t, q_tr, ksnew, kwnew, win_t, tbl_r8, kgs_col, kgw_col)


def _route(f, rhi_ref, rlo_ref):
    f_hi = f.astype(BF16)
    f_lo = (f - f_hi.astype(F32)).astype(BF16)
    logits = (_dot(f_hi, rhi_ref[...]) + _dot(f_lo, rhi_ref[...])) + _dot(f_hi, rlo_ref[...])
    lane = lax.broadcasted_iota(jnp.int32, logits.shape, 1).astype(F32)
    logits = jnp.where(lane < N_EXPERTS, logits, -jnp.inf)
    m1 = jnp.max(logits, axis=-1, keepdims=True)
    i1 = jnp.min(jnp.where(logits == m1, lane, float(LANES)), axis=-1, keepdims=True)
    rest = jnp.where(lane == i1, -jnp.inf, logits)
    m2 = jnp.max(rest, axis=-1, keepdims=True)
    i2 = jnp.min(jnp.where(rest == m2, lane, float(LANES)), axis=-1, keepdims=True)
    e2 = jnp.exp(m2 - m1)
    denom = 1.0 + e2
    return jnp.where(lane == i1, 1.0 / denom, 0.0) + jnp.where(lane == i2, e2 / denom, 0.0)


def _attn_out_kernel(h_ref, o_ref, wout_ref, gf_ref, rhi_ref, rlo_ref, h2_ref, f_ref, gate_ref):
    h2 = h_ref[...] + _dot(o_ref[...], wout_ref[...])
    h2_ref[...] = h2
    f = _rmsnorm(h2, gf_ref[...])
    f_ref[...] = f.astype(f_ref.dtype)
    gate_ref[...] = _route(f, rhi_ref, rlo_ref)


def _attn_out_merge_kernel(h_ref, oc_ref, os_ref, ow_ref, gt_ref, wout_ref, gf_ref, rhi_ref,
                           rlo_ref, h2_ref, f_ref, gate_ref):
    gates = gt_ref[...]
    n_gate = gates.shape[1]
    hpg = HEADS_PER_GROUP
    idx = lax.broadcasted_iota(jnp.int32, (n_gate, D_MODEL), 0)
    head = lax.broadcasted_iota(jnp.int32, (n_gate, D_MODEL), 1) // HEAD_DIM
    o = None
    for c, ref in enumerate((oc_ref, os_ref, ow_ref)):
        lane_of_gate = (head // hpg) * LANES + c * hpg + head % hpg
        gate = _exact_dot(gates, (idx == lane_of_gate).astype(BF16))
        term = gate * ref[...]
        o = term if o is None else o + term
    h2 = h_ref[...] + _dot(o.astype(BF16), wout_ref[...])
    h2_ref[...] = h2
    f = _rmsnorm(h2, gf_ref[...])
    f_ref[...] = f.astype(f_ref.dtype)
    gate_ref[...] = _route(f, rhi_ref, rlo_ref)


def _attn_out(h2d, o_list, gates, wout, gf, rhi, rlo):
    n, d = h2d.shape
    tm = min(TOKEN_TILE, n)
    row = lambda w: pl.BlockSpec((tm, w), lambda i: (i, 0))
    merged = len(o_list) == 1
    body = _attn_out_kernel if merged else _attn_out_merge_kernel
    in_specs = [row(d)] + [row(d)] * len(o_list) + ([] if merged else [row(gates.shape[1])])
    in_specs += [_const_spec(wout.shape), _const_spec((1, d)), _const_spec(rhi.shape),
                 _const_spec(rlo.shape)]
    args = [h2d] + list(o_list) + ([] if merged else [gates]) + [wout, gf, rhi, rlo]
    return pl.pallas_call(
        body,
        grid=(n // tm,),
        in_specs=in_specs,
        out_specs=[row(d), row(d), row(LANES)],
        out_shape=[jax.ShapeDtypeStruct((n, d), F32), jax.ShapeDtypeStruct((n, d), BF16),
                   jax.ShapeDtypeStruct((n, LANES), F32)],
        compiler_params=_cparams(("arbitrary",)),
        name="attn_out_route",
    )(*args)


def _moe_kernel(f_ref, h2_ref, gate_ref, p_ref, wg_ref, wu_ref, wd_ref, gp_ref, wpg_ref, wpp_ref,
                o_ref, acc_ref):
    e = pl.program_id(1)

    @pl.when(e == 0)
    def _():
        acc_ref[...] = jnp.zeros_like(acc_ref)

    f = f_ref[...]
    gate = gate_ref[...]
    lane = lax.broadcasted_iota(jnp.int32, gate.shape, 1)
    ge = jnp.sum(jnp.where(lane == e, gate, 0.0), axis=-1, keepdims=True)
    hid = _silu(_dot(f, wg_ref[...])) * _dot(f, wu_ref[...])
    acc_ref[...] += ge * _dot(hid.astype(BF16), wd_ref[...])

    @pl.when(e == pl.num_programs(1) - 1)
    def _():
        o_ref[...] = _ple(h2_ref[...] + acc_ref[...], p_ref[...], gp_ref[...], wpg_ref, wpp_ref)


def _moe(f2d, h2, gate, p2d, wg, wu, wd, gp, wpg, wpp):
    n, d = h2.shape
    tm = min(TOKEN_TILE, n)
    n_e, _, dff = wg.shape
    row = lambda w: pl.BlockSpec((tm, w), lambda i, e: (i, 0))
    return pl.pallas_call(
        _moe_kernel,
        grid=(n // tm, n_e),
        in_specs=[
            row(d), row(d), row(LANES), row(p2d.shape[1]),
            pl.BlockSpec((None, d, dff), lambda i, e: (e, 0, 0)),
            pl.BlockSpec((None, d, dff), lambda i, e: (e, 0, 0)),
            pl.BlockSpec((None, dff, d), lambda i, e: (e, 0, 0)),
            _const_spec((1, d)), _const_spec(wpg.shape), _const_spec(wpp.shape),
        ],
        out_specs=row(d),
        out_shape=jax.ShapeDtypeStruct((n, d), F32),
        scratch_shapes=[pltpu.VMEM((tm, d), F32)],
        compiler_params=_cparams(("arbitrary", "arbitrary")),
        name="moe_ple",
    )(f2d, h2, gate, p2d, wg, wu, wd, gp, wpg, wpp)


def _bucket_thresholds():
    n = jnp.arange(MAX_DISTANCE + 1)
    max_exact = N_BUCKETS // 2
    nf = jnp.maximum(n, 1).astype(F32)
    large = max_exact + (jnp.log(nf / max_exact) / math.log(MAX_DISTANCE / max_exact)
                         * (N_BUCKETS - max_exact)).astype(jnp.int32)
    bucket = jnp.where(n < max_exact, n, jnp.minimum(large, N_BUCKETS - 1))
    return jnp.sum(bucket[None, :] < jnp.arange(N_BUCKETS)[:, None], axis=1).astype(jnp.int32)


def _row(v):
    return v.reshape(1, -1).astype(F32)


def _key_gain_row(k_gain, cols):
    g = jnp.concatenate([k_gain, jnp.ones_like(k_gain)])
    return jnp.tile(g, cols // GROUP_COLS).reshape(1, cols).astype(F32)


def _rows_on_lanes(x):
    lead = x.shape[:-4]
    rows = x.shape[-4]
    nl = len(lead)
    perm = tuple(range(nl)) + (nl + 1, nl + 2, nl + 3, nl)
    return jnp.transpose(x, perm).reshape(*lead, N_KV_HEADS, GROUP_COLS, rows)


def _rows_on_lanes_inverse(xt):
    b, _, rows = xt.shape
    x = xt.reshape(b, N_KV_HEADS, 2, HEAD_DIM, rows)
    return jnp.transpose(x, (0, 4, 1, 2, 3))[None]


def kernel(x_prompt, x_sample, state_conv, cache_cmp, cache_sel, state_win, page_table,
           p_prompt, p_sample, norm_mix, norm_ffn, norm_ple, conv_w_in, conv_w, conv_w_out,
           nsa_w_in, nsa_w_cmp, nsa_q_norm, nsa_k_norm, nsa_w_out, rel_bias,
           ffn_w_gate, ffn_w_up, ffn_w_down, moe_router, moe_w_gate, moe_w_up, moe_w_down,
           ple_w_proj, ple_w_gate):
    bp, sp, d = x_prompt.shape
    bs, ss, _ = x_sample.shape
    n_p = bp * sp
    n_s = bs * ss
    page = cache_cmp.shape[2]
    past_len = page_table.shape[1] * page
    hpg = HEADS_PER_GROUP
    bf = lambda w: w.astype(BF16)

    cw = jnp.zeros((SUBLANES, d), F32).at[:CONV_WIDTH].set(conv_w[0])
    w_in0, w_out0 = bf(conv_w_in[0]), bf(conv_w_out[0])
    g_mix0 = _row(norm_mix[0])
    h_p, tail_p = _mix0_prompt(x_prompt, g_mix0, w_in0, cw, w_out0)
    conv_prompt = tail_p[:, SUBLANES - (CONV_WIDTH - 1):][None]

    st = state_conv[0]
    zeros = jnp.zeros((bs, ss, d), F32)
    s1 = zeros.at[:, 0].set(st[:, 1]).reshape(n_s, d)
    s2 = zeros.at[:, 0].set(st[:, 0]).at[:, 1].set(st[:, 1]).reshape(n_s, d)
    h_s, u_s = _mix0_sample(x_sample.reshape(n_s, d), g_mix0, w_in0, cw, w_out0, s1, s2, ss)
    conv_sample = u_s.reshape(bs, ss, d)[:, ss - (CONV_WIDTH - 1):][None]

    ffn0_w = (_row(norm_ffn[0]), bf(ffn_w_gate[0]), bf(ffn_w_up[0]), bf(ffn_w_down[0]),
              _row(norm_ple[0]), bf(ple_w_gate[0]), bf(ple_w_proj[0]))
    h_p = _ffn0(h_p.reshape(n_p, d), p_prompt[0].reshape(n_p, -1), *ffn0_w)
    h_s = _ffn0(h_s, p_sample[0].reshape(n_s, -1), *ffn0_w)

    q_cols = N_HEADS * HEAD_DIM
    kv_cols = N_BRANCHES * KV_COLS
    w_in1 = nsa_w_in[0]
    wq = bf(w_in1[:, :q_cols])
    wkv = w_in1[:, q_cols:q_cols + kv_cols]
    wkc = bf(wkv[:, :KV_COLS])
    wkvt = bf(wkv.T)
    wv_src = wkv[:, KV_COLS:].reshape(d, 2 * N_KV_HEADS, 2, HEAD_DIM)[:, :, 1]
    wv = bf(jnp.zeros((d, 2 * N_KV_HEADS, LANES), F32).at[:, :, :HEAD_DIM].set(wv_src).reshape(
        d, 2 * KV_COLS))
    wg_src = w_in1[:, q_cols + kv_cols:].reshape(d, N_BRANCHES, N_KV_HEADS, hpg)
    wgt = jnp.zeros((d, N_KV_HEADS, LANES), F32).at[:, :, :N_BRANCHES * hpg].set(
        wg_src.transpose(0, 2, 1, 3).reshape(d, N_KV_HEADS, N_BRANCHES * hpg))
    wgt = bf(wgt.reshape(d, N_KV_HEADS * LANES))
    gi = jnp.arange(MXU_DIM) // HEAD_DIM
    gmat = (gi[:, None] == gi[None, :]).astype(BF16)
    qg = jnp.tile(nsa_q_norm[0], N_HEADS).reshape(1, q_cols).astype(F32)
    kgc = _key_gain_row(nsa_k_norm[0, 0], KV_COLS)
    kgs_col = nsa_k_norm[0, 1].reshape(HEAD_DIM, 1).astype(F32)
    kgw_col = nsa_k_norm[0, 2].reshape(HEAD_DIM, 1).astype(F32)
    proj_w = (_row(norm_mix[1]), wq, wkc, wkvt, wv, wgt, gmat, qg, kgs_col, kgw_col)

    per_tile = MXU_DIM // HEAD_DIM
    wc = nsa_w_cmp[0].reshape(CMP_BLOCK, 2, per_tile, HEAD_DIM, HEAD_DIM)
    w2 = bf(jnp.einsum('lhade,ab->lhadbe', wc, jnp.eye(per_tile, dtype=F32)).reshape(
        CMP_BLOCK, 2, MXU_DIM, MXU_DIM))

    thr = _bucket_thresholds()
    tbl = rel_bias.astype(F32)

    q_p, kvc_p, kvct_p, kvst_p, kvwt_p, ksn_p, kwn_p, vr_p, gt_p = _nsa_proj(h_p, bp, *proj_w)
    kcn_p = _compress_prompt(kvc_p, w2, gmat, kgc)
    o_p = _nsa_prompt_attn(tbl, thr, q_p, kcn_p, ksn_p, kwn_p, vr_p, gt_p, bp, sp)

    q_s, _, kvct_s, kvst_s, kvwt_s, ksn_s, kwn_s, _, gt_s = _nsa_proj(h_s, 1, *proj_w)
    pt_flat = page_table.reshape(-1).astype(jnp.int32)
    kcn_s = _compress_paged(pt_flat, _rows_on_lanes(cache_cmp[0]), w2, gmat, kgc, PAGES_PER_STEP)
    q5 = q_s.astype(F32).reshape(bs, ss, N_KV_HEADS, hpg, HEAD_DIM)
    q_rt = q5.transpose(0, 2, 3, 1, 4).reshape(bs, N_KV_HEADS, hpg * ss, HEAD_DIM)
    q_tr = jnp.zeros((bs, N_KV_HEADS, ss, SUBLANES, HEAD_DIM), F32).at[:, :, :, :hpg].set(
        q5.transpose(0, 2, 1, 3, 4))
    tbl_gr = tbl.T.reshape(N_KV_HEADS, hpg, N_BUCKETS)
    tbl_rt = jnp.repeat(tbl_gr, ss, axis=1)
    tbl_r8 = jnp.zeros((N_KV_HEADS, SUBLANES, N_BUCKETS), F32).at[:, :hpg].set(tbl_gr)
    oc_s, idx = _sample_cmp(thr, q_rt, kcn_s, tbl_rt, bs, past_len, ss)
    n_top_s = min(N_SELECTED - 1, past_len // SEL_BLOCK)
    idx_flat = idx[:, :, :n_top_s + 1].reshape(-1)

    def new_rows(kn_t):
        r = kn_t[0].astype(F32).reshape(N_KV_HEADS, GROUP_COLS, bs, ss).transpose(2, 0, 3, 1)
        return jnp.zeros((bs, N_KV_HEADS, SUBLANES, GROUP_COLS), F32).at[:, :, :ss].set(r)

    win_t = _rows_on_lanes(state_win[0])
    os_s, ow_s = _sample_sel_win(idx_flat, pt_flat, thr, _rows_on_lanes(cache_sel[0]), q_tr,
                                 new_rows(ksn_s), new_rows(kwn_s), win_t, tbl_r8,
                                 kgs_col, kgw_col, bs, past_len, ss)
    oc_s2 = oc_s.reshape(bs, N_KV_HEADS, hpg, ss, HEAD_DIM).transpose(0, 3, 1, 2, 4).reshape(n_s, d)
    to2d = lambda o: o[:, :, :, :hpg].transpose(0, 2, 1, 3, 4).reshape(n_s, d)

    w_out1 = bf(nsa_w_out[0])
    router = jnp.zeros((d, LANES), F32).at[:, :N_EXPERTS].set(moe_router[0])
    r_hi = bf(router)
    r_lo = bf(router - r_hi.astype(F32))
    gf1 = _row(norm_ffn[1])
    moe_w = (bf(moe_w_gate[0]), bf(moe_w_up[0]), bf(moe_w_down[0]), _row(norm_ple[1]),
             bf(ple_w_gate[1]), bf(ple_w_proj[1]))
    h2_p, f_p, gate_p = _attn_out(h_p, [o_p], None, w_out1, gf1, r_hi, r_lo)
    y_p = _moe(f_p, h2_p, gate_p, p_prompt[1].reshape(n_p, -1), *moe_w)
    h2_s, f_s, gate_s = _attn_out(h_s, [oc_s2, to2d(os_s), to2d(ow_s)], gt_s, w_out1, gf1,
                                  r_hi, r_lo)
    y_s = _moe(f_s, h2_s, gate_s, p_sample[1].reshape(n_s, -1), *moe_w)

    sample_rows = lambda xt: _rows_on_lanes_inverse(
        xt[0].reshape(KV_COLS, bs, ss).transpose(1, 0, 2))
    wbp = min(WINDOW, sp)
    win_s_t = jnp.concatenate(
        [win_t, kvwt_s[0].reshape(N_KV_HEADS, GROUP_COLS, bs, ss).transpose(2, 0, 1, 3)],
        axis=3)[..., ss:]
    return (y_p.reshape(bp, sp, d), y_s.reshape(bs, ss, d), conv_prompt, conv_sample,
            _rows_on_lanes_inverse(kvct_p), sample_rows(kvct_s),
            _rows_on_lanes_inverse(kvst_p), sample_rows(kvst_s),
            _rows_on_lanes_inverse(kvwt_p[:, :, sp - wbp:]),
            _rows_on_lanes_inverse(win_s_t.reshape(bs, KV_COLS, -1)))
```

```python
import functools
import math

import jax
import jax.numpy as jnp
from jax import lax
from jax.experimental import pallas as pl
from jax.experimental.pallas import tpu as pltpu

F32 = jnp.float32
BF16 = jnp.bfloat16

D_MODEL = 1024
N_HEADS = 16
HEAD_DIM = 64
N_KV_HEADS = 4
HEADS_PER_GROUP = 4
N_BRANCHES = 3
CMP_BLOCK = 32
SEL_BLOCK = 64
CMP_PER_SEL = SEL_BLOCK // CMP_BLOCK
N_SELECTED = 16
WINDOW = 512
N_BUCKETS = 32
MAX_DISTANCE = 128
N_EXPERTS = 8
CONV_WIDTH = 3
EPS = 1e-6
FORCED_SCORE = 1e4
NEG = -1e30
KV_COLS = N_KV_HEADS * 2 * HEAD_DIM
GROUP_COLS = 2 * HEAD_DIM
Q_GROUP_COLS = HEADS_PER_GROUP * HEAD_DIM

LANES = 128
SUBLANES = 8
MXU_DIM = 256
VMEM_LIMIT = 56 * 1024 * 1024
SLABS = KV_COLS // LANES

TOKEN_TILE = 512
ATTN_TILE = 256
PAGES_PER_STEP = 32


def _cparams(sem, vmem=VMEM_LIMIT):
    return pltpu.CompilerParams(dimension_semantics=sem, vmem_limit_bytes=vmem)


def _const_spec(shape):
    nd = len(shape)
    return pl.BlockSpec(shape, lambda *_: (0,) * nd, pipeline_mode=pl.Buffered(1))


def _smem_spec():
    return pl.BlockSpec(memory_space=pltpu.SMEM)


def _dot(a, b):
    return jnp.dot(a, b, preferred_element_type=F32)


def _dot_nt(a, b):
    return lax.dot_general(a, b, (((1,), (1,)), ((), ())), preferred_element_type=F32)


def _split3(x):
    hi = x.astype(BF16)
    r1 = x - hi.astype(F32)
    mid = r1.astype(BF16)
    lo = (r1 - mid.astype(F32)).astype(BF16)
    return hi, mid, lo


def _exact_dot(x, m01):
    hi, mid, lo = _split3(x)
    return (_dot(hi, m01) + _dot(mid, m01)) + _dot(lo, m01)


def _rmsnorm(x, g):
    ms = jnp.mean(x * x, axis=-1, keepdims=True)
    return x * lax.rsqrt(ms + EPS) * g


def _group_mean_sq(x, gmat):
    n = x.shape[1]
    w = gmat.shape[0]
    outs = []
    for c in range(n // w):
        blk = x[:, c * w:(c + 1) * w]
        sq = blk * blk
        hi = sq.astype(BF16)
        lo = (sq - hi.astype(F32)).astype(BF16)
        outs.append(_dot(hi, gmat) + _dot(lo, gmat))
    out = outs[0] if len(outs) == 1 else jnp.concatenate(outs, axis=1)
    return out * (1.0 / HEAD_DIM)


def _norm_keys(kv, gain_row, gmat):
    ms = _group_mean_sq(kv, gmat)
    lane = lax.broadcasted_iota(jnp.int32, kv.shape, 1)
    is_k = (lane % GROUP_COLS) < HEAD_DIM
    return jnp.where(is_k, kv * lax.rsqrt(ms + EPS) * gain_row, kv)


def _bucket_bias(dist, thr_ref, value_of_bucket):
    val = value_of_bucket(0)
    val = jnp.broadcast_to(val, dist.shape).astype(F32)
    for k in range(1, N_BUCKETS):
        val = jnp.where(dist >= thr_ref[k], value_of_bucket(k), val)
    return val


def _silu(x):
    return x * jax.nn.sigmoid(x)


def _ple(h, p, gain, wpg_ref, wpp_ref):
    r = _rmsnorm(h, gain).astype(BF16)
    g = jax.nn.sigmoid(_dot(r, wpg_ref[...]))
    return h + g * _dot(p.astype(BF16), wpp_ref[...])


def _conv_mix_tail(x, gb, u, um1, um2, cw_ref, wout_ref):
    conv = cw_ref[0:1, :] * um2 + cw_ref[1:2, :] * um1 + cw_ref[2:3, :] * u
    y = _dot((gb * conv).astype(BF16), wout_ref[...])
    return x + y


def _mix0_prompt_kernel(x_ref, g_ref, win_ref, cw_ref, wout_ref, h_ref, st_ref, carry_ref):
    j = pl.program_id(1)
    tm = x_ref.shape[0]

    @pl.when(j == 0)
    def _():
        carry_ref[...] = jnp.zeros_like(carry_ref)

    x = x_ref[...]
    a = _rmsnorm(x, g_ref[...]).astype(BF16)
    proj = _dot(a, win_ref[...])
    gb = proj[:, :D_MODEL]
    u = proj[:, D_MODEL:2 * D_MODEL] * proj[:, 2 * D_MODEL:]
    c0 = carry_ref[SUBLANES - 2:SUBLANES - 1, :]
    c1 = carry_ref[SUBLANES - 1:SUBLANES, :]
    row = lax.broadcasted_iota(jnp.int32, u.shape, 0)
    um1 = jnp.where(row == 0, c1, pltpu.roll(u, 1, 0))
    um2 = jnp.where(row == 0, c0, jnp.where(row == 1, c1, pltpu.roll(u, 2, 0)))
    h_ref[...] = _conv_mix_tail(x, gb, u, um1, um2, cw_ref, wout_ref)
    tail = u[tm - SUBLANES:, :]
    carry_ref[...] = tail
    st_ref[...] = tail


def _mix0_sample_kernel(x_ref, g_ref, win_ref, cw_ref, wout_ref, s1_ref, s2_ref, h_ref, u_ref,
                        *, seq):
    x = x_ref[...]
    a = _rmsnorm(x, g_ref[...]).astype(BF16)
    proj = _dot(a, win_ref[...])
    gb = proj[:, :D_MODEL]
    u = proj[:, D_MODEL:2 * D_MODEL] * proj[:, 2 * D_MODEL:]
    t = lax.broadcasted_iota(jnp.int32, u.shape, 0) % seq
    um1 = jnp.where(t >= 1, pltpu.roll(u, 1, 0), s1_ref[...])
    um2 = jnp.where(t >= 2, pltpu.roll(u, 2, 0), s2_ref[...])
    h_ref[...] = _conv_mix_tail(x, gb, u, um1, um2, cw_ref, wout_ref)
    u_ref[...] = u


def _mix0_prompt(x, gain, w_in, cw, w_out):
    b, s, d = x.shape
    tm = min(TOKEN_TILE, s)
    grid = (b, s // tm)
    return pl.pallas_call(
        _mix0_prompt_kernel,
        grid=grid,
        in_specs=[
            pl.BlockSpec((None, tm, d), lambda i, j: (i, j, 0)),
            _const_spec((1, d)),
            _const_spec(w_in.shape),
            _const_spec(cw.shape),
            _const_spec(w_out.shape),
        ],
        out_specs=[
            pl.BlockSpec((None, tm, d), lambda i, j: (i, j, 0)),
            pl.BlockSpec((None, SUBLANES, d), lambda i, j: (i, 0, 0)),
        ],
        out_shape=[
            jax.ShapeDtypeStruct((b, s, d), F32),
            jax.ShapeDtypeStruct((b, SUBLANES, d), F32),
        ],
        scratch_shapes=[pltpu.VMEM((SUBLANES, d), F32)],
        compiler_params=_cparams(("arbitrary", "arbitrary")),
        name="mix0_prompt",
    )(x, gain, w_in, cw, w_out)


def _mix0_sample(x2d, gain, w_in, cw, w_out, s1, s2, seq):
    n, d = x2d.shape
    return pl.pallas_call(
        functools.partial(_mix0_sample_kernel, seq=seq),
        out_shape=[jax.ShapeDtypeStruct((n, d), F32), jax.ShapeDtypeStruct((n, d), F32)],
        compiler_params=_cparams(None),
        name="mix0_sample",
    )(x2d, gain, w_in, cw, w_out, s1, s2)


def _swiglu_hidden(f, wgu):
    gu = _dot(f, wgu)
    half = gu.shape[1] // 2
    return _silu(gu[:, :half]) * gu[:, half:]


def _ffn0_kernel(h_ref, p_ref, gf_ref, wgu_ref, wd_ref, gp_ref, wpg_ref, wpp_ref, o_ref, *, chunk):
    h = h_ref[...]
    f = _rmsnorm(h, gf_ref[...]).astype(BF16)
    d_ff = wd_ref.shape[0]
    acc = None
    for c in range(d_ff // chunk):
        hid = _swiglu_hidden(f, wgu_ref[:, 2 * c * chunk:2 * (c + 1) * chunk])
        part = _dot(hid.astype(BF16), wd_ref[c * chunk:(c + 1) * chunk, :])
        acc = part if acc is None else acc + part
    o_ref[...] = _ple(h + acc, p_ref[...], gp_ref[...], wpg_ref, wpp_ref)


def _fuse_gate_up(wg, wu, chunk):
    lead = wg.shape[:-1]
    n_chunks = wg.shape[-1] // chunk
    parts = jnp.stack([wg.reshape(*lead, n_chunks, chunk), wu.reshape(*lead, n_chunks, chunk)],
                      axis=-2)
    return parts.reshape(*lead, 2 * wg.shape[-1])


def _ffn0(h2d, p2d, gf, wg, wu, wd, gp, wpg, wpp):
    n, d = h2d.shape
    tm = min(TOKEN_TILE, n)
    d_ff = wg.shape[1]
    chunk = d_ff // 2
    assert chunk % LANES == 0
    wgu = _fuse_gate_up(wg, wu, chunk)
    return pl.pallas_call(
        functools.partial(_ffn0_kernel, chunk=chunk),
        grid=(n // tm,),
        in_specs=[
            pl.BlockSpec((tm, d), lambda i: (i, 0)),
            pl.BlockSpec((tm, p2d.shape[1]), lambda i: (i, 0)),
            _const_spec((1, d)),
            _const_spec(wgu.shape), _const_spec(wd.shape),
            _const_spec((1, d)),
            _const_spec(wpg.shape), _const_spec(wpp.shape),
        ],
        out_specs=pl.BlockSpec((tm, d), lambda i: (i, 0)),
        out_shape=jax.ShapeDtypeStruct((n, d), F32),
        compiler_params=_cparams(("arbitrary",)),
        name="ffn0_ple",
    )(h2d, p2d, gf, wgu, wd, gp, wpg, wpp)


def _norm_keys_t(kvt, gain_col):
    parts = []
    for gi in range(KV_COLS // HEAD_DIM):
        x = kvt[gi * HEAD_DIM:(gi + 1) * HEAD_DIM, :]
        if gi % 2 == 0:
            ms = jnp.mean(x * x, axis=0, keepdims=True)
            x = x * lax.rsqrt(ms + EPS) * gain_col
        parts.append(x)
    return jnp.concatenate(parts, axis=0)


def _nsa_proj_kernel(h_ref, g_ref, wq_ref, wkc_ref, wkvt_ref, wv_ref, wgt_ref, gm_ref, qg_ref,
                     kgs_ref, kgw_ref, q_ref, kvc_ref, kvct_ref, kvst_ref, kvwt_ref, ksn_ref,
                     kwn_ref, vr_ref, gt_ref):
    a = _rmsnorm(h_ref[...], g_ref[...]).astype(BF16)
    vr = _dot(a, wv_ref[...])
    lane = lax.broadcasted_iota(jnp.int32, vr.shape, 1)
    vr_ref[...] = jnp.where(lane % LANES < HEAD_DIM, vr, 1.0).astype(vr_ref.dtype)
    q = _dot(a, wq_ref[...])
    ms = _group_mean_sq(q, gm_ref[...])
    q_ref[...] = ((q * lax.rsqrt(ms + EPS) * qg_ref[...]) * (HEAD_DIM ** -0.5)).astype(q_ref.dtype)
    kvc = _dot(a, wkc_ref[...])
    for g in range(SLABS):
        kvc_ref[g] = kvc[:, g * LANES:(g + 1) * LANES]
    kvt = _dot_nt(wkvt_ref[...], a)
    kvst = kvt[KV_COLS:2 * KV_COLS]
    kvwt = kvt[2 * KV_COLS:]
    kvct_ref[...] = kvt[:KV_COLS]
    kvst_ref[...] = kvst
    kvwt_ref[...] = kvwt
    ksn_ref[...] = _norm_keys_t(kvst, kgs_ref[...]).astype(ksn_ref.dtype)
    kwn_ref[...] = _norm_keys_t(kvwt, kgw_ref[...]).astype(kwn_ref.dtype)
    gt_ref[...] = jax.nn.sigmoid(_dot(a, wgt_ref[...]))


def _nsa_proj(h2d, n_seq, gain, wq, wkc, wkvt, wv, wgt, gmat, qg, kgs_col, kgw_col):
    n, d = h2d.shape
    s = n // n_seq
    tm = min(TOKEN_TILE, s)
    tps = s // tm
    row = lambda w: pl.BlockSpec((tm, w), lambda i: (i, 0))
    col = pl.BlockSpec((None, KV_COLS, tm), lambda i: (i // tps, 0, i % tps))
    tshape = lambda dt: jax.ShapeDtypeStruct((n_seq, KV_COLS, s), dt)
    n_gate_cols = wgt.shape[1]
    return pl.pallas_call(
        _nsa_proj_kernel,
        grid=(n // tm,),
        in_specs=[
            row(d), _const_spec((1, d)),
            _const_spec(wq.shape), _const_spec(wkc.shape), _const_spec(wkvt.shape),
            _const_spec(wv.shape), _const_spec(wgt.shape), _const_spec(gmat.shape),
            _const_spec(qg.shape), _const_spec(kgs_col.shape), _const_spec(kgw_col.shape),
        ],
        out_specs=[row(d), pl.BlockSpec((SLABS, tm, LANES), lambda i: (0, i, 0)),
                   col, col, col, col, col, row(wv.shape[1]), row(n_gate_cols)],
        out_shape=[
            jax.ShapeDtypeStruct((n, d), BF16),
            jax.ShapeDtypeStruct((SLABS, n, LANES), F32),
            tshape(F32), tshape(F32), tshape(F32), tshape(BF16), tshape(BF16),
            jax.ShapeDtypeStruct((n, wv.shape[1]), BF16),
            jax.ShapeDtypeStruct((n, n_gate_cols), F32),
        ],
        compiler_params=_cparams(("arbitrary",)),
        name="nsa_proj",
    )(h2d, gain, wq, wkc, wkvt, wv, wgt, gmat, qg, kgs_col, kgw_col)


def _compress_rows(read_slab, n_blocks, w_ref, gmat, kg_row):
    half = KV_COLS // 2
    per_half = SLABS // 2
    acc = [jnp.zeros((n_blocks, half), F32) for _ in range(2)]
    for l in range(CMP_BLOCK):
        for hf in range(2):
            xl = jnp.concatenate([read_slab(l, hf * per_half + c) for c in range(per_half)], axis=1)
            acc[hf] = acc[hf] + _dot(xl.astype(BF16), w_ref[l, hf])
    kv = jnp.concatenate(acc, axis=1)
    return _norm_keys(kv, kg_row, gmat).astype(BF16)


def _compress_kernel(x_ref, w_ref, gm_ref, kg_ref, o_ref):
    nb = o_ref.shape[0]
    read = lambda l, g: x_ref[g, pl.ds(l, nb, stride=CMP_BLOCK), :]
    o_ref[...] = _compress_rows(read, nb, w_ref, gm_ref[...], kg_ref[...])


def _compress_prompt(slabs, w2, gmat, kg):
    n = slabs.shape[1]
    tr = min(4096, n)
    nb = tr // CMP_BLOCK
    return pl.pallas_call(
        _compress_kernel,
        grid=(n // tr,),
        in_specs=[pl.BlockSpec((SLABS, tr, LANES), lambda i: (0, i, 0)),
                  _const_spec(w2.shape), _const_spec(gmat.shape), _const_spec(kg.shape)],
        out_specs=pl.BlockSpec((nb, KV_COLS), lambda i: (i, 0)),
        out_shape=jax.ShapeDtypeStruct((n // CMP_BLOCK, KV_COLS), BF16),
        compiler_params=_cparams(("arbitrary",)),
        name="compress_prompt",
    )(slabs, w2, gmat, kg)


def _compress_paged_kernel(pt_ref, pool_ref, w_ref, gm_ref, kg_ref, o_ref, buf_ref, st_ref, sem_ref,
                           *, pages_per_step, page):
    i = pl.program_id(0)
    n = pl.num_programs(0)

    def page_copy(step, p, slot):
        phys = pt_ref[step * pages_per_step + p]
        return pltpu.make_async_copy(pool_ref.at[phys], buf_ref.at[slot, p], sem_ref.at[slot])

    def start(step, slot):
        def issue(p, c):
            page_copy(step, p, slot).start()
            return c
        lax.fori_loop(0, pages_per_step, issue, 0)

    @pl.when(i == 0)
    def _():
        start(0, 0)

    @pl.when(i + 1 < n)
    def _():
        start(i + 1, (i + 1) % 2)

    slot = i % 2

    def wait(p, c):
        page_copy(i, p, slot).wait()
        return c
    lax.fori_loop(0, pages_per_step, wait, 0)

    blocks_per_page = page // CMP_BLOCK
    group = 2 * blocks_per_page
    assert group == SUBLANES
    ri = lax.broadcasted_iota(jnp.int32, (2 * page, 2 * page), 0)
    ci = lax.broadcasted_iota(jnp.int32, (2 * page, 2 * page), 1)
    src_lane = ((ri % group) // blocks_per_page) * page + (ri % blocks_per_page) * CMP_BLOCK + ri // group
    perm = (ci == src_lane).astype(BF16)

    def to_rows(pp, c):
        for g0 in range(0, N_KV_HEADS, 2):
            pair = jnp.concatenate([buf_ref[slot, 2 * pp, g0:g0 + 2].reshape(2 * GROUP_COLS, page),
                                    buf_ref[slot, 2 * pp + 1, g0:g0 + 2].reshape(2 * GROUP_COLS, page)],
                                   axis=1).astype(BF16)
            rows = _dot_nt(perm, pair)
            for l in range(CMP_BLOCK):
                dst = pl.ds(pl.multiple_of(pp * group, group), group)
                blk = rows[l * group:(l + 1) * group]
                st_ref[g0, l, dst, :] = blk[:, :GROUP_COLS]
                st_ref[g0 + 1, l, dst, :] = blk[:, GROUP_COLS:]
        return c
    lax.fori_loop(0, pages_per_step // 2, to_rows, 0)

    nb = pages_per_step * page // CMP_BLOCK
    read = lambda l, g: st_ref[g, l]
    o_ref[...] = _compress_rows(read, nb, w_ref, gm_ref[...], kg_ref[...])


def _compress_paged(page_table_flat, pool_t, w2, gmat, kg, pages_per_step):
    n_pages = page_table_flat.shape[0]
    page = pool_t.shape[3]
    assert n_pages % pages_per_step == 0 and page == LANES and pool_t.shape[2] == GROUP_COLS
    steps = n_pages // pages_per_step
    nb = pages_per_step * page // CMP_BLOCK
    grid_spec = pltpu.PrefetchScalarGridSpec(
        num_scalar_prefetch=1,
        grid=(steps,),
        in_specs=[pl.BlockSpec(memory_space=pl.ANY),
                  pl.BlockSpec(w2.shape, lambda i, pt: (0, 0, 0, 0), pipeline_mode=pl.Buffered(1)),
                  pl.BlockSpec(gmat.shape, lambda i, pt: (0, 0)),
                  pl.BlockSpec(kg.shape, lambda i, pt: (0, 0))],
        out_specs=pl.BlockSpec((nb, KV_COLS), lambda i, pt: (i, 0)),
        scratch_shapes=[pltpu.VMEM((2, pages_per_step, N_KV_HEADS, GROUP_COLS, page), F32),
                        pltpu.VMEM((N_KV_HEADS, CMP_BLOCK, nb, GROUP_COLS), F32),
                        pltpu.SemaphoreType.DMA((2,))],
    )
    return pl.pallas_call(
        functools.partial(_compress_paged_kernel, pages_per_step=pages_per_step, page=page),
        grid_spec=grid_spec,
        out_shape=jax.ShapeDtypeStruct((steps * nb, KV_COLS), BF16),
        compiler_params=_cparams(("arbitrary",)),
        name="compress_paged",
    )(page_table_flat, pool_t, w2, gmat, kg)


FLASH_ROWS = 128


def _flash_tile(qh_ref, kt_ref, v_ref, kt, tk, state, terms):
    m_ref, acc_ref = state
    start = pl.multiple_of(kt * tk, tk)
    k = kt_ref[:, pl.ds(start, tk)]
    v = v_ref[pl.ds(start, tk), :]
    tq = qh_ref.shape[1]
    blocks = [(r, pl.ds(rb * FLASH_ROWS, FLASH_ROWS))
              for r in range(HEADS_PER_GROUP) for rb in range(tq // FLASH_ROWS)]
    old = [(m_ref[r, rows, :], acc_ref[r, rows, :]) for r, rows in blocks]
    new = []
    for (r, rows), (m_old, acc_old) in zip(blocks, old):
        s = _dot(qh_ref[r, rows, :], k)
        for term in terms(r, rows):
            s = s + term
        m_new = jnp.maximum(m_old, jnp.max(s, axis=-1, keepdims=True))
        e = jnp.exp(s - jnp.concatenate([m_new] * (tk // LANES), axis=1))
        new.append((m_new, jnp.exp(m_old - m_new) * acc_old + _dot(e.astype(BF16), v)))
    for (r, rows), (m_new, acc_new) in zip(blocks, new):
        m_ref[r, rows, :] = m_new
        acc_ref[r, rows, :] = acc_new


def _nsa_prompt_kernel(tbl_ref, thr_ref, q_ref, kc_ref, ks_ref, vs_ref, kw_ref, vw_ref, gt_ref, o_ref,
                       bias_ref, edge_ref, cbias_ref, amask_ref, qh_ref, m_ref, acc_ref, *, n_top):
    g = pl.program_id(0)
    b = pl.program_id(1)
    qi = pl.program_id(2)
    tq = q_ref.shape[0]
    tk = tq
    hpg = HEADS_PER_GROUP
    nc = kc_ref.shape[0]
    ns = nc // CMP_PER_SEL
    blocks_per_tile = tk // SEL_BLOCK
    far = N_BUCKETS - 1

    qpos = qi * tq + lax.broadcasted_iota(jnp.int32, (tq, 1), 0)
    n_io = lax.broadcasted_iota(jnp.int32, (tq, nc), 1)
    relc = qpos - ((n_io + 1) * CMP_BLOCK - 1)

    n_back = WINDOW // tk

    @pl.when((b == 0) & (qi == 0))
    def _():
        diag = (lax.broadcasted_iota(jnp.int32, (tq, tk), 0)
                - lax.broadcasted_iota(jnp.int32, (tq, tk), 1))
        for r in range(hpg):
            h = g * hpg + r
            shifted = lambda k: tbl_ref[k, h] - tbl_ref[far, h]
            bias_ref[r, 0] = jnp.where(diag >= 0, _bucket_bias(diag, thr_ref, shifted), NEG)
            bias_ref[r, 1] = _bucket_bias(diag + tk, thr_ref, shifted)
        edge_ref[...] = jnp.where(diag + n_back * tk < WINDOW, 0.0, NEG)

    @pl.when(b == 0)
    def _():
        for r in range(hpg):
            h = g * hpg + r
            cbias_ref[r, qi] = _bucket_bias(relc, thr_ref, lambda k: tbl_ref[k, h])

    q = q_ref[...]
    for r in range(hpg):
        qh_ref[r] = q[:, r * HEAD_DIM:(r + 1) * HEAD_DIM]

    sel_state = (m_ref.at[0], acc_ref.at[0])
    win_state = (m_ref.at[1], acc_ref.at[1])
    m_ref[...] = jnp.full(m_ref.shape, NEG, F32)
    acc_ref[...] = jnp.zeros(acc_ref.shape, F32)

    def finish(branch):
        out = []
        for r in range(hpg):
            acc = acc_ref[branch, r]
            out.append((acc / pltpu.roll(acc, HEAD_DIM, 1))[:, :HEAD_DIM])
        return out

    def win_tile(kt, terms):
        _flash_tile(qh_ref, kw_ref, vw_ref, kt, tk, win_state, terms)

    win_tile(qi, lambda r, rows: (bias_ref[r, 0, rows, :],))
    no_prev = jnp.where(qi >= 1, 0.0, NEG)
    win_tile(jnp.maximum(qi - 1, 0), lambda r, rows: (bias_ref[r, 1, rows, :], no_prev))

    kc = kc_ref[...]
    kck = kc[:, :HEAD_DIM]
    kcv = kc[:, HEAD_DIM:]
    maskc = relc >= 0
    imp = None
    oc = []
    for r in range(hpg):
        s = _dot_nt(q[:, r * HEAD_DIM:(r + 1) * HEAD_DIM], kck) + cbias_ref[r, qi]
        sm = jnp.where(maskc, s, NEG)
        e = jnp.exp(sm - jnp.max(sm, axis=-1, keepdims=True))
        p = jnp.where(maskc, e / jnp.sum(e, axis=-1, keepdims=True), 0.0)
        imp = p if imp is None else imp + p
        oc.append(_dot(p.astype(BF16), kcv))

    pair_t = (lax.broadcasted_iota(jnp.int32, (ns, nc), 0)
              == lax.broadcasted_iota(jnp.int32, (ns, nc), 1) // CMP_PER_SEL).astype(BF16)
    hi, mid, lo = _split3(imp)
    imp_sel = (_dot_nt(pair_t, hi) + _dot_nt(pair_t, mid)) + _dot_nt(pair_t, lo)
    blk = lax.broadcasted_iota(jnp.int32, (ns, tq), 0)
    cur = (qi * tq + lax.broadcasted_iota(jnp.int32, (1, tq), 1)) // SEL_BLOCK
    forced = (blk == 0) | (blk == cur) | (blk == cur - 1)
    score = jnp.where(forced, FORCED_SCORE, jnp.where(blk <= cur, imp_sel, -1.0))
    rank = jnp.zeros((ns, tq), F32)
    for i in range(ns):
        si = score[i:i + 1, :]
        beats = (si > score) | ((si == score) & (blk > i))
        rank = rank + jnp.where(beats, 1.0, 0.0)
    sel_t = jnp.where(rank < n_top, 1.0, 0.0).astype(BF16)
    eye = (lax.broadcasted_iota(jnp.int32, (tq, tq), 0)
           == lax.broadcasted_iota(jnp.int32, (tq, tq), 1)).astype(BF16)
    sel = _dot_nt(eye, sel_t).astype(BF16)

    def sel_tile(kt, tiles_back):
        expand = (lax.broadcasted_iota(jnp.int32, (ns, tk), 0)
                  == kt * blocks_per_tile + lax.broadcasted_iota(jnp.int32, (ns, tk), 1) // SEL_BLOCK)
        amask_ref[...] = jnp.where(_dot(sel, expand.astype(BF16)) > 0.5, 0.0, NEG)
        if tiles_back is None:
            terms = lambda r, rows: (amask_ref[rows, :],)
        else:
            terms = lambda r, rows: (bias_ref[r, tiles_back, rows, :], amask_ref[rows, :])
        _flash_tile(qh_ref, ks_ref, vs_ref, kt, tk, sel_state, terms)

    def far_tile(kt, c):
        sel_tile(kt, None)
        return c
    lax.fori_loop(0, jnp.maximum(qi - 1, 0), far_tile, 0)

    @pl.when(qi >= n_back)
    def _():
        sel_tile(qi - 1, 1)
        win_tile(qi - n_back, lambda r, rows: (edge_ref[rows, :],))

    @pl.when((qi >= 1) & (qi < n_back))
    def _():
        sel_tile(qi - 1, 1)

    sel_tile(qi, 0)
    o_sel = finish(0)
    o_win = finish(1)

    width = hpg * HEAD_DIM
    idx = lax.broadcasted_iota(jnp.int32, (LANES, N_BRANCHES * width), 0)
    col = lax.broadcasted_iota(jnp.int32, (LANES, N_BRANCHES * width), 1)
    spread = (idx == (col // width) * hpg + (col % width) // HEAD_DIM).astype(BF16)
    gates = _exact_dot(gt_ref[...], spread)
    out = None
    for c, branch in enumerate((oc, o_sel, o_win)):
        term = gates[:, c * width:(c + 1) * width] * jnp.concatenate(branch, axis=1)
        out = term if out is None else out + term
    o_ref[...] = out.astype(o_ref.dtype)


def _nsa_prompt_attn(tbl, thr, q, kcn, ksn_t, kwn_t, v_rows, gates, b, s):
    tq = min(ATTN_TILE, s)
    assert WINDOW == 2 * tq and tq + 1 >= MAX_DISTANCE and tq % FLASH_ROWS == 0
    nq = s // tq
    nc = s // CMP_BLOCK
    ns = s // SEL_BLOCK
    n_top = min(N_SELECTED, ns)
    hpg = HEADS_PER_GROUP
    k_spec = pl.BlockSpec((None, HEAD_DIM, s), lambda g, i, j: (i, 2 * g, 0))
    v_spec = lambda branch: pl.BlockSpec((s, LANES), lambda g, i, j: (i, branch * N_KV_HEADS + g))
    return pl.pallas_call(
        functools.partial(_nsa_prompt_kernel, n_top=n_top),
        grid=(N_KV_HEADS, b, nq),
        in_specs=[
            _smem_spec(), _smem_spec(),
            pl.BlockSpec((tq, Q_GROUP_COLS), lambda g, i, j: (i * nq + j, g)),
            pl.BlockSpec((nc, GROUP_COLS), lambda g, i, j: (i, g)),
            k_spec, v_spec(0), k_spec, v_spec(1),
            pl.BlockSpec((tq, LANES), lambda g, i, j: (i * nq + j, g)),
        ],
        out_specs=pl.BlockSpec((tq, Q_GROUP_COLS), lambda g, i, j: (i * nq + j, g)),
        out_shape=jax.ShapeDtypeStruct((b * s, D_MODEL), BF16),
        scratch_shapes=[pltpu.VMEM((hpg, 2, tq, tq), F32),
                        pltpu.VMEM((tq, tq), F32),
                        pltpu.VMEM((hpg, nq, tq, nc), F32),
                        pltpu.VMEM((tq, tq), F32),
                        pltpu.VMEM((hpg, tq, HEAD_DIM), BF16),
                        pltpu.VMEM((2, hpg, tq, LANES), F32),
                        pltpu.VMEM((2, hpg, tq, LANES), F32)],
        compiler_params=_cparams(("arbitrary", "arbitrary", "arbitrary")),
        name="nsa_prompt_attn",
    )(tbl, thr, q, kcn, ksn_t, v_rows, kwn_t, v_rows, gates)


def _bucket_onehot(dist_row, thr_ref):
    n = dist_row.shape[1]
    bucket = jnp.zeros(dist_row.shape, jnp.int32)
    for k in range(1, N_BUCKETS):
        bucket = bucket + jnp.where(dist_row >= thr_ref[k], 1, 0)
    return (lax.broadcasted_iota(jnp.int32, (N_BUCKETS, n), 0) == bucket).astype(BF16)


def _sample_cmp_kernel(thr_ref, q_ref, kc_ref, tbl_ref, oc_ref, idx_ref, *, past_len, seq, n_top):
    nc = kc_ref.shape[0]
    ns = past_len // SEL_BLOCK
    hpg = HEADS_PER_GROUP
    rows = hpg * seq
    kc = kc_ref[...]
    pair = (lax.broadcasted_iota(jnp.int32, (nc, ns), 0) // CMP_PER_SEL
            == lax.broadcasted_iota(jnp.int32, (nc, ns), 1)).astype(BF16)
    eye = (lax.broadcasted_iota(jnp.int32, (ns, ns), 0)
           == lax.broadcasted_iota(jnp.int32, (ns, ns), 1))
    eye_bf = eye.astype(BF16)
    ii = lax.broadcasted_iota(jnp.int32, (ns, ns), 0)
    jj = lax.broadcasted_iota(jnp.int32, (ns, ns), 1)
    t_row = lax.broadcasted_iota(jnp.int32, (rows, nc), 0) % seq
    n_io = lax.broadcasted_iota(jnp.int32, (rows, nc), 1)
    rel = past_len + t_row - ((n_io + 1) * CMP_BLOCK - 1)
    mask = rel >= 0
    blk = lax.broadcasted_iota(jnp.int32, (SUBLANES, ns), 1)
    forced = (blk == 0) | (blk == ns - 1)
    rank_lane = lax.broadcasted_iota(jnp.int32, (ns, LANES), 1).astype(F32)
    blk_col = lax.broadcasted_iota(jnp.int32, (ns, LANES), 0).astype(F32)
    for g in range(N_KV_HEADS):
        kck = kc[:, g * GROUP_COLS:g * GROUP_COLS + HEAD_DIM]
        kcv = kc[:, g * GROUP_COLS + HEAD_DIM:(g + 1) * GROUP_COLS]
        qg = q_ref[g].astype(BF16)
        tbl_g = tbl_ref[g]
        s = _dot_nt(qg, kck)
        s = s + _bucket_bias(rel, thr_ref, lambda k: tbl_g[:, k:k + 1])
        sm = jnp.where(mask, s, NEG)
        e = jnp.exp(sm - jnp.max(sm, axis=-1, keepdims=True))
        p = jnp.where(mask, e / jnp.sum(e, axis=-1, keepdims=True), 0.0)
        oc_ref[g] = _dot(p.astype(BF16), kcv)
        imp = p[0:seq]
        for r in range(1, hpg):
            imp = imp + p[r * seq:(r + 1) * seq]
        imp = jnp.concatenate([imp, jnp.zeros((SUBLANES - seq, nc), F32)], axis=0)
        score = jnp.where(forced, FORCED_SCORE, _exact_dot(imp, pair))
        hi, mid, lo = _split3(score)
        score_t = (_dot_nt(eye_bf, hi) + _dot_nt(eye_bf, mid)) + _dot_nt(eye_bf, lo)
        for t in range(seq):
            s_row = score[t:t + 1, :]
            s_col = score_t[:, t:t + 1]
            beats = (ii != jj) & ((s_row > s_col) | ((s_row == s_col) & (jj < ii)))
            rank_col = jnp.sum(jnp.where(beats, 1.0, 0.0), axis=1, keepdims=True)
            onehot = rank_col == rank_lane
            idx_row = jnp.sum(jnp.where(onehot, blk_col, 0.0), axis=0, keepdims=True)
            idx_ref[g * seq + t:g * seq + t + 1, :] = idx_row.astype(jnp.int32)


def _sample_cmp(thr, q_rt, kcn, tbl_rt, b, past_len, seq):
    nc = kcn.shape[0] // b
    ns = past_len // SEL_BLOCK
    n_top = min(N_SELECTED - 1, ns)
    rows = HEADS_PER_GROUP * seq
    return pl.pallas_call(
        functools.partial(_sample_cmp_kernel, past_len=past_len, seq=seq, n_top=n_top),
        grid=(b,),
        in_specs=[
            _smem_spec(),
            pl.BlockSpec((None, N_KV_HEADS, rows, HEAD_DIM), lambda i: (i, 0, 0, 0)),
            pl.BlockSpec((nc, KV_COLS), lambda i: (i, 0)),
            _const_spec(tbl_rt.shape),
        ],
        out_specs=[
            pl.BlockSpec((None, N_KV_HEADS, rows, HEAD_DIM), lambda i: (i, 0, 0, 0)),
            pl.BlockSpec((None, N_KV_HEADS * seq, LANES), lambda i: (i, 0, 0)),
        ],
        out_shape=[
            jax.ShapeDtypeStruct((b, N_KV_HEADS, rows, HEAD_DIM), F32),
            jax.ShapeDtypeStruct((b, N_KV_HEADS * seq, LANES), jnp.int32),
        ],
        compiler_params=_cparams(("arbitrary",)),
        name="sample_cmp",
    )(thr, q_rt, kcn, tbl_rt)


def _norm_k_cols(kt, gain_col):
    ms = jnp.mean(kt * kt, axis=0, keepdims=True)
    return (kt * lax.rsqrt(ms + EPS) * gain_col).astype(BF16)


def _sample_sel_win_kernel(idx_ref, pt_ref, thr_ref, pool_ref, q_ref, ksnew_ref, kwnew_ref,
                           win_ref, tbl_ref, kgs_ref, kgw_ref, os_ref, ow_ref,
                           buf_ref, sem_ref, *, past_len, seq, n_top, page):
    i = pl.program_id(0)
    n_pages = past_len // page
    blocks_per_page = page // SEL_BLOCK
    n_keys = n_top * page
    stride = n_top + 1

    def block_of(g, t, j):
        return idx_ref[(i * N_KV_HEADS * seq + g * seq + t) * stride + j]

    def tile_copy(g, t, j):
        phys = pt_ref[i * n_pages + block_of(g, t, j) // blocks_per_page]
        return pltpu.make_async_copy(
            pool_ref.at[phys, g],
            buf_ref.at[g * seq + t, :, pl.ds(pl.multiple_of(j * page, page), page)],
            sem_ref.at[0])

    def for_all_tiles(fn):
        for g in range(N_KV_HEADS):
            for t in range(seq):
                def per_tile(j, c):
                    fn(tile_copy(g, t, j))
                    return c
                lax.fori_loop(0, n_top, per_tile, 0)

    for_all_tiles(lambda cp: cp.start())

    wb = win_ref.shape[2]
    wk = [_norm_k_cols(win_ref[g, 0:HEAD_DIM, :], kgw_ref[...]) for g in range(N_KV_HEADS)]
    wv = [win_ref[g, HEAD_DIM:GROUP_COLS, :].astype(BF16) for g in range(N_KV_HEADS)]
    j_io = lax.broadcasted_iota(jnp.int32, (1, wb), 1)
    tnew = lax.broadcasted_iota(jnp.int32, (1, SUBLANES), 1)

    def attend(q, tbl_g, k_past, v_past, buckets, mask, knew, bucketsn, maskn):
        s1 = jnp.where(mask, _dot(q, k_past) + _exact_dot(tbl_g, buckets), NEG)
        s2 = jnp.where(maskn, _dot_nt(q, knew[:, :HEAD_DIM].astype(BF16))
                       + _exact_dot(tbl_g, bucketsn), NEG)
        m = jnp.maximum(jnp.max(s1, axis=-1, keepdims=True), jnp.max(s2, axis=-1, keepdims=True))
        e1 = jnp.where(mask, jnp.exp(s1 - m), 0.0)
        e2 = jnp.where(maskn, jnp.exp(s2 - m), 0.0)
        l = jnp.sum(e1, axis=-1, keepdims=True) + jnp.sum(e2, axis=-1, keepdims=True)
        o = _dot_nt(e1.astype(BF16), v_past) + _dot(e2.astype(BF16),
                                                    knew[:, HEAD_DIM:].astype(BF16))
        return o / l

    def window(t, c):
        dist = wb + t - j_io
        maskw = (dist >= 0) & (dist < WINDOW)
        distn = t - tnew
        maskn = (distn >= 0) & (distn < WINDOW)
        buckets = _bucket_onehot(dist, thr_ref)
        bucketsn = _bucket_onehot(distn, thr_ref)
        for g in range(N_KV_HEADS):
            ow_ref[g, t] = attend(q_ref[g, t].astype(BF16), tbl_ref[g], wk[g], wv[g], buckets,
                                  maskw, kwnew_ref[g], bucketsn, maskn)
        return c
    lax.fori_loop(0, seq, window, 0)

    for_all_tiles(lambda cp: cp.wait())

    lane = lax.broadcasted_iota(jnp.int32, (1, n_keys), 1)
    tile_of_lane = lane // page
    row_in_page = lane % page

    def selected(t, c):
        distn = t - tnew
        maskn = distn >= 0
        bucketsn = _bucket_onehot(distn, thr_ref)
        for g in range(N_KV_HEADS):
            kvt = buf_ref[g * seq + t]
            page_base = jnp.zeros((1, n_keys), jnp.int32)
            half = jnp.zeros((1, n_keys), jnp.int32)
            for j in range(n_top):
                blk = block_of(g, t, j)
                page_base = jnp.where(tile_of_lane == j, (blk // blocks_per_page) * page, page_base)
                half = jnp.where(tile_of_lane == j, blk % blocks_per_page, half)
            mask = (row_in_page // SEL_BLOCK) == half
            dist = past_len + t - (page_base + row_in_page)
            os_ref[g, t] = attend(q_ref[g, t].astype(BF16), tbl_ref[g],
                                  _norm_k_cols(kvt[0:HEAD_DIM], kgs_ref[...]),
                                  kvt[HEAD_DIM:GROUP_COLS].astype(BF16),
                                  _bucket_onehot(dist, thr_ref), mask,
                                  ksnew_ref[g], bucketsn, maskn)
        return c
    lax.fori_loop(0, seq, selected, 0)


def _sample_sel_win(idx_flat, pt_flat, thr, pool_t, q_tr, ksnew, kwnew, win_t, tbl_r8,
                    kgs_col, kgw_col, b, past_len, seq):
    ns = past_len // SEL_BLOCK
    n_top = min(N_SELECTED - 1, ns)
    page = pool_t.shape[3]
    wb = win_t.shape[3]
    o_shape = jax.ShapeDtypeStruct((b, N_KV_HEADS, seq, SUBLANES, HEAD_DIM), F32)
    o_spec = pl.BlockSpec((None, N_KV_HEADS, seq, SUBLANES, HEAD_DIM),
                          lambda i, *_: (i, 0, 0, 0, 0))
    new_spec = pl.BlockSpec((None, N_KV_HEADS, SUBLANES, GROUP_COLS), lambda i, *_: (i, 0, 0, 0))
    grid_spec = pltpu.PrefetchScalarGridSpec(
        num_scalar_prefetch=3,
        grid=(b,),
        in_specs=[
            pl.BlockSpec(memory_space=pl.ANY),
            pl.BlockSpec((None, N_KV_HEADS, seq, SUBLANES, HEAD_DIM), lambda i, *_: (i, 0, 0, 0, 0)),
            new_spec, new_spec,
            pl.BlockSpec((None, N_KV_HEADS, GROUP_COLS, wb), lambda i, *_: (i, 0, 0, 0)),
            pl.BlockSpec(tbl_r8.shape, lambda i, *_: (0, 0, 0)),
            pl.BlockSpec(kgs_col.shape, lambda i, *_: (0, 0)),
            pl.BlockSpec(kgw_col.shape, lambda i, *_: (0, 0)),
        ],
        out_specs=[o_spec, o_spec],
        scratch_shapes=[pltpu.VMEM((N_KV_HEADS * seq, GROUP_COLS, n_top * page), F32),
                        pltpu.SemaphoreType.DMA((1,))],
    )
    return pl.pallas_call(
        functools.partial(_sample_sel_win_kernel, past_len=past_len, seq=seq, n_top=n_top,
                          page=page),
        grid_spec=grid_spec,
        out_shape=[o_shape, o_shape],
        compiler_params=_cparams(("arbitrary",)),
        name="sample_sel_win",
    )(idx_flat, pt_flat, thr, pool_t, q_tr, ksnew, kwnew, win_t, tbl_r8, kgs_col, kgw_col)


def _route(f, rhi_ref, rlo_ref):
    f_hi = f.astype(BF16)
    f_lo = (f - f_hi.astype(F32)).astype(BF16)
    logits = (_dot(f_hi, rhi_ref[...]) + _dot(f_lo, rhi_ref[...])) + _dot(f_hi, rlo_ref[...])
    lane = lax.broadcasted_iota(jnp.int32, logits.shape, 1).astype(F32)
    logits = jnp.where(lane < N_EXPERTS, logits, -jnp.inf)
    m1 = jnp.max(logits, axis=-1, keepdims=True)
    i1 = jnp.min(jnp.where(logits == m1, lane, float(LANES)), axis=-1, keepdims=True)
    rest = jnp.where(lane == i1, -jnp.inf, logits)
    m2 = jnp.max(rest, axis=-1, keepdims=True)
    i2 = jnp.min(jnp.where(rest == m2, lane, float(LANES)), axis=-1, keepdims=True)
    e2 = jnp.exp(m2 - m1)
    denom = 1.0 + e2
    return jnp.where(lane == i1, 1.0 / denom, 0.0) + jnp.where(lane == i2, e2 / denom, 0.0)


def _attn_out_kernel(h_ref, o_ref, wout_ref, gf_ref, rhi_ref, rlo_ref, h2_ref, f_ref, gate_ref):
    h2 = h_ref[...] + _dot(o_ref[...], wout_ref[...])
    h2_ref[...] = h2
    f = _rmsnorm(h2, gf_ref[...])
    f_ref[...] = f.astype(f_ref.dtype)
    gate_ref[...] = _route(f, rhi_ref, rlo_ref)


def _attn_out_merge_kernel(h_ref, oc_ref, os_ref, ow_ref, gt_ref, wout_ref, gf_ref, rhi_ref,
                           rlo_ref, h2_ref, f_ref, gate_ref):
    gates = gt_ref[...]
    n_gate = gates.shape[1]
    hpg = HEADS_PER_GROUP
    idx = lax.broadcasted_iota(jnp.int32, (n_gate, D_MODEL), 0)
    head = lax.broadcasted_iota(jnp.int32, (n_gate, D_MODEL), 1) // HEAD_DIM
    o = None
    for c, ref in enumerate((oc_ref, os_ref, ow_ref)):
        lane_of_gate = (head // hpg) * LANES + c * hpg + head % hpg
        gate = _exact_dot(gates, (idx == lane_of_gate).astype(BF16))
        term = gate * ref[...]
        o = term if o is None else o + term
    h2 = h_ref[...] + _dot(o.astype(BF16), wout_ref[...])
    h2_ref[...] = h2
    f = _rmsnorm(h2, gf_ref[...])
    f_ref[...] = f.astype(f_ref.dtype)
    gate_ref[...] = _route(f, rhi_ref, rlo_ref)


def _attn_out(h2d, o_list, gates, wout, gf, rhi, rlo):
    n, d = h2d.shape
    tm = min(TOKEN_TILE, n)
    row = lambda w: pl.BlockSpec((tm, w), lambda i: (i, 0))
    merged = len(o_list) == 1
    body = _attn_out_kernel if merged else _attn_out_merge_kernel
    in_specs = [row(d)] + [row(d)] * len(o_list) + ([] if merged else [row(gates.shape[1])])
    in_specs += [_const_spec(wout.shape), _const_spec((1, d)), _const_spec(rhi.shape),
                 _const_spec(rlo.shape)]
    args = [h2d] + list(o_list) + ([] if merged else [gates]) + [wout, gf, rhi, rlo]
    return pl.pallas_call(
        body,
        grid=(n // tm,),
        in_specs=in_specs,
        out_specs=[row(d), row(d), row(LANES)],
        out_shape=[jax.ShapeDtypeStruct((n, d), F32), jax.ShapeDtypeStruct((n, d), BF16),
                   jax.ShapeDtypeStruct((n, LANES), F32)],
        compiler_params=_cparams(("arbitrary",)),
        name="attn_out_route",
    )(*args)


def _moe_kernel(f_ref, h2_ref, gate_ref, p_ref, wgu_ref, wd_ref, gp_ref, wpg_ref, wpp_ref,
                o_ref, acc_ref):
    e = pl.program_id(1)

    @pl.when(e == 0)
    def _():
        acc_ref[...] = jnp.zeros_like(acc_ref)

    f = f_ref[...]
    gate = gate_ref[...]
    lane = lax.broadcasted_iota(jnp.int32, gate.shape, 1)
    ge = jnp.sum(jnp.where(lane == e, gate, 0.0), axis=-1, keepdims=True)
    hid = _swiglu_hidden(f, wgu_ref[...])
    acc_ref[...] += ge * _dot(hid.astype(BF16), wd_ref[...])

    @pl.when(e == pl.num_programs(1) - 1)
    def _():
        o_ref[...] = _ple(h2_ref[...] + acc_ref[...], p_ref[...], gp_ref[...], wpg_ref, wpp_ref)


def _moe(f2d, h2, gate, p2d, wg, wu, wd, gp, wpg, wpp):
    n, d = h2.shape
    tm = min(TOKEN_TILE, n)
    n_e, _, dff = wg.shape
    wgu = _fuse_gate_up(wg, wu, dff)
    row = lambda w: pl.BlockSpec((tm, w), lambda i, e: (i, 0))
    return pl.pallas_call(
        _moe_kernel,
        grid=(n // tm, n_e),
        in_specs=[
            row(d), row(d), row(LANES), row(p2d.shape[1]),
            pl.BlockSpec((None, d, 2 * dff), lambda i, e: (e, 0, 0)),
            pl.BlockSpec((None, dff, d), lambda i, e: (e, 0, 0)),
            _const_spec((1, d)), _const_spec(wpg.shape), _const_spec(wpp.shape),
        ],
        out_specs=row(d),
        out_shape=jax.ShapeDtypeStruct((n, d), F32),
        scratch_shapes=[pltpu.VMEM((tm, d), F32)],
        compiler_params=_cparams(("arbitrary", "arbitrary")),
        name="moe_ple",
    )(f2d, h2, gate, p2d, wgu, wd, gp, wpg, wpp)


def _bucket_thresholds():
    n = jnp.arange(MAX_DISTANCE + 1)
    max_exact = N_BUCKETS // 2
    nf = jnp.maximum(n, 1).astype(F32)
    large = max_exact + (jnp.log(nf / max_exact) / math.log(MAX_DISTANCE / max_exact)
                         * (N_BUCKETS - max_exact)).astype(jnp.int32)
    bucket = jnp.where(n < max_exact, n, jnp.minimum(large, N_BUCKETS - 1))
    return jnp.sum(bucket[None, :] < jnp.arange(N_BUCKETS)[:, None], axis=1).astype(jnp.int32)


def _row(v):
    return v.reshape(1, -1).astype(F32)


def _key_gain_row(k_gain, cols):
    g = jnp.concatenate([k_gain, jnp.ones_like(k_gain)])
    return jnp.tile(g, cols // GROUP_COLS).reshape(1, cols).astype(F32)


def _rows_on_lanes(x):
    lead = x.shape[:-4]
    rows = x.shape[-4]
    nl = len(lead)
    perm = tuple(range(nl)) + (nl + 1, nl + 2, nl + 3, nl)
    return jnp.transpose(x, perm).reshape(*lead, N_KV_HEADS, GROUP_COLS, rows)


def _rows_on_lanes_inverse(xt):
    b, _, rows = xt.shape
    x = xt.reshape(b, N_KV_HEADS, 2, HEAD_DIM, rows)
    return jnp.transpose(x, (0, 4, 1, 2, 3))[None]


def kernel(x_prompt, x_sample, state_conv, cache_cmp, cache_sel, state_win, page_table,
           p_prompt, p_sample, norm_mix, norm_ffn, norm_ple, conv_w_in, conv_w, conv_w_out,
           nsa_w_in, nsa_w_cmp, nsa_q_norm, nsa_k_norm, nsa_w_out, rel_bias,
           ffn_w_gate, ffn_w_up, ffn_w_down, moe_router, moe_w_gate, moe_w_up, moe_w_down,
           ple_w_proj, ple_w_gate):
    bp, sp, d = x_prompt.shape
    bs, ss, _ = x_sample.shape
    n_p = bp * sp
    n_s = bs * ss
    page = cache_cmp.shape[2]
    past_len = page_table.shape[1] * page
    hpg = HEADS_PER_GROUP
    bf = lambda w: w.astype(BF16)

    cw = jnp.zeros((SUBLANES, d), F32).at[:CONV_WIDTH].set(conv_w[0])
    w_in0, w_out0 = bf(conv_w_in[0]), bf(conv_w_out[0])
    g_mix0 = _row(norm_mix[0])
    h_p, tail_p = _mix0_prompt(x_prompt, g_mix0, w_in0, cw, w_out0)
    conv_prompt = tail_p[:, SUBLANES - (CONV_WIDTH - 1):][None]

    st = state_conv[0]
    zeros = jnp.zeros((bs, ss, d), F32)
    s1 = zeros.at[:, 0].set(st[:, 1]).reshape(n_s, d)
    s2 = zeros.at[:, 0].set(st[:, 0]).at[:, 1].set(st[:, 1]).reshape(n_s, d)
    h_s, u_s = _mix0_sample(x_sample.reshape(n_s, d), g_mix0, w_in0, cw, w_out0, s1, s2, ss)
    conv_sample = u_s.reshape(bs, ss, d)[:, ss - (CONV_WIDTH - 1):][None]

    ffn0_w = (_row(norm_ffn[0]), bf(ffn_w_gate[0]), bf(ffn_w_up[0]), bf(ffn_w_down[0]),
              _row(norm_ple[0]), bf(ple_w_gate[0]), bf(ple_w_proj[0]))
    h_p = _ffn0(h_p.reshape(n_p, d), p_prompt[0].reshape(n_p, -1), *ffn0_w)
    h_s = _ffn0(h_s, p_sample[0].reshape(n_s, -1), *ffn0_w)

    q_cols = N_HEADS * HEAD_DIM
    kv_cols = N_BRANCHES * KV_COLS
    w_in1 = nsa_w_in[0]
    wq = bf(w_in1[:, :q_cols])
    wkv = w_in1[:, q_cols:q_cols + kv_cols]
    wkc = bf(wkv[:, :KV_COLS])
    wkvt = bf(wkv.T)
    wv_src = wkv[:, KV_COLS:].reshape(d, 2 * N_KV_HEADS, 2, HEAD_DIM)[:, :, 1]
    wv = bf(jnp.zeros((d, 2 * N_KV_HEADS, LANES), F32).at[:, :, :HEAD_DIM].set(wv_src).reshape(
        d, 2 * KV_COLS))
    wg_src = w_in1[:, q_cols + kv_cols:].reshape(d, N_BRANCHES, N_KV_HEADS, hpg)
    wgt = jnp.zeros((d, N_KV_HEADS, LANES), F32).at[:, :, :N_BRANCHES * hpg].set(
        wg_src.transpose(0, 2, 1, 3).reshape(d, N_KV_HEADS, N_BRANCHES * hpg))
    wgt = bf(wgt.reshape(d, N_KV_HEADS * LANES))
    gi = jnp.arange(MXU_DIM) // HEAD_DIM
    gmat = (gi[:, None] == gi[None, :]).astype(BF16)
    qg = jnp.tile(nsa_q_norm[0], N_HEADS).reshape(1, q_cols).astype(F32)
    kgc = _key_gain_row(nsa_k_norm[0, 0], KV_COLS)
    kgs_col = nsa_k_norm[0, 1].reshape(HEAD_DIM, 1).astype(F32)
    kgw_col = nsa_k_norm[0, 2].reshape(HEAD_DIM, 1).astype(F32)
    proj_w = (_row(norm_mix[1]), wq, wkc, wkvt, wv, wgt, gmat, qg, kgs_col, kgw_col)

    per_tile = MXU_DIM // HEAD_DIM
    wc = nsa_w_cmp[0].reshape(CMP_BLOCK, 2, per_tile, HEAD_DIM, HEAD_DIM)
    w2 = bf(jnp.einsum('lhade,ab->lhadbe', wc, jnp.eye(per_tile, dtype=F32)).reshape(
        CMP_BLOCK, 2, MXU_DIM, MXU_DIM))

    thr = _bucket_thresholds()
    tbl = rel_bias.astype(F32)

    q_p, kvc_p, kvct_p, kvst_p, kvwt_p, ksn_p, kwn_p, vr_p, gt_p = _nsa_proj(h_p, bp, *proj_w)
    kcn_p = _compress_prompt(kvc_p, w2, gmat, kgc)
    o_p = _nsa_prompt_attn(tbl, thr, q_p, kcn_p, ksn_p, kwn_p, vr_p, gt_p, bp, sp)

    q_s, _, kvct_s, kvst_s, kvwt_s, ksn_s, kwn_s, _, gt_s = _nsa_proj(h_s, 1, *proj_w)
    pt_flat = page_table.reshape(-1).astype(jnp.int32)
    kcn_s = _compress_paged(pt_flat, _rows_on_lanes(cache_cmp[0]), w2, gmat, kgc, PAGES_PER_STEP)
    q5 = q_s.astype(F32).reshape(bs, ss, N_KV_HEADS, hpg, HEAD_DIM)
    q_rt = q5.transpose(0, 2, 3, 1, 4).reshape(bs, N_KV_HEADS, hpg * ss, HEAD_DIM)
    q_tr = jnp.zeros((bs, N_KV_HEADS, ss, SUBLANES, HEAD_DIM), F32).at[:, :, :, :hpg].set(
        q5.transpose(0, 2, 1, 3, 4))
    tbl_gr = tbl.T.reshape(N_KV_HEADS, hpg, N_BUCKETS)
    tbl_rt = jnp.repeat(tbl_gr, ss, axis=1)
    tbl_r8 = jnp.zeros((N_KV_HEADS, SUBLANES, N_BUCKETS), F32).at[:, :hpg].set(tbl_gr)
    oc_s, idx = _sample_cmp(thr, q_rt, kcn_s, tbl_rt, bs, past_len, ss)
    n_top_s = min(N_SELECTED - 1, past_len // SEL_BLOCK)
    idx_flat = idx[:, :, :n_top_s + 1].reshape(-1)

    def new_rows(kn_t):
        r = kn_t[0].astype(F32).reshape(N_KV_HEADS, GROUP_COLS, bs, ss).transpose(2, 0, 3, 1)
        return jnp.zeros((bs, N_KV_HEADS, SUBLANES, GROUP_COLS), F32).at[:, :, :ss].set(r)

    win_t = _rows_on_lanes(state_win[0])
    os_s, ow_s = _sample_sel_win(idx_flat, pt_flat, thr, _rows_on_lanes(cache_sel[0]), q_tr,
                                 new_rows(ksn_s), new_rows(kwn_s), win_t, tbl_r8,
                                 kgs_col, kgw_col, bs, past_len, ss)
    oc_s2 = oc_s.reshape(bs, N_KV_HEADS, hpg, ss, HEAD_DIM).transpose(0, 3, 1, 2, 4).reshape(n_s, d)
    to2d = lambda o: o[:, :, :, :hpg].transpose(0, 2, 1, 3, 4).reshape(n_s, d)

    w_out1 = bf(nsa_w_out[0])
    router = jnp.zeros((d, LANES), F32).at[:, :N_EXPERTS].set(moe_router[0])
    r_hi = bf(router)
    r_lo = bf(router - r_hi.astype(F32))
    gf1 = _row(norm_ffn[1])
    moe_w = (bf(moe_w_gate[0]), bf(moe_w_up[0]), bf(moe_w_down[0]), _row(norm_ple[1]),
             bf(ple_w_gate[1]), bf(ple_w_proj[1]))
    h2_p, f_p, gate_p = _attn_out(h_p, [o_p], None, w_out1, gf1, r_hi, r_lo)
    y_p = _moe(f_p, h2_p, gate_p, p_prompt[1].reshape(n_p, -1), *moe_w)
    h2_s, f_s, gate_s = _attn_out(h_s, [oc_s2, to2d(os_s), to2d(ow_s)], gt_s, w_out1, gf1,
                                  r_hi, r_lo)
    y_s = _moe(f_s, h2_s, gate_s, p_sample[1].reshape(n_s, -1), *moe_w)

    sample_rows = lambda xt: _rows_on_lanes_inverse(
        xt[0].reshape(KV_COLS, bs, ss).transpose(1, 0, 2))
    wbp = min(WINDOW, sp)
    win_s_t = jnp.concatenate(
        [win_t, kvwt_s[0].reshape(N_KV_HEADS, GROUP_COLS, bs, ss).transpose(2, 0, 1, 3)],
        axis=3)[..., ss:]
    return (y_p.reshape(bp, sp, d), y_s.reshape(bs, ss, d), conv_prompt, conv_sample,
            _rows_on_lanes_inverse(kvct_p), sample_rows(kvct_s),
            _rows_on_lanes_inverse(kvst_p), sample_rows(kvst_s),
            _rows_on_lanes_inverse(kvwt_p[:, :, sp - wbp:]),
            _rows_on_lanes_inverse(win_s_t.reshape(bs, KV_COLS, -1)))
```

```python
import functools
import math

import jax
import jax.numpy as jnp
from jax import lax
from jax.experimental import pallas as pl
from jax.experimental.pallas import tpu as pltpu

F32 = jnp.float32
BF16 = jnp.bfloat16

D_MODEL = 1024
N_HEADS = 16
HEAD_DIM = 64
N_KV_HEADS = 4
HEADS_PER_GROUP = 4
N_BRANCHES = 3
CMP_BLOCK = 32
SEL_BLOCK = 64
CMP_PER_SEL = SEL_BLOCK // CMP_BLOCK
N_SELECTED = 16
WINDOW = 512
N_BUCKETS = 32
MAX_DISTANCE = 128
N_EXPERTS = 8
CONV_WIDTH = 3
EPS = 1e-6
FORCED_SCORE = 1e4
NEG = -1e30
KV_COLS = N_KV_HEADS * 2 * HEAD_DIM
GROUP_COLS = 2 * HEAD_DIM
Q_GROUP_COLS = HEADS_PER_GROUP * HEAD_DIM

LANES = 128
SUBLANES = 8
MXU_DIM = 256
VMEM_LIMIT = 56 * 1024 * 1024
SLABS = KV_COLS // LANES

TOKEN_TILE = 512
ATTN_TILE = 256
PAGES_PER_STEP = 32


def _cparams(sem, vmem=VMEM_LIMIT):
    return pltpu.CompilerParams(dimension_semantics=sem, vmem_limit_bytes=vmem)


def _const_spec(shape):
    nd = len(shape)
    return pl.BlockSpec(shape, lambda *_: (0,) * nd, pipeline_mode=pl.Buffered(1))


def _smem_spec():
    return pl.BlockSpec(memory_space=pltpu.SMEM)


def _dot(a, b):
    return jnp.dot(a, b, preferred_element_type=F32)


def _dot_nt(a, b):
    return lax.dot_general(a, b, (((1,), (1,)), ((), ())), preferred_element_type=F32)


def _split3(x):
    hi = x.astype(BF16)
    r1 = x - hi.astype(F32)
    mid = r1.astype(BF16)
    lo = (r1 - mid.astype(F32)).astype(BF16)
    return hi, mid, lo


def _exact_dot(x, m01):
    hi, mid, lo = _split3(x)
    return (_dot(hi, m01) + _dot(mid, m01)) + _dot(lo, m01)


def _rmsnorm(x, g):
    ms = jnp.mean(x * x, axis=-1, keepdims=True)
    return x * lax.rsqrt(ms + EPS) * g


def _group_mean_sq(x, gmat):
    n = x.shape[1]
    w = gmat.shape[0]
    outs = []
    for c in range(n // w):
        blk = x[:, c * w:(c + 1) * w]
        sq = blk * blk
        hi = sq.astype(BF16)
        lo = (sq - hi.astype(F32)).astype(BF16)
        outs.append(_dot(hi, gmat) + _dot(lo, gmat))
    out = outs[0] if len(outs) == 1 else jnp.concatenate(outs, axis=1)
    return out * (1.0 / HEAD_DIM)


def _norm_keys(kv, gain_row, gmat):
    ms = _group_mean_sq(kv, gmat)
    lane = lax.broadcasted_iota(jnp.int32, kv.shape, 1)
    is_k = (lane % GROUP_COLS) < HEAD_DIM
    return jnp.where(is_k, kv * lax.rsqrt(ms + EPS) * gain_row, kv)


def _bucket_bias(dist, thr_ref, value_of_bucket):
    val = value_of_bucket(0)
    val = jnp.broadcast_to(val, dist.shape).astype(F32)
    for k in range(1, N_BUCKETS):
        val = jnp.where(dist >= thr_ref[k], value_of_bucket(k), val)
    return val


def _silu(x):
    return x * jax.nn.sigmoid(x)


def _ple(h, p, gain, wpg_ref, wpp_ref):
    r = _rmsnorm(h, gain).astype(BF16)
    g = jax.nn.sigmoid(_dot(r, wpg_ref[...]))
    return h + g * _dot(p.astype(BF16), wpp_ref[...])


def _conv_mix_tail(x, gb, u, um1, um2, cw_ref, wout_ref):
    conv = cw_ref[0:1, :] * um2 + cw_ref[1:2, :] * um1 + cw_ref[2:3, :] * u
    y = _dot((gb * conv).astype(BF16), wout_ref[...])
    return x + y


def _mix0_prompt_kernel(x_ref, g_ref, win_ref, cw_ref, wout_ref, h_ref, st_ref, carry_ref):
    j = pl.program_id(1)
    tm = x_ref.shape[0]

    @pl.when(j == 0)
    def _():
        carry_ref[...] = jnp.zeros_like(carry_ref)

    x = x_ref[...]
    a = _rmsnorm(x, g_ref[...]).astype(BF16)
    proj = _dot(a, win_ref[...])
    gb = proj[:, :D_MODEL]
    u = proj[:, D_MODEL:2 * D_MODEL] * proj[:, 2 * D_MODEL:]
    c0 = carry_ref[SUBLANES - 2:SUBLANES - 1, :]
    c1 = carry_ref[SUBLANES - 1:SUBLANES, :]
    row = lax.broadcasted_iota(jnp.int32, u.shape, 0)
    um1 = jnp.where(row == 0, c1, pltpu.roll(u, 1, 0))
    um2 = jnp.where(row == 0, c0, jnp.where(row == 1, c1, pltpu.roll(u, 2, 0)))
    h_ref[...] = _conv_mix_tail(x, gb, u, um1, um2, cw_ref, wout_ref)
    tail = u[tm - SUBLANES:, :]
    carry_ref[...] = tail
    st_ref[...] = tail


def _mix0_sample_kernel(x_ref, g_ref, win_ref, cw_ref, wout_ref, s1_ref, s2_ref, h_ref, u_ref,
                        *, seq):
    x = x_ref[...]
    a = _rmsnorm(x, g_ref[...]).astype(BF16)
    proj = _dot(a, win_ref[...])
    gb = proj[:, :D_MODEL]
    u = proj[:, D_MODEL:2 * D_MODEL] * proj[:, 2 * D_MODEL:]
    t = lax.broadcasted_iota(jnp.int32, u.shape, 0) % seq
    um1 = jnp.where(t >= 1, pltpu.roll(u, 1, 0), s1_ref[...])
    um2 = jnp.where(t >= 2, pltpu.roll(u, 2, 0), s2_ref[...])
    h_ref[...] = _conv_mix_tail(x, gb, u, um1, um2, cw_ref, wout_ref)
    u_ref[...] = u


def _mix0_prompt(x, gain, w_in, cw, w_out):
    b, s, d = x.shape
    tm = min(TOKEN_TILE, s)
    grid = (b, s // tm)
    return pl.pallas_call(
        _mix0_prompt_kernel,
        grid=grid,
        in_specs=[
            pl.BlockSpec((None, tm, d), lambda i, j: (i, j, 0)),
            _const_spec((1, d)),
            _const_spec(w_in.shape),
            _const_spec(cw.shape),
            _const_spec(w_out.shape),
        ],
        out_specs=[
            pl.BlockSpec((None, tm, d), lambda i, j: (i, j, 0)),
            pl.BlockSpec((None, SUBLANES, d), lambda i, j: (i, 0, 0)),
        ],
        out_shape=[
            jax.ShapeDtypeStruct((b, s, d), F32),
            jax.ShapeDtypeStruct((b, SUBLANES, d), F32),
        ],
        scratch_shapes=[pltpu.VMEM((SUBLANES, d), F32)],
        compiler_params=_cparams(("arbitrary", "arbitrary")),
        name="mix0_prompt",
    )(x, gain, w_in, cw, w_out)


def _mix0_sample(x2d, gain, w_in, cw, w_out, s1, s2, seq):
    n, d = x2d.shape
    return pl.pallas_call(
        functools.partial(_mix0_sample_kernel, seq=seq),
        out_shape=[jax.ShapeDtypeStruct((n, d), F32), jax.ShapeDtypeStruct((n, d), F32)],
        compiler_params=_cparams(None),
        name="mix0_sample",
    )(x2d, gain, w_in, cw, w_out, s1, s2)


def _ffn0_kernel(h_ref, p_ref, gf_ref, wg_ref, wu_ref, wd_ref, gp_ref, wpg_ref, wpp_ref, o_ref,
                 *, chunk):
    h = h_ref[...]
    f = _rmsnorm(h, gf_ref[...]).astype(BF16)
    d_ff = wg_ref.shape[1]
    acc = None
    for c in range(d_ff // chunk):
        sl = slice(c * chunk, (c + 1) * chunk)
        hid = _silu(_dot(f, wg_ref[:, sl])) * _dot(f, wu_ref[:, sl])
        part = _dot(hid.astype(BF16), wd_ref[sl, :])
        acc = part if acc is None else acc + part
    o_ref[...] = _ple(h + acc, p_ref[...], gp_ref[...], wpg_ref, wpp_ref)


def _ffn0(h2d, p2d, gf, wg, wu, wd, gp, wpg, wpp):
    n, d = h2d.shape
    tm = min(TOKEN_TILE, n)
    d_ff = wg.shape[1]
    chunk = d_ff // 2
    assert chunk % LANES == 0
    return pl.pallas_call(
        functools.partial(_ffn0_kernel, chunk=chunk),
        grid=(n // tm,),
        in_specs=[
            pl.BlockSpec((tm, d), lambda i: (i, 0)),
            pl.BlockSpec((tm, p2d.shape[1]), lambda i: (i, 0)),
            _const_spec((1, d)),
            _const_spec(wg.shape), _const_spec(wu.shape), _const_spec(wd.shape),
            _const_spec((1, d)),
            _const_spec(wpg.shape), _const_spec(wpp.shape),
        ],
        out_specs=pl.BlockSpec((tm, d), lambda i: (i, 0)),
        out_shape=jax.ShapeDtypeStruct((n, d), F32),
        compiler_params=_cparams(("arbitrary",)),
        name="ffn0_ple",
    )(h2d, p2d, gf, wg, wu, wd, gp, wpg, wpp)


def _norm_keys_t(kvt, gain_col):
    parts = []
    for gi in range(KV_COLS // HEAD_DIM):
        x = kvt[gi * HEAD_DIM:(gi + 1) * HEAD_DIM, :]
        if gi % 2 == 0:
            ms = jnp.mean(x * x, axis=0, keepdims=True)
            x = x * lax.rsqrt(ms + EPS) * gain_col
        parts.append(x)
    return jnp.concatenate(parts, axis=0)


def _nsa_proj_kernel(h_ref, g_ref, wq_ref, wkc_ref, wkvt_ref, wv_ref, wgt_ref, gm_ref, qg_ref,
                     kgs_ref, kgw_ref, q_ref, kvc_ref, kvct_ref, kvst_ref, kvwt_ref, ksn_ref,
                     kwn_ref, vr_ref, gt_ref):
    a = _rmsnorm(h_ref[...], g_ref[...]).astype(BF16)
    vr = _dot(a, wv_ref[...])
    lane = lax.broadcasted_iota(jnp.int32, vr.shape, 1)
    vr_ref[...] = jnp.where(lane % LANES < HEAD_DIM, vr, 1.0).astype(vr_ref.dtype)
    q = _dot(a, wq_ref[...])
    ms = _group_mean_sq(q, gm_ref[...])
    q_ref[...] = ((q * lax.rsqrt(ms + EPS) * qg_ref[...]) * (HEAD_DIM ** -0.5)).astype(q_ref.dtype)
    kvc = _dot(a, wkc_ref[...])
    for g in range(SLABS):
        kvc_ref[g] = kvc[:, g * LANES:(g + 1) * LANES]
    kvt = _dot_nt(wkvt_ref[...], a)
    kvst = kvt[KV_COLS:2 * KV_COLS]
    kvwt = kvt[2 * KV_COLS:]
    kvct_ref[...] = kvt[:KV_COLS]
    kvst_ref[...] = kvst
    kvwt_ref[...] = kvwt
    ksn_ref[...] = _norm_keys_t(kvst, kgs_ref[...]).astype(ksn_ref.dtype)
    kwn_ref[...] = _norm_keys_t(kvwt, kgw_ref[...]).astype(kwn_ref.dtype)
    gt_ref[...] = jax.nn.sigmoid(_dot(a, wgt_ref[...]))


def _nsa_proj(h2d, n_seq, gain, wq, wkc, wkvt, wv, wgt, gmat, qg, kgs_col, kgw_col):
    n, d = h2d.shape
    s = n // n_seq
    tm = min(TOKEN_TILE, s)
    tps = s // tm
    row = lambda w: pl.BlockSpec((tm, w), lambda i: (i, 0))
    col = pl.BlockSpec((None, KV_COLS, tm), lambda i: (i // tps, 0, i % tps))
    tshape = lambda dt: jax.ShapeDtypeStruct((n_seq, KV_COLS, s), dt)
    n_gate_cols = wgt.shape[1]
    return pl.pallas_call(
        _nsa_proj_kernel,
        grid=(n // tm,),
        in_specs=[
            row(d), _const_spec((1, d)),
            _const_spec(wq.shape), _const_spec(wkc.shape), _const_spec(wkvt.shape),
            _const_spec(wv.shape), _const_spec(wgt.shape), _const_spec(gmat.shape),
            _const_spec(qg.shape), _const_spec(kgs_col.shape), _const_spec(kgw_col.shape),
        ],
        out_specs=[row(d), pl.BlockSpec((SLABS, tm, LANES), lambda i: (0, i, 0)),
                   col, col, col, col, col, row(wv.shape[1]), row(n_gate_cols)],
        out_shape=[
            jax.ShapeDtypeStruct((n, d), BF16),
            jax.ShapeDtypeStruct((SLABS, n, LANES), F32),
            tshape(F32), tshape(F32), tshape(F32), tshape(BF16), tshape(BF16),
            jax.ShapeDtypeStruct((n, wv.shape[1]), BF16),
            jax.ShapeDtypeStruct((n, n_gate_cols), F32),
        ],
        compiler_params=_cparams(("arbitrary",)),
        name="nsa_proj",
    )(h2d, gain, wq, wkc, wkvt, wv, wgt, gmat, qg, kgs_col, kgw_col)


def _compress_rows(read_slab, n_blocks, w_ref, gmat, kg_row):
    half = KV_COLS // 2
    per_half = SLABS // 2
    acc = [jnp.zeros((n_blocks, half), F32) for _ in range(2)]
    for l in range(CMP_BLOCK):
        for hf in range(2):
            xl = jnp.concatenate([read_slab(l, hf * per_half + c) for c in range(per_half)], axis=1)
            acc[hf] = acc[hf] + _dot(xl.astype(BF16), w_ref[l, hf])
    kv = jnp.concatenate(acc, axis=1)
    return _norm_keys(kv, kg_row, gmat).astype(BF16)


def _compress_kernel(x_ref, w_ref, gm_ref, kg_ref, o_ref):
    nb = o_ref.shape[0]
    read = lambda l, g: x_ref[g, pl.ds(l, nb, stride=CMP_BLOCK), :]
    o_ref[...] = _compress_rows(read, nb, w_ref, gm_ref[...], kg_ref[...])


def _compress_prompt(slabs, w2, gmat, kg):
    n = slabs.shape[1]
    tr = min(4096, n)
    nb = tr // CMP_BLOCK
    return pl.pallas_call(
        _compress_kernel,
        grid=(n // tr,),
        in_specs=[pl.BlockSpec((SLABS, tr, LANES), lambda i: (0, i, 0)),
                  _const_spec(w2.shape), _const_spec(gmat.shape), _const_spec(kg.shape)],
        out_specs=pl.BlockSpec((nb, KV_COLS), lambda i: (i, 0)),
        out_shape=jax.ShapeDtypeStruct((n // CMP_BLOCK, KV_COLS), BF16),
        compiler_params=_cparams(("arbitrary",)),
        name="compress_prompt",
    )(slabs, w2, gmat, kg)


def _compress_paged_kernel(pt_ref, pool_ref, w_ref, gm_ref, kg_ref, o_ref, buf_ref, st_ref, sem_ref,
                           *, pages_per_step, page):
    i = pl.program_id(0)
    n = pl.num_programs(0)

    def page_copy(step, p, slot):
        phys = pt_ref[step * pages_per_step + p]
        return pltpu.make_async_copy(pool_ref.at[phys], buf_ref.at[slot, p], sem_ref.at[slot])

    def start(step, slot):
        def issue(p, c):
            page_copy(step, p, slot).start()
            return c
        lax.fori_loop(0, pages_per_step, issue, 0)

    @pl.when(i == 0)
    def _():
        start(0, 0)

    @pl.when(i + 1 < n)
    def _():
        start(i + 1, (i + 1) % 2)

    slot = i % 2

    def wait(p, c):
        page_copy(i, p, slot).wait()
        return c
    lax.fori_loop(0, pages_per_step, wait, 0)

    blocks_per_page = page // CMP_BLOCK
    group = 2 * blocks_per_page
    assert group == SUBLANES
    ri = lax.broadcasted_iota(jnp.int32, (2 * page, 2 * page), 0)
    ci = lax.broadcasted_iota(jnp.int32, (2 * page, 2 * page), 1)
    src_lane = ((ri % group) // blocks_per_page) * page + (ri % blocks_per_page) * CMP_BLOCK + ri // group
    perm = (ci == src_lane).astype(BF16)

    def to_rows(pp, c):
        for g0 in range(0, N_KV_HEADS, 2):
            pair = jnp.concatenate([buf_ref[slot, 2 * pp, g0:g0 + 2].reshape(2 * GROUP_COLS, page),
                                    buf_ref[slot, 2 * pp + 1, g0:g0 + 2].reshape(2 * GROUP_COLS, page)],
                                   axis=1).astype(BF16)
            rows = _dot_nt(perm, pair)
            for l in range(CMP_BLOCK):
                dst = pl.ds(pl.multiple_of(pp * group, group), group)
                blk = rows[l * group:(l + 1) * group]
                st_ref[g0, l, dst, :] = blk[:, :GROUP_COLS]
                st_ref[g0 + 1, l, dst, :] = blk[:, GROUP_COLS:]
        return c
    lax.fori_loop(0, pages_per_step // 2, to_rows, 0)

    nb = pages_per_step * page // CMP_BLOCK
    read = lambda l, g: st_ref[g, l]
    o_ref[...] = _compress_rows(read, nb, w_ref, gm_ref[...], kg_ref[...])


def _compress_paged(page_table_flat, pool_t, w2, gmat, kg, pages_per_step):
    n_pages = page_table_flat.shape[0]
    page = pool_t.shape[3]
    assert n_pages % pages_per_step == 0 and page == LANES and pool_t.shape[2] == GROUP_COLS
    steps = n_pages // pages_per_step
    nb = pages_per_step * page // CMP_BLOCK
    grid_spec = pltpu.PrefetchScalarGridSpec(
        num_scalar_prefetch=1,
        grid=(steps,),
        in_specs=[pl.BlockSpec(memory_space=pl.ANY),
                  pl.BlockSpec(w2.shape, lambda i, pt: (0, 0, 0, 0), pipeline_mode=pl.Buffered(1)),
                  pl.BlockSpec(gmat.shape, lambda i, pt: (0, 0)),
                  pl.BlockSpec(kg.shape, lambda i, pt: (0, 0))],
        out_specs=pl.BlockSpec((nb, KV_COLS), lambda i, pt: (i, 0)),
        scratch_shapes=[pltpu.VMEM((2, pages_per_step, N_KV_HEADS, GROUP_COLS, page), F32),
                        pltpu.VMEM((N_KV_HEADS, CMP_BLOCK, nb, GROUP_COLS), F32),
                        pltpu.SemaphoreType.DMA((2,))],
    )
    return pl.pallas_call(
        functools.partial(_compress_paged_kernel, pages_per_step=pages_per_step, page=page),
        grid_spec=grid_spec,
        out_shape=jax.ShapeDtypeStruct((steps * nb, KV_COLS), BF16),
        compiler_params=_cparams(("arbitrary",)),
        name="compress_paged",
    )(page_table_flat, pool_t, w2, gmat, kg)


FLASH_ROWS = 128


def _flash_tile(qh_ref, kt_ref, v_ref, kt, tk, state, terms):
    m_ref, acc_ref = state
    start = pl.multiple_of(kt * tk, tk)
    k = kt_ref[:, pl.ds(start, tk)]
    v = v_ref[pl.ds(start, tk), :]
    tq = qh_ref.shape[1]
    blocks = [(r, pl.ds(rb * FLASH_ROWS, FLASH_ROWS))
              for r in range(HEADS_PER_GROUP) for rb in range(tq // FLASH_ROWS)]
    old = [(m_ref[r, rows, :], acc_ref[r, rows, :]) for r, rows in blocks]
    new = []
    for (r, rows), (m_old, acc_old) in zip(blocks, old):
        s = _dot(qh_ref[r, rows, :], k)
        for term in terms(r, rows):
            s = s + term
        m_new = jnp.maximum(m_old, jnp.max(s, axis=-1, keepdims=True))
        e = jnp.exp(s - jnp.concatenate([m_new] * (tk // LANES), axis=1))
        new.append((m_new, jnp.exp(m_old - m_new) * acc_old + _dot(e.astype(BF16), v)))
    for (r, rows), (m_new, acc_new) in zip(blocks, new):
        m_ref[r, rows, :] = m_new
        acc_ref[r, rows, :] = acc_new


def _nsa_prompt_kernel(tbl_ref, thr_ref, q_ref, kc_ref, ks_ref, vs_ref, kw_ref, vw_ref, gt_ref, o_ref,
                       bias_ref, edge_ref, cbias_ref, amask_ref, qh_ref, m_ref, acc_ref, *, n_top):
    g = pl.program_id(0)
    b = pl.program_id(1)
    qi = pl.program_id(2)
    tq = q_ref.shape[0]
    tk = tq
    hpg = HEADS_PER_GROUP
    nc = kc_ref.shape[0]
    ns = nc // CMP_PER_SEL
    blocks_per_tile = tk // SEL_BLOCK
    far = N_BUCKETS - 1

    qpos = qi * tq + lax.broadcasted_iota(jnp.int32, (tq, 1), 0)
    n_io = lax.broadcasted_iota(jnp.int32, (tq, nc), 1)
    relc = qpos - ((n_io + 1) * CMP_BLOCK - 1)

    n_back = WINDOW // tk

    @pl.when((b == 0) & (qi == 0))
    def _():
        diag = (lax.broadcasted_iota(jnp.int32, (tq, tk), 0)
                - lax.broadcasted_iota(jnp.int32, (tq, tk), 1))
        for r in range(hpg):
            h = g * hpg + r
            shifted = lambda k: tbl_ref[k, h] - tbl_ref[far, h]
            bias_ref[r, 0] = jnp.where(diag >= 0, _bucket_bias(diag, thr_ref, shifted), NEG)
            bias_ref[r, 1] = _bucket_bias(diag + tk, thr_ref, shifted)
        edge_ref[...] = jnp.where(diag + n_back * tk < WINDOW, 0.0, NEG)

    @pl.when(b == 0)
    def _():
        for r in range(hpg):
            h = g * hpg + r
            cbias_ref[r, qi] = _bucket_bias(relc, thr_ref, lambda k: tbl_ref[k, h])

    q = q_ref[...]
    for r in range(hpg):
        qh_ref[r] = q[:, r * HEAD_DIM:(r + 1) * HEAD_DIM]

    sel_state = (m_ref.at[0], acc_ref.at[0])
    win_state = (m_ref.at[1], acc_ref.at[1])
    m_ref[...] = jnp.full(m_ref.shape, NEG, F32)
    acc_ref[...] = jnp.zeros(acc_ref.shape, F32)

    def finish(branch):
        out = []
        for r in range(hpg):
            acc = acc_ref[branch, r]
            out.append((acc / pltpu.roll(acc, HEAD_DIM, 1))[:, :HEAD_DIM])
        return out

    def win_tile(kt, terms):
        _flash_tile(qh_ref, kw_ref, vw_ref, kt, tk, win_state, terms)

    win_tile(qi, lambda r, rows: (bias_ref[r, 0, rows, :],))
    no_prev = jnp.where(qi >= 1, 0.0, NEG)
    win_tile(jnp.maximum(qi - 1, 0), lambda r, rows: (bias_ref[r, 1, rows, :], no_prev))

    kc = kc_ref[...]
    kck = kc[:, :HEAD_DIM]
    kcv = kc[:, HEAD_DIM:]
    maskc = relc >= 0
    imp = None
    oc = []
    for r in range(hpg):
        s = _dot_nt(q[:, r * HEAD_DIM:(r + 1) * HEAD_DIM], kck) + cbias_ref[r, qi]
        sm = jnp.where(maskc, s, NEG)
        e = jnp.exp(sm - jnp.max(sm, axis=-1, keepdims=True))
        p = jnp.where(maskc, e / jnp.sum(e, axis=-1, keepdims=True), 0.0)
        imp = p if imp is None else imp + p
        oc.append(_dot(p.astype(BF16), kcv))

    pair_t = (lax.broadcasted_iota(jnp.int32, (ns, nc), 0)
              == lax.broadcasted_iota(jnp.int32, (ns, nc), 1) // CMP_PER_SEL).astype(BF16)
    hi, mid, lo = _split3(imp)
    imp_sel = (_dot_nt(pair_t, hi) + _dot_nt(pair_t, mid)) + _dot_nt(pair_t, lo)
    blk = lax.broadcasted_iota(jnp.int32, (ns, tq), 0)
    cur = (qi * tq + lax.broadcasted_iota(jnp.int32, (1, tq), 1)) // SEL_BLOCK
    forced = (blk == 0) | (blk == cur) | (blk == cur - 1)
    score = jnp.where(forced, FORCED_SCORE, jnp.where(blk <= cur, imp_sel, -1.0))
    rank = jnp.zeros((ns, tq), F32)
    for i in range(ns):
        si = score[i:i + 1, :]
        beats = (si > score) | ((si == score) & (blk > i))
        rank = rank + jnp.where(beats, 1.0, 0.0)
    sel_t = jnp.where(rank < n_top, 1.0, 0.0).astype(BF16)
    eye = (lax.broadcasted_iota(jnp.int32, (tq, tq), 0)
           == lax.broadcasted_iota(jnp.int32, (tq, tq), 1)).astype(BF16)
    sel = _dot_nt(eye, sel_t).astype(BF16)

    def sel_tile(kt, tiles_back):
        expand = (lax.broadcasted_iota(jnp.int32, (ns, tk), 0)
                  == kt * blocks_per_tile + lax.broadcasted_iota(jnp.int32, (ns, tk), 1) // SEL_BLOCK)
        amask_ref[...] = jnp.where(_dot(sel, expand.astype(BF16)) > 0.5, 0.0, NEG)
        if tiles_back is None:
            terms = lambda r, rows: (amask_ref[rows, :],)
        else:
            terms = lambda r, rows: (bias_ref[r, tiles_back, rows, :], amask_ref[rows, :])
        _flash_tile(qh_ref, ks_ref, vs_ref, kt, tk, sel_state, terms)

    def far_tile(kt, c):
        sel_tile(kt, None)
        return c
    lax.fori_loop(0, jnp.maximum(qi - 1, 0), far_tile, 0)

    @pl.when(qi >= n_back)
    def _():
        sel_tile(qi - 1, 1)
        win_tile(qi - n_back, lambda r, rows: (edge_ref[rows, :],))

    @pl.when((qi >= 1) & (qi < n_back))
    def _():
        sel_tile(qi - 1, 1)

    sel_tile(qi, 0)
    o_sel = finish(0)
    o_win = finish(1)

    width = hpg * HEAD_DIM
    idx = lax.broadcasted_iota(jnp.int32, (LANES, N_BRANCHES * width), 0)
    col = lax.broadcasted_iota(jnp.int32, (LANES, N_BRANCHES * width), 1)
    spread = (idx == (col // width) * hpg + (col % width) // HEAD_DIM).astype(BF16)
    gates = _exact_dot(gt_ref[...], spread)
    out = None
    for c, branch in enumerate((oc, o_sel, o_win)):
        term = gates[:, c * width:(c + 1) * width] * jnp.concatenate(branch, axis=1)
        out = term if out is None else out + term
    o_ref[...] = out.astype(o_ref.dtype)


def _nsa_prompt_attn(tbl, thr, q, kcn, ksn_t, kwn_t, v_rows, gates, b, s):
    tq = min(ATTN_TILE, s)
    assert WINDOW == 2 * tq and tq + 1 >= MAX_DISTANCE and tq % FLASH_ROWS == 0
    nq = s // tq
    nc = s // CMP_BLOCK
    ns = s // SEL_BLOCK
    n_top = min(N_SELECTED, ns)
    hpg = HEADS_PER_GROUP
    k_spec = pl.BlockSpec((None, HEAD_DIM, s), lambda g, i, j: (i, 2 * g, 0))
    v_spec = lambda branch: pl.BlockSpec((s, LANES), lambda g, i, j: (i, branch * N_KV_HEADS + g))
    return pl.pallas_call(
        functools.partial(_nsa_prompt_kernel, n_top=n_top),
        grid=(N_KV_HEADS, b, nq),
        in_specs=[
            _smem_spec(), _smem_spec(),
            pl.BlockSpec((tq, Q_GROUP_COLS), lambda g, i, j: (i * nq + j, g)),
            pl.BlockSpec((nc, GROUP_COLS), lambda g, i, j: (i, g)),
            k_spec, v_spec(0), k_spec, v_spec(1),
            pl.BlockSpec((tq, LANES), lambda g, i, j: (i * nq + j, g)),
        ],
        out_specs=pl.BlockSpec((tq, Q_GROUP_COLS), lambda g, i, j: (i * nq + j, g)),
        out_shape=jax.ShapeDtypeStruct((b * s, D_MODEL), BF16),
        scratch_shapes=[pltpu.VMEM((hpg, 2, tq, tq), F32),
                        pltpu.VMEM((tq, tq), F32),
                        pltpu.VMEM((hpg, nq, tq, nc), F32),
                        pltpu.VMEM((tq, tq), F32),
                        pltpu.VMEM((hpg, tq, HEAD_DIM), BF16),
                        pltpu.VMEM((2, hpg, tq, LANES), F32),
                        pltpu.VMEM((2, hpg, tq, LANES), F32)],
        compiler_params=_cparams(("arbitrary", "arbitrary", "arbitrary")),
        name="nsa_prompt_attn",
    )(tbl, thr, q, kcn, ksn_t, v_rows, kwn_t, v_rows, gates)


def _bucket_onehot(dist_row, thr_ref):
    n = dist_row.shape[1]
    bucket = jnp.zeros(dist_row.shape, jnp.int32)
    for k in range(1, N_BUCKETS):
        bucket = bucket + jnp.where(dist_row >= thr_ref[k], 1, 0)
    return (lax.broadcasted_iota(jnp.int32, (N_BUCKETS, n), 0) == bucket).astype(BF16)


def _sample_cmp_kernel(thr_ref, q_ref, kc_ref, tbl_ref, oc_ref, idx_ref, *, past_len, seq, n_top):
    nc = kc_ref.shape[0]
    ns = past_len // SEL_BLOCK
    hpg = HEADS_PER_GROUP
    rows = hpg * seq
    kc = kc_ref[...]
    pair = (lax.broadcasted_iota(jnp.int32, (nc, ns), 0) // CMP_PER_SEL
            == lax.broadcasted_iota(jnp.int32, (nc, ns), 1)).astype(BF16)
    eye = (lax.broadcasted_iota(jnp.int32, (ns, ns), 0)
           == lax.broadcasted_iota(jnp.int32, (ns, ns), 1))
    eye_bf = eye.astype(BF16)
    ii = lax.broadcasted_iota(jnp.int32, (ns, ns), 0)
    jj = lax.broadcasted_iota(jnp.int32, (ns, ns), 1)
    t_row = lax.broadcasted_iota(jnp.int32, (rows, nc), 0) % seq
    n_io = lax.broadcasted_iota(jnp.int32, (rows, nc), 1)
    rel = past_len + t_row - ((n_io + 1) * CMP_BLOCK - 1)
    mask = rel >= 0
    blk = lax.broadcasted_iota(jnp.int32, (SUBLANES, ns), 1)
    forced = (blk == 0) | (blk == ns - 1)
    rank_lane = lax.broadcasted_iota(jnp.int32, (ns, LANES), 1).astype(F32)
    blk_col = lax.broadcasted_iota(jnp.int32, (ns, LANES), 0).astype(F32)
    for g in range(N_KV_HEADS):
        kck = kc[:, g * GROUP_COLS:g * GROUP_COLS + HEAD_DIM]
        kcv = kc[:, g * GROUP_COLS + HEAD_DIM:(g + 1) * GROUP_COLS]
        qg = q_ref[g].astype(BF16)
        tbl_g = tbl_ref[g]
        s = _dot_nt(qg, kck)
        s = s + _bucket_bias(rel, thr_ref, lambda k: tbl_g[:, k:k + 1])
        sm = jnp.where(mask, s, NEG)
        e = jnp.exp(sm - jnp.max(sm, axis=-1, keepdims=True))
        p = jnp.where(mask, e / jnp.sum(e, axis=-1, keepdims=True), 0.0)
        oc_ref[g] = _dot(p.astype(BF16), kcv)
        imp = p[0:seq]
        for r in range(1, hpg):
            imp = imp + p[r * seq:(r + 1) * seq]
        imp = jnp.concatenate([imp, jnp.zeros((SUBLANES - seq, nc), F32)], axis=0)
        score = jnp.where(forced, FORCED_SCORE, _exact_dot(imp, pair))
        hi, mid, lo = _split3(score)
        score_t = (_dot_nt(eye_bf, hi) + _dot_nt(eye_bf, mid)) + _dot_nt(eye_bf, lo)
        for t in range(seq):
            s_row = score[t:t + 1, :]
            s_col = score_t[:, t:t + 1]
            beats = (ii != jj) & ((s_row > s_col) | ((s_row == s_col) & (jj < ii)))
            rank_col = jnp.sum(jnp.where(beats, 1.0, 0.0), axis=1, keepdims=True)
            onehot = rank_col == rank_lane
            idx_row = jnp.sum(jnp.where(onehot, blk_col, 0.0), axis=0, keepdims=True)
            idx_ref[g * seq + t:g * seq + t + 1, :] = idx_row.astype(jnp.int32)


def _sample_cmp(thr, q_rt, kcn, tbl_rt, b, past_len, seq):
    nc = kcn.shape[0] // b
    ns = past_len // SEL_BLOCK
    n_top = min(N_SELECTED - 1, ns)
    rows = HEADS_PER_GROUP * seq
    return pl.pallas_call(
        functools.partial(_sample_cmp_kernel, past_len=past_len, seq=seq, n_top=n_top),
        grid=(b,),
        in_specs=[
            _smem_spec(),
            pl.BlockSpec((None, N_KV_HEADS, rows, HEAD_DIM), lambda i: (i, 0, 0, 0)),
            pl.BlockSpec((nc, KV_COLS), lambda i: (i, 0)),
            _const_spec(tbl_rt.shape),
        ],
        out_specs=[
            pl.BlockSpec((None, N_KV_HEADS, rows, HEAD_DIM), lambda i: (i, 0, 0, 0)),
            pl.BlockSpec((None, N_KV_HEADS * seq, LANES), lambda i: (i, 0, 0)),
        ],
        out_shape=[
            jax.ShapeDtypeStruct((b, N_KV_HEADS, rows, HEAD_DIM), F32),
            jax.ShapeDtypeStruct((b, N_KV_HEADS * seq, LANES), jnp.int32),
        ],
        compiler_params=_cparams(("arbitrary",)),
        name="sample_cmp",
    )(thr, q_rt, kcn, tbl_rt)


def _norm_k_cols(kt, gain_col):
    ms = jnp.mean(kt * kt, axis=0, keepdims=True)
    return (kt * lax.rsqrt(ms + EPS) * gain_col).astype(BF16)


def _sample_sel_win_kernel(idx_ref, pt_ref, thr_ref, pool_ref, q_ref, ksnew_ref, kwnew_ref,
                           win_ref, tbl_ref, kgs_ref, kgw_ref, os_ref, ow_ref,
                           buf_ref, sem_ref, *, past_len, seq, n_top, page):
    i = pl.program_id(0)
    n_pages = past_len // page
    blocks_per_page = page // SEL_BLOCK
    n_keys = n_top * page
    stride = n_top + 1

    def block_of(g, t, j):
        return idx_ref[(i * N_KV_HEADS * seq + g * seq + t) * stride + j]

    def tile_copy(g, t, j):
        phys = pt_ref[i * n_pages + block_of(g, t, j) // blocks_per_page]
        return pltpu.make_async_copy(
            pool_ref.at[phys, g],
            buf_ref.at[g * seq + t, :, pl.ds(pl.multiple_of(j * page, page), page)],
            sem_ref.at[0])

    def for_all_tiles(fn):
        for g in range(N_KV_HEADS):
            for t in range(seq):
                def per_tile(j, c):
                    fn(tile_copy(g, t, j))
                    return c
                lax.fori_loop(0, n_top, per_tile, 0)

    for_all_tiles(lambda cp: cp.start())

    wb = win_ref.shape[2]
    wk = [_norm_k_cols(win_ref[g, 0:HEAD_DIM, :], kgw_ref[...]) for g in range(N_KV_HEADS)]
    wv = [win_ref[g, HEAD_DIM:GROUP_COLS, :].astype(BF16) for g in range(N_KV_HEADS)]
    j_io = lax.broadcasted_iota(jnp.int32, (1, wb), 1)
    tnew = lax.broadcasted_iota(jnp.int32, (1, SUBLANES), 1)

    def attend(q, tbl_g, k_past, v_past, buckets, mask, knew, bucketsn, maskn):
        s1 = jnp.where(mask, _dot(q, k_past) + _exact_dot(tbl_g, buckets), NEG)
        s2 = jnp.where(maskn, _dot_nt(q, knew[:, :HEAD_DIM].astype(BF16))
                       + _exact_dot(tbl_g, bucketsn), NEG)
        m = jnp.maximum(jnp.max(s1, axis=-1, keepdims=True), jnp.max(s2, axis=-1, keepdims=True))
        e1 = jnp.where(mask, jnp.exp(s1 - m), 0.0)
        e2 = jnp.where(maskn, jnp.exp(s2 - m), 0.0)
        l = jnp.sum(e1, axis=-1, keepdims=True) + jnp.sum(e2, axis=-1, keepdims=True)
        o = _dot_nt(e1.astype(BF16), v_past) + _dot(e2.astype(BF16),
                                                    knew[:, HEAD_DIM:].astype(BF16))
        return o / l

    def window(t, c):
        dist = wb + t - j_io
        maskw = (dist >= 0) & (dist < WINDOW)
        distn = t - tnew
        maskn = (distn >= 0) & (distn < WINDOW)
        buckets = _bucket_onehot(dist, thr_ref)
        bucketsn = _bucket_onehot(distn, thr_ref)
        for g in range(N_KV_HEADS):
            ow_ref[g, t] = attend(q_ref[g, t].astype(BF16), tbl_ref[g], wk[g], wv[g], buckets,
                                  maskw, kwnew_ref[g], bucketsn, maskn)
        return c
    lax.fori_loop(0, seq, window, 0)

    for_all_tiles(lambda cp: cp.wait())

    lane = lax.broadcasted_iota(jnp.int32, (1, n_keys), 1)
    tile_of_lane = lane // page
    row_in_page = lane % page

    def selected(t, c):
        distn = t - tnew
        maskn = distn >= 0
        bucketsn = _bucket_onehot(distn, thr_ref)
        for g in range(N_KV_HEADS):
            kvt = buf_ref[g * seq + t]
            page_base = jnp.zeros((1, n_keys), jnp.int32)
            half = jnp.zeros((1, n_keys), jnp.int32)
            for j in range(n_top):
                blk = block_of(g, t, j)
                page_base = jnp.where(tile_of_lane == j, (blk // blocks_per_page) * page, page_base)
                half = jnp.where(tile_of_lane == j, blk % blocks_per_page, half)
            mask = (row_in_page // SEL_BLOCK) == half
            dist = past_len + t - (page_base + row_in_page)
            os_ref[g, t] = attend(q_ref[g, t].astype(BF16), tbl_ref[g],
                                  _norm_k_cols(kvt[0:HEAD_DIM], kgs_ref[...]),
                                  kvt[HEAD_DIM:GROUP_COLS].astype(BF16),
                                  _bucket_onehot(dist, thr_ref), mask,
                                  ksnew_ref[g], bucketsn, maskn)
        return c
    lax.fori_loop(0, seq, selected, 0)


def _sample_sel_win(idx_flat, pt_flat, thr, pool_t, q_tr, ksnew, kwnew, win_t, tbl_r8,
                    kgs_col, kgw_col, b, past_len, seq):
    ns = past_len // SEL_BLOCK
    n_top = min(N_SELECTED - 1, ns)
    page = pool_t.shape[3]
    wb = win_t.shape[3]
    o_shape = jax.ShapeDtypeStruct((b, N_KV_HEADS, seq, SUBLANES, HEAD_DIM), F32)
    o_spec = pl.BlockSpec((None, N_KV_HEADS, seq, SUBLANES, HEAD_DIM),
                          lambda i, *_: (i, 0, 0, 0, 0))
    new_spec = pl.BlockSpec((None, N_KV_HEADS, SUBLANES, GROUP_COLS), lambda i, *_: (i, 0, 0, 0))
    grid_spec = pltpu.PrefetchScalarGridSpec(
        num_scalar_prefetch=3,
        grid=(b,),
        in_specs=[
            pl.BlockSpec(memory_space=pl.ANY),
            pl.BlockSpec((None, N_KV_HEADS, seq, SUBLANES, HEAD_DIM), lambda i, *_: (i, 0, 0, 0, 0)),
            new_spec, new_spec,
            pl.BlockSpec((None, N_KV_HEADS, GROUP_COLS, wb), lambda i, *_: (i, 0, 0, 0)),
            pl.BlockSpec(tbl_r8.shape, lambda i, *_: (0, 0, 0)),
            pl.BlockSpec(kgs_col.shape, lambda i, *_: (0, 0)),
            pl.BlockSpec(kgw_col.shape, lambda i, *_: (0, 0)),
        ],
        out_specs=[o_spec, o_spec],
        scratch_shapes=[pltpu.VMEM((N_KV_HEADS * seq, GROUP_COLS, n_top * page), F32),
                        pltpu.SemaphoreType.DMA((1,))],
    )
    return pl.pallas_call(
        functools.partial(_sample_sel_win_kernel, past_len=past_len, seq=seq, n_top=n_top,
                          page=page),
        grid_spec=grid_spec,
        out_shape=[o_shape, o_shape],
        compiler_params=_cparams(("arbitrary",)),
        name="sample_sel_win",
    )(idx_flat, pt_flat, thr, pool_t, q_tr, ksnew, kwnew, win_t, tbl_r8, kgs_col, kgw_col)


def _route(f, rhi_ref, rlo_ref):
    f_hi = f.astype(BF16)
    f_lo = (f - f_hi.astype(F32)).astype(BF16)
    logits = (_dot(f_hi, rhi_ref[...]) + _dot(f_lo, rhi_ref[...])) + _dot(f_hi, rlo_ref[...])
    lane = lax.broadcasted_iota(jnp.int32, logits.shape, 1).astype(F32)
    logits = jnp.where(lane < N_EXPERTS, logits, -jnp.inf)
    m1 = jnp.max(logits, axis=-1, keepdims=True)
    i1 = jnp.min(jnp.where(logits == m1, lane, float(LANES)), axis=-1, keepdims=True)
    rest = jnp.where(lane == i1, -jnp.inf, logits)
    m2 = jnp.max(rest, axis=-1, keepdims=True)
    i2 = jnp.min(jnp.where(rest == m2, lane, float(LANES)), axis=-1, keepdims=True)
    e2 = jnp.exp(m2 - m1)
    denom = 1.0 + e2
    return jnp.where(lane == i1, 1.0 / denom, 0.0) + jnp.where(lane == i2, e2 / denom, 0.0)


def _attn_out_kernel(h_ref, o_ref, wout_ref, gf_ref, rhi_ref, rlo_ref, h2_ref, f_ref, gate_ref):
    tm = h_ref.shape[0]
    halves = 2 if tm % (2 * SUBLANES) == 0 else 1
    for i in range(halves):
        rows = pl.ds(i * (tm // halves), tm // halves)
        h2 = h_ref[rows, :] + _dot(o_ref[rows, :], wout_ref[...])
        h2_ref[rows, :] = h2
        f = _rmsnorm(h2, gf_ref[...])
        f_ref[rows, :] = f.astype(f_ref.dtype)
        gate_ref[rows, :] = _route(f, rhi_ref, rlo_ref)


def _attn_out_merge_kernel(h_ref, oc_ref, os_ref, ow_ref, gt_ref, wout_ref, gf_ref, rhi_ref,
                           rlo_ref, h2_ref, f_ref, gate_ref):
    gates = gt_ref[...]
    n_gate = gates.shape[1]
    hpg = HEADS_PER_GROUP
    idx = lax.broadcasted_iota(jnp.int32, (n_gate, D_MODEL), 0)
    head = lax.broadcasted_iota(jnp.int32, (n_gate, D_MODEL), 1) // HEAD_DIM
    o = None
    for c, ref in enumerate((oc_ref, os_ref, ow_ref)):
        lane_of_gate = (head // hpg) * LANES + c * hpg + head % hpg
        gate = _exact_dot(gates, (idx == lane_of_gate).astype(BF16))
        term = gate * ref[...]
        o = term if o is None else o + term
    h2 = h_ref[...] + _dot(o.astype(BF16), wout_ref[...])
    h2_ref[...] = h2
    f = _rmsnorm(h2, gf_ref[...])
    f_ref[...] = f.astype(f_ref.dtype)
    gate_ref[...] = _route(f, rhi_ref, rlo_ref)


def _attn_out(h2d, o_list, gates, wout, gf, rhi, rlo):
    n, d = h2d.shape
    tm = min(TOKEN_TILE, n)
    row = lambda w: pl.BlockSpec((tm, w), lambda i: (i, 0))
    merged = len(o_list) == 1
    body = _attn_out_kernel if merged else _attn_out_merge_kernel
    in_specs = [row(d)] + [row(d)] * len(o_list) + ([] if merged else [row(gates.shape[1])])
    in_specs += [_const_spec(wout.shape), _const_spec((1, d)), _const_spec(rhi.shape),
                 _const_spec(rlo.shape)]
    args = [h2d] + list(o_list) + ([] if merged else [gates]) + [wout, gf, rhi, rlo]
    return pl.pallas_call(
        body,
        grid=(n // tm,),
        in_specs=in_specs,
        out_specs=[row(d), row(d), row(LANES)],
        out_shape=[jax.ShapeDtypeStruct((n, d), F32), jax.ShapeDtypeStruct((n, d), BF16),
                   jax.ShapeDtypeStruct((n, LANES), F32)],
        compiler_params=_cparams(("arbitrary",)),
        name="attn_out_route",
    )(*args)


def _moe_kernel(f_ref, h2_ref, gate_ref, p_ref, wg_ref, wu_ref, wd_ref, gp_ref, wpg_ref, wpp_ref,
                o_ref, acc_ref):
    e = pl.program_id(1)

    @pl.when(e == 0)
    def _():
        acc_ref[...] = jnp.zeros_like(acc_ref)

    f = f_ref[...]
    gate = gate_ref[...]
    lane = lax.broadcasted_iota(jnp.int32, gate.shape, 1)
    ge = jnp.sum(jnp.where(lane == e, gate, 0.0), axis=-1, keepdims=True)
    hid = _silu(_dot(f, wg_ref[...])) * _dot(f, wu_ref[...])
    acc_ref[...] += ge * _dot(hid.astype(BF16), wd_ref[...])

    @pl.when(e == pl.num_programs(1) - 1)
    def _():
        o_ref[...] = _ple(h2_ref[...] + acc_ref[...], p_ref[...], gp_ref[...], wpg_ref, wpp_ref)


def _moe(f2d, h2, gate, p2d, wg, wu, wd, gp, wpg, wpp):
    n, d = h2.shape
    tm = min(TOKEN_TILE, n)
    n_e, _, dff = wg.shape
    row = lambda w: pl.BlockSpec((tm, w), lambda i, e: (i, 0))
    return pl.pallas_call(
        _moe_kernel,
        grid=(n // tm, n_e),
        in_specs=[
            row(d), row(d), row(LANES), row(p2d.shape[1]),
            pl.BlockSpec((None, d, dff), lambda i, e: (e, 0, 0)),
            pl.BlockSpec((None, d, dff), lambda i, e: (e, 0, 0)),
            pl.BlockSpec((None, dff, d), lambda i, e: (e, 0, 0)),
            _const_spec((1, d)), _const_spec(wpg.shape), _const_spec(wpp.shape),
        ],
        out_specs=row(d),
        out_shape=jax.ShapeDtypeStruct((n, d), F32),
        scratch_shapes=[pltpu.VMEM((tm, d), F32)],
        compiler_params=_cparams(("arbitrary", "arbitrary")),
        name="moe_ple",
    )(f2d, h2, gate, p2d, wg, wu, wd, gp, wpg, wpp)


def _bucket_thresholds():
    n = jnp.arange(MAX_DISTANCE + 1)
    max_exact = N_BUCKETS // 2
    nf = jnp.maximum(n, 1).astype(F32)
    large = max_exact + (jnp.log(nf / max_exact) / math.log(MAX_DISTANCE / max_exact)
                         * (N_BUCKETS - max_exact)).astype(jnp.int32)
    bucket = jnp.where(n < max_exact, n, jnp.minimum(large, N_BUCKETS - 1))
    return jnp.sum(bucket[None, :] < jnp.arange(N_BUCKETS)[:, None], axis=1).astype(jnp.int32)


def _row(v):
    return v.reshape(1, -1).astype(F32)


def _key_gain_row(k_gain, cols):
    g = jnp.concatenate([k_gain, jnp.ones_like(k_gain)])
    return jnp.tile(g, cols // GROUP_COLS).reshape(1, cols).astype(F32)


def _rows_on_lanes(x):
    lead = x.shape[:-4]
    rows = x.shape[-4]
    nl = len(lead)
    perm = tuple(range(nl)) + (nl + 1, nl + 2, nl + 3, nl)
    return jnp.transpose(x, perm).reshape(*lead, N_KV_HEADS, GROUP_COLS, rows)


def _rows_on_lanes_inverse(xt):
    b, _, rows = xt.shape
    x = xt.reshape(b, N_KV_HEADS, 2, HEAD_DIM, rows)
    return jnp.transpose(x, (0, 4, 1, 2, 3))[None]


def kernel(x_prompt, x_sample, state_conv, cache_cmp, cache_sel, state_win, page_table,
           p_prompt, p_sample, norm_mix, norm_ffn, norm_ple, conv_w_in, conv_w, conv_w_out,
           nsa_w_in, nsa_w_cmp, nsa_q_norm, nsa_k_norm, nsa_w_out, rel_bias,
           ffn_w_gate, ffn_w_up, ffn_w_down, moe_router, moe_w_gate, moe_w_up, moe_w_down,
           ple_w_proj, ple_w_gate):
    bp, sp, d = x_prompt.shape
    bs, ss, _ = x_sample.shape
    n_p = bp * sp
    n_s = bs * ss
    page = cache_cmp.shape[2]
    past_len = page_table.shape[1] * page
    hpg = HEADS_PER_GROUP
    bf = lambda w: w.astype(BF16)

    cw = jnp.zeros((SUBLANES, d), F32).at[:CONV_WIDTH].set(conv_w[0])
    w_in0, w_out0 = bf(conv_w_in[0]), bf(conv_w_out[0])
    g_mix0 = _row(norm_mix[0])
    h_p, tail_p = _mix0_prompt(x_prompt, g_mix0, w_in0, cw, w_out0)
    conv_prompt = tail_p[:, SUBLANES - (CONV_WIDTH - 1):][None]

    st = state_conv[0]
    zeros = jnp.zeros((bs, ss, d), F32)
    s1 = zeros.at[:, 0].set(st[:, 1]).reshape(n_s, d)
    s2 = zeros.at[:, 0].set(st[:, 0]).at[:, 1].set(st[:, 1]).reshape(n_s, d)
    h_s, u_s = _mix0_sample(x_sample.reshape(n_s, d), g_mix0, w_in0, cw, w_out0, s1, s2, ss)
    conv_sample = u_s.reshape(bs, ss, d)[:, ss - (CONV_WIDTH - 1):][None]

    ffn0_w = (_row(norm_ffn[0]), bf(ffn_w_gate[0]), bf(ffn_w_up[0]), bf(ffn_w_down[0]),
              _row(norm_ple[0]), bf(ple_w_gate[0]), bf(ple_w_proj[0]))
    h_p = _ffn0(h_p.reshape(n_p, d), p_prompt[0].reshape(n_p, -1), *ffn0_w)
    h_s = _ffn0(h_s, p_sample[0].reshape(n_s, -1), *ffn0_w)

    q_cols = N_HEADS * HEAD_DIM
    kv_cols = N_BRANCHES * KV_COLS
    w_in1 = nsa_w_in[0]
    wq = bf(w_in1[:, :q_cols])
    wkv = w_in1[:, q_cols:q_cols + kv_cols]
    wkc = bf(wkv[:, :KV_COLS])
    wkvt = bf(wkv.T)
    wv_src = wkv[:, KV_COLS:].reshape(d, 2 * N_KV_HEADS, 2, HEAD_DIM)[:, :, 1]
    wv = bf(jnp.zeros((d, 2 * N_KV_HEADS, LANES), F32).at[:, :, :HEAD_DIM].set(wv_src).reshape(
        d, 2 * KV_COLS))
    wg_src = w_in1[:, q_cols + kv_cols:].reshape(d, N_BRANCHES, N_KV_HEADS, hpg)
    wgt = jnp.zeros((d, N_KV_HEADS, LANES), F32).at[:, :, :N_BRANCHES * hpg].set(
        wg_src.transpose(0, 2, 1, 3).reshape(d, N_KV_HEADS, N_BRANCHES * hpg))
    wgt = bf(wgt.reshape(d, N_KV_HEADS * LANES))
    gi = jnp.arange(MXU_DIM) // HEAD_DIM
    gmat = (gi[:, None] == gi[None, :]).astype(BF16)
    qg = jnp.tile(nsa_q_norm[0], N_HEADS).reshape(1, q_cols).astype(F32)
    kgc = _key_gain_row(nsa_k_norm[0, 0], KV_COLS)
    kgs_col = nsa_k_norm[0, 1].reshape(HEAD_DIM, 1).astype(F32)
    kgw_col = nsa_k_norm[0, 2].reshape(HEAD_DIM, 1).astype(F32)
    proj_w = (_row(norm_mix[1]), wq, wkc, wkvt, wv, wgt, gmat, qg, kgs_col, kgw_col)

    per_tile = MXU_DIM // HEAD_DIM
    wc = nsa_w_cmp[0].reshape(CMP_BLOCK, 2, per_tile, HEAD_DIM, HEAD_DIM)
    w2 = bf(jnp.einsum('lhade,ab->lhadbe', wc, jnp.eye(per_tile, dtype=F32)).reshape(
        CMP_BLOCK, 2, MXU_DIM, MXU_DIM))

    thr = _bucket_thresholds()
    tbl = rel_bias.astype(F32)

    q_p, kvc_p, kvct_p, kvst_p, kvwt_p, ksn_p, kwn_p, vr_p, gt_p = _nsa_proj(h_p, bp, *proj_w)
    kcn_p = _compress_prompt(kvc_p, w2, gmat, kgc)
    o_p = _nsa_prompt_attn(tbl, thr, q_p, kcn_p, ksn_p, kwn_p, vr_p, gt_p, bp, sp)

    q_s, _, kvct_s, kvst_s, kvwt_s, ksn_s, kwn_s, _, gt_s = _nsa_proj(h_s, 1, *proj_w)
    pt_flat = page_table.reshape(-1).astype(jnp.int32)
    kcn_s = _compress_paged(pt_flat, _rows_on_lanes(cache_cmp[0]), w2, gmat, kgc, PAGES_PER_STEP)
    q5 = q_s.astype(F32).reshape(bs, ss, N_KV_HEADS, hpg, HEAD_DIM)
    q_rt = q5.transpose(0, 2, 3, 1, 4).reshape(bs, N_KV_HEADS, hpg * ss, HEAD_DIM)
    q_tr = jnp.zeros((bs, N_KV_HEADS, ss, SUBLANES, HEAD_DIM), F32).at[:, :, :, :hpg].set(
        q5.transpose(0, 2, 1, 3, 4))
    tbl_gr = tbl.T.reshape(N_KV_HEADS, hpg, N_BUCKETS)
    tbl_rt = jnp.repeat(tbl_gr, ss, axis=1)
    tbl_r8 = jnp.zeros((N_KV_HEADS, SUBLANES, N_BUCKETS), F32).at[:, :hpg].set(tbl_gr)
    oc_s, idx = _sample_cmp(thr, q_rt, kcn_s, tbl_rt, bs, past_len, ss)
    n_top_s = min(N_SELECTED - 1, past_len // SEL_BLOCK)
    idx_flat = idx[:, :, :n_top_s + 1].reshape(-1)

    def new_rows(kn_t):
        r = kn_t[0].astype(F32).reshape(N_KV_HEADS, GROUP_COLS, bs, ss).transpose(2, 0, 3, 1)
        return jnp.zeros((bs, N_KV_HEADS, SUBLANES, GROUP_COLS), F32).at[:, :, :ss].set(r)

    win_t = _rows_on_lanes(state_win[0])
    os_s, ow_s = _sample_sel_win(idx_flat, pt_flat, thr, _rows_on_lanes(cache_sel[0]), q_tr,
                                 new_rows(ksn_s), new_rows(kwn_s), win_t, tbl_r8,
                                 kgs_col, kgw_col, bs, past_len, ss)
    oc_s2 = oc_s.reshape(bs, N_KV_HEADS, hpg, ss, HEAD_DIM).transpose(0, 3, 1, 2, 4).reshape(n_s, d)
    to2d = lambda o: o[:, :, :, :hpg].transpose(0, 2, 1, 3, 4).reshape(n_s, d)

    w_out1 = bf(nsa_w_out[0])
    router = jnp.zeros((d, LANES), F32).at[:, :N_EXPERTS].set(moe_router[0])
    r_hi = bf(router)
    r_lo = bf(router - r_hi.astype(F32))
    gf1 = _row(norm_ffn[1])
    moe_w = (bf(moe_w_gate[0]), bf(moe_w_up[0]), bf(moe_w_down[0]), _row(norm_ple[1]),
             bf(ple_w_gate[1]), bf(ple_w_proj[1]))
    h2_p, f_p, gate_p = _attn_out(h_p, [o_p], None, w_out1, gf1, r_hi, r_lo)
    y_p = _moe(f_p, h2_p, gate_p, p_prompt[1].reshape(n_p, -1), *moe_w)
    h2_s, f_s, gate_s = _attn_out(h_s, [oc_s2, to2d(os_s), to2d(ow_s)], gt_s, w_out1, gf1,
                                  r_hi, r_lo)
    y_s = _moe(f_s, h2_s, gate_s, p_sample[1].reshape(n_s, -1), *moe_w)

    sample_rows = lambda xt: _rows_on_lanes_inverse(
        xt[0].reshape(KV_COLS, bs, ss).transpose(1, 0, 2))
    wbp = min(WINDOW, sp)
    win_s_t = jnp.concatenate(
        [win_t, kvwt_s[0].reshape(N_KV_HEADS, GROUP_COLS, bs, ss).transpose(2, 0, 1, 3)],
        axis=3)[..., ss:]
    return (y_p.reshape(bp, sp, d), y_s.reshape(bs, ss, d), conv_prompt, conv_sample,
            _rows_on_lanes_inverse(kvct_p), sample_rows(kvct_s),
            _rows_on_lanes_inverse(kvst_p), sample_rows(kvst_s),
            _rows_on_lanes_inverse(kvwt_p[:, :, sp - wbp:]),
            _rows_on_lanes_inverse(win_s_t.reshape(bs, KV_COLS, -1)))
```
